```python
import math
import jax, jax.numpy as jnp
from jax import lax
import numpy as np


D_MODEL = 1024
BATCH = 2
SEQ = 16384
DEPTH = 4

HEAD_DIM = 64
BLOCK = 128
DIFF_HEADS = 6
DIFF_QK_DIM = HEAD_DIM // 2
DIFF_V_DIM = HEAD_DIM
DIFF_WIDTH = DIFF_HEADS * DIFF_V_DIM
GMLP_GROUPS = 4
GMLP_GROUP_DIM = 64
GMLP_WIDTH = GMLP_GROUPS * GMLP_GROUP_DIM
CHUNK = 128
SWA_HEADS = 6
SWA_KV_HEADS = 2
SWA_GROUP = SWA_HEADS // SWA_KV_HEADS
SWA_WIDTH = SWA_HEADS * HEAD_DIM
WINDOW = 128
MIX_WIDTH = DIFF_WIDTH + GMLP_WIDTH + SWA_WIDTH
IN_SIZES = (DIFF_HEADS * 2 * DIFF_QK_DIM, DIFF_HEADS * 2 * DIFF_QK_DIM, DIFF_WIDTH,
            GMLP_WIDTH, GMLP_WIDTH,
            SWA_WIDTH, SWA_KV_HEADS * HEAD_DIM, SWA_KV_HEADS * HEAD_DIM)
IN_WIDTH = 384 + 384 + 384 + 256 + 256 + 384 + 128 + 128
D_FF = ((-(-8 * D_MODEL // 3) + 255) // 256) * 256
PLE_DIM = 256
N_ATTN_HEADS = DIFF_HEADS + SWA_HEADS
ALIBI_MAX_EXP = 8.0
EPS = 1e-6
NEG_INF = -1e30

kernel_name = 'hybrid_parallel_group_encoder_block'


def rmsnorm(x, g):
    x32 = x.astype(jnp.float32)
    y = x32 * lax.rsqrt(jnp.mean(x32 * x32, axis=-1, keepdims=True) + EPS)
    return (y * g.astype(jnp.float32)).astype(x.dtype)


def alibi_slopes():
    k = jnp.arange(1, N_ATTN_HEADS + 1, dtype=jnp.float32)
    s = jnp.exp2(-ALIBI_MAX_EXP * k / N_ATTN_HEADS)
    return s[SWA_HEADS:], s[:SWA_HEADS]


def diff_attention(q, k, v, lam, lam_init, g_sub, slopes):
    b, s_len = q.shape[0], q.shape[1]
    nb = s_len // BLOCK
    scale = DIFF_QK_DIM ** -0.5
    key_pos = jnp.arange(s_len)
    qb = q.reshape(b, nb, BLOCK, DIFF_HEADS, 2, DIFF_QK_DIM).transpose(1, 0, 2, 3, 4, 5)

    def one_block(args):
        q_blk, j = args
        sc = jnp.einsum('bqhcd,bkhcd->bhcqk', q_blk, k,
                        preferred_element_type=jnp.float32) * scale
        q_pos = j * BLOCK + jnp.arange(BLOCK)
        dist = jnp.abs(q_pos[:, None] - key_pos[None, :]).astype(jnp.float32)
        sc = sc - slopes[None, :, None, None, None] * dist
        pr = jax.nn.softmax(sc, axis=-1)
        attn = pr[:, :, 0] - lam * pr[:, :, 1]
        return jnp.einsum('bhqk,bkhd->bqhd', attn.astype(v.dtype), v)

    o = lax.map(one_block, (qb, jnp.arange(nb)))
    o = o.transpose(1, 0, 2, 3, 4).reshape(b, s_len, DIFF_HEADS, DIFF_V_DIM)
    o = rmsnorm(o, g_sub) * (1.0 - lam_init)
    return o.reshape(b, s_len, DIFF_WIDTH)


def spatial_gating(u, v, ln_g, ln_b, w_s, b_s):
    b, s_len = u.shape[0], u.shape[1]
    v32 = v.astype(jnp.float32)
    mu = jnp.mean(v32, axis=-1, keepdims=True)
    var = jnp.mean(jnp.square(v32 - mu), axis=-1, keepdims=True)
    vn = ((v32 - mu) * lax.rsqrt(var + EPS) * ln_g.astype(jnp.float32)
          + ln_b.astype(jnp.float32)).astype(v.dtype)
    nc = s_len // CHUNK
    vc = vn.reshape(b, nc, CHUNK, GMLP_GROUPS, GMLP_GROUP_DIM)
    mixed = jnp.einsum('gts,bnsgd->bntgd', w_s, vc) + b_s.T[:, :, None]
    return u * mixed.reshape(b, s_len, GMLP_WIDTH)


def window_gqa(q, k, v, sinks, slopes):
    b, s_len = q.shape[0], q.shape[1]
    nb = s_len // BLOCK
    k = k.reshape(b, s_len, SWA_KV_HEADS, HEAD_DIM)
    v = v.reshape(b, s_len, SWA_KV_HEADS, HEAD_DIM)
    pad = ((0, 0), (BLOCK, BLOCK), (0, 0), (0, 0))
    kp = jnp.pad(k, pad).reshape(b, nb + 2, BLOCK, SWA_KV_HEADS, HEAD_DIM)
    vp = jnp.pad(v, pad).reshape(b, nb + 2, BLOCK, SWA_KV_HEADS, HEAD_DIM)
    kb = jnp.concatenate([kp[:, :-2], kp[:, 1:-1], kp[:, 2:]], axis=2)
    vb = jnp.concatenate([vp[:, :-2], vp[:, 1:-1], vp[:, 2:]], axis=2)
    qb = q.reshape(b, nb, BLOCK, SWA_KV_HEADS, SWA_GROUP, HEAD_DIM)
    sc = jnp.einsum('bnqkgd,bnckd->bnkgqc', qb, kb,
                    preferred_element_type=jnp.float32) * (HEAD_DIM ** -0.5)
    rel = jnp.arange(3 * BLOCK)[None, :] - BLOCK - jnp.arange(BLOCK)[:, None]
    dist = jnp.abs(rel)
    key_pos = (jnp.arange(nb)[:, None] - 1) * BLOCK + jnp.arange(3 * BLOCK)[None, :]
    valid = (dist <= WINDOW)[None] & ((key_pos >= 0) & (key_pos < s_len))[:, None, :]
    sl = slopes.reshape(SWA_KV_HEADS, SWA_GROUP)
    sc = sc - sl[:, :, None, None] * dist.astype(jnp.float32)
    sc = jnp.where(valid[None, :, None, None], sc, NEG_INF)
    sink = sinks.astype(jnp.float32).reshape(SWA_KV_HEADS, SWA_GROUP)[:, :, None, None]
    m = jnp.maximum(jnp.max(sc, axis=-1, keepdims=True), sink)
    e = jnp.exp(sc - m)
    pr = e / (jnp.sum(e, axis=-1, keepdims=True) + jnp.exp(sink - m))
    o = jnp.einsum('bnkgqc,bnckd->bnqkgd', pr.astype(v.dtype), vb)
    return o.reshape(b, s_len, SWA_WIDTH)


def setup_inputs(seed: int = 0) -> dict:
    key = jax.random.key(seed)
    ks = jax.random.split(key, 24)

    def nrm(k, shape, scale):
        return jax.random.normal(k, shape, jnp.float32) * scale

    def gain(k, shape):
        return 1.0 + 0.05 * jax.random.normal(k, shape, jnp.float32)

    L = DEPTH
    return {
        'x': nrm(ks[0], (BATCH, SEQ, D_MODEL), 1.0),
        'p': nrm(ks[1], (DEPTH, BATCH, SEQ, PLE_DIM), 1.0),
        'g_pre_mix': gain(ks[2], (L, D_MODEL)),
        'w_in': nrm(ks[3], (L, D_MODEL, IN_WIDTH), D_MODEL ** -0.5),
        'lam_q1': nrm(ks[4], (L, DIFF_QK_DIM), 0.1),
        'lam_k1': nrm(ks[5], (L, DIFF_QK_DIM), 0.1),
        'lam_q2': nrm(ks[6], (L, DIFF_QK_DIM), 0.1),
        'lam_k2': nrm(ks[7], (L, DIFF_QK_DIM), 0.1),
        'g_diff_sub': gain(ks[8], (L, DIFF_V_DIM)),
        'gmlp_ln_g': gain(ks[9], (L, GMLP_WIDTH)),
        'gmlp_ln_b': nrm(ks[10], (L, GMLP_WIDTH), 0.02),
        'w_spatial': nrm(ks[11], (L, GMLP_GROUPS, CHUNK, CHUNK), CHUNK ** -0.5),
        'b_spatial': gain(ks[12], (L, GMLP_GROUPS, CHUNK)),
        'swa_sinks': nrm(ks[13], (L, SWA_HEADS), 0.5),
        'w_out': nrm(ks[14], (L, MIX_WIDTH, D_MODEL), MIX_WIDTH ** -0.5),
        'g_post_mix': gain(ks[15], (L, D_MODEL)),
        'g_pre_ffn': gain(ks[16], (L, D_MODEL)),
        'w_ffn_in': nrm(ks[17], (L, D_MODEL, 2 * D_FF), D_MODEL ** -0.5),
        'w_ffn_out': nrm(ks[18], (L, D_FF, D_MODEL), D_FF ** -0.5),
        'g_post_ffn': gain(ks[19], (L, D_MODEL)),
        'w_ple_up': nrm(ks[20], (L, PLE_DIM, D_MODEL), PLE_DIM ** -0.5),
        'w_ple_gate': nrm(ks[21], (L, D_MODEL, D_MODEL), D_MODEL ** -0.5),
        'g_ple_gate': gain(ks[22], (L, D_MODEL)),
        'g_ple_post': gain(ks[23], (L, D_MODEL)),
    }


def reference(x, p, g_pre_mix, w_in, lam_q1, lam_k1, lam_q2, lam_k2, g_diff_sub,
              gmlp_ln_g, gmlp_ln_b, w_spatial, b_spatial, swa_sinks, w_out, g_post_mix,
              g_pre_ffn, w_ffn_in, w_ffn_out, g_post_ffn, w_ple_up, w_ple_gate,
              g_ple_gate, g_ple_post):
    b, s_len = x.shape[0], x.shape[1]
    diff_slopes, swa_slopes = alibi_slopes()
    split_at = np.cumsum(IN_SIZES)[:-1].tolist()
    for l in range(DEPTH):
        h = rmsnorm(x, g_pre_mix[l])
        proj = h @ w_in[l]
        a_q, a_k, a_v, b_u, b_v, c_q, c_k, c_v = jnp.split(proj, split_at, axis=-1)
        lam_init = 0.8 - 0.6 * math.exp(-0.3 * l)
        f32 = jnp.float32
        lam = (jnp.exp(jnp.sum(lam_q1[l].astype(f32) * lam_k1[l].astype(f32)))
               - jnp.exp(jnp.sum(lam_q2[l].astype(f32) * lam_k2[l].astype(f32)))
               + lam_init)
        y_a = diff_attention(
            a_q.reshape(b, s_len, DIFF_HEADS, 2, DIFF_QK_DIM),
            a_k.reshape(b, s_len, DIFF_HEADS, 2, DIFF_QK_DIM),
            a_v.reshape(b, s_len, DIFF_HEADS, DIFF_V_DIM),
            lam, lam_init, g_diff_sub[l], diff_slopes)
        y_b = spatial_gating(b_u, b_v, gmlp_ln_g[l], gmlp_ln_b[l], w_spatial[l], b_spatial[l])
        y_c = window_gqa(c_q, c_k, c_v, swa_sinks[l], swa_slopes)
        y = jnp.concatenate([y_a, y_b, y_c], axis=-1) @ w_out[l]
        x = x + rmsnorm(y, g_post_mix[l])
        h = rmsnorm(x, g_pre_ffn[l])
        gate, up = jnp.split(h @ w_ffn_in[l], 2, axis=-1)
        f = (jax.nn.silu(gate) * up) @ w_ffn_out[l]
        x = x + rmsnorm(f, g_post_ffn[l])
        e = p[l] @ w_ple_up[l]
        g = jax.nn.sigmoid(rmsnorm(x, g_ple_gate[l]) @ w_ple_gate[l])
        x = x + rmsnorm(e * g, g_ple_post[l])
    return x
```

```python
import functools
import math

import numpy as np
import jax
import jax.numpy as jnp
from jax import lax
from jax.experimental import pallas as pl
from jax.experimental.pallas import tpu as pltpu

D_MODEL = 1024
HEAD_DIM = 64
DIFF_HEADS = 6
DIFF_QK_DIM = 32
DIFF_V_DIM = 64
DIFF_WIDTH = DIFF_HEADS * DIFF_V_DIM
GMLP_GROUPS = 4
GMLP_GROUP_DIM = 64
GMLP_WIDTH = GMLP_GROUPS * GMLP_GROUP_DIM
CHUNK = 128
SWA_HEADS = 6
SWA_KV_HEADS = 2
SWA_GROUP = SWA_HEADS // SWA_KV_HEADS
SWA_WIDTH = SWA_HEADS * HEAD_DIM
WINDOW = 128
SWA_BLOCK = 128
D_FF = 2816
PLE_DIM = 256
N_ATTN_HEADS = DIFF_HEADS + SWA_HEADS
ALIBI_MAX_EXP = 8.0
EPS = 1e-6
NEG_INF = -1e30
LOG2E = 1.4426950408889634

LANES = 128
MXU_DIM_V7X = 256
VMEM_LIMIT_BYTES_V7X = 56 * 1024 * 1024

TOKEN_TILE = 512
ATT_BLOCK = MXU_DIM_V7X
KDIM = 2 * 64
AUG_OFF = DIFF_QK_DIM
VT_ROWS = 80
FF_CHUNK = 256

F32 = jnp.float32
BF16 = jnp.bfloat16


def _bf16_round_np(x):
    u = np.asarray(x, np.float32).view(np.uint32).astype(np.uint64)
    r = ((u >> 16) & 1) + 0x7FFF
    return ((u + r) & 0xFFFF0000).astype(np.uint32).view(np.float32)


def _alibi_slopes_np():
    k = np.arange(1, N_ATTN_HEADS + 1, dtype=np.float64)
    s = np.exp2(-ALIBI_MAX_EXP * k / N_ATTN_HEADS).astype(np.float32)
    return s[SWA_HEADS:], s[:SWA_HEADS]


def _split3_bf16(v):
    v = np.asarray(v, np.float32)
    hi = _bf16_round_np(v)
    mid = _bf16_round_np(v - hi)
    lo = _bf16_round_np(v - hi - mid)
    return hi, mid, lo


def _diff_bias_constants(token_tile):
    slopes, _ = _alibi_slopes_np()
    slope2 = (slopes.astype(np.float64) * LOG2E).astype(np.float32)
    hi, mid, lo = _split3_bf16(slope2)
    parts = np.stack([hi, mid, lo], axis=1)
    rel = np.arange(ATT_BLOCK, dtype=np.float32)
    qaug = np.zeros((DIFF_HEADS, KDIM, ATT_BLOCK), np.float32)
    kaug = np.zeros((DIFF_HEADS, ATT_BLOCK, KDIM), np.float32)
    for c in range(2):
        base = c * 64 + AUG_OFF
        for t in range(3):
            qaug[:, base + t, :] = rel[None, :]
            qaug[:, base + 3 + t, :] = -parts[:, t][:, None]
            kaug[:, :, base + t] = parts[:, t][:, None]
            kaug[:, :, base + 3 + t] = rel[None, :]
    kaug = np.tile(kaug, (1, token_tile // ATT_BLOCK, 1))
    dist = np.abs(rel[:, None] - rel[None, :])
    diag = -(slope2[:, None, None] * dist[None])
    return slope2, qaug, kaug, diag.astype(np.float32)


def _rms(x, g):
    return x * lax.rsqrt(jnp.mean(x * x, axis=-1, keepdims=True) + EPS) * g


def _dot(a, b):
    return jnp.dot(a, b, preferred_element_type=F32)


def _dot_nt(a, b):
    return lax.dot_general(a, b, (((1,), (1,)), ((), ())), preferred_element_type=F32)


def _dot_tn(a, b):
    return lax.dot_general(a, b, (((0,), (0,)), ((), ())), preferred_element_type=F32)


def _const_spec(shape):
    nd = len(shape)
    return pl.BlockSpec(shape, lambda *_: (0,) * nd, pipeline_mode=pl.Buffered(1))


def _params(sem):
    return pltpu.CompilerParams(dimension_semantics=sem, vmem_limit_bytes=VMEM_LIMIT_BYTES_V7X)


N_STD = DIFF_HEADS * KDIM + 2 * GMLP_WIDTH + SWA_WIDTH + 2 * SWA_KV_HEADS * HEAD_DIM
N_TR = DIFF_HEADS * KDIM + DIFF_HEADS * VT_ROWS
Q_SCALE = (DIFF_QK_DIM ** -0.5) * LOG2E


def _proj_kernel(x_ref, g_ref, wstd_ref, wtr_ref, kaug_ref, lamp_ref,
                 k_ref, qt_ref, vt_ref, bu_ref, bv_ref, cq_ref, ck_ref, cv_ref, lam_ref, *, lam_init):
    tn = x_ref.shape[1]
    nsub = tn // ATT_BLOCK
    h = _rms(x_ref[0], g_ref[...]).astype(BF16)
    r1 = _dot(h, wstd_ref[...])
    for hh in range(DIFF_HEADS):
        kk = (r1[:, hh * KDIM:(hh + 1) * KDIM] + kaug_ref[hh]).astype(BF16)
        for j in range(nsub):
            k_ref[0, hh, j] = kk[j * ATT_BLOCK:(j + 1) * ATT_BLOCK]
    o = DIFF_HEADS * KDIM
    bu_ref[0] = r1[:, o:o + GMLP_WIDTH]
    o += GMLP_WIDTH
    bv_ref[0] = r1[:, o:o + GMLP_WIDTH]
    o += GMLP_WIDTH
    cq_ref[0] = r1[:, o:o + SWA_WIDTH].astype(BF16)
    o += SWA_WIDTH
    ck_ref[0] = r1[:, o:o + 128].astype(BF16)
    o += 128
    cv_ref[0] = r1[:, o:o + 128].astype(BF16)

    r2 = _dot_nt(wtr_ref[...], h)
    ones_row = jnp.where(lax.broadcasted_iota(jnp.int32, (VT_ROWS, ATT_BLOCK), 0) == DIFF_V_DIM, 1.0, 0.0)
    vo = DIFF_HEADS * KDIM
    for hh in range(DIFF_HEADS):
        qt_ref[0, hh] = (r2[hh * KDIM:(hh + 1) * KDIM] * Q_SCALE).astype(BF16)
        vv = r2[vo + hh * VT_ROWS: vo + (hh + 1) * VT_ROWS]
        for j in range(nsub):
            vt_ref[0, hh, j] = (vv[:, j * ATT_BLOCK:(j + 1) * ATT_BLOCK] + ones_row).astype(BF16)

    lp = lamp_ref[...]
    s1 = jnp.sum(lp[0:1] * lp[1:2], axis=-1, keepdims=True)
    s2 = jnp.sum(lp[2:3] * lp[3:4], axis=-1, keepdims=True)
    lam = jnp.exp(s1) - jnp.exp(s2) + lam_init
    lam_ref[...] = jnp.broadcast_to(lam, lam_ref.shape)


def _proj_call(x, g, wstd, wtr, kaug, lamp, lam_init):
    b, s, d = x.shape
    tn = TOKEN_TILE
    nsub = tn // ATT_BLOCK
    nkb = s // ATT_BLOCK
    grid = (b, s // tn)
    tok = lambda bi, si: (bi, si, 0)
    out_shape = (
        jax.ShapeDtypeStruct((b, DIFF_HEADS, nkb, ATT_BLOCK, KDIM), BF16),
        jax.ShapeDtypeStruct((b, DIFF_HEADS, KDIM, s), BF16),
        jax.ShapeDtypeStruct((b, DIFF_HEADS, nkb, VT_ROWS, ATT_BLOCK), BF16),
        jax.ShapeDtypeStruct((b, s, GMLP_WIDTH), F32),
        jax.ShapeDtypeStruct((b, s, GMLP_WIDTH), F32),
        jax.ShapeDtypeStruct((b, s, SWA_WIDTH), BF16),
        jax.ShapeDtypeStruct((b, s, 128), BF16),
        jax.ShapeDtypeStruct((b, s, 128), BF16),
        jax.ShapeDtypeStruct((8, LANES), F32),
    )
    out_specs = (
        pl.BlockSpec((1, DIFF_HEADS, nsub, ATT_BLOCK, KDIM), lambda bi, si: (bi, 0, si, 0, 0)),
        pl.BlockSpec((1, DIFF_HEADS, KDIM, tn), lambda bi, si: (bi, 0, 0, si)),
        pl.BlockSpec((1, DIFF_HEADS, nsub, VT_ROWS, ATT_BLOCK), lambda bi, si: (bi, 0, si, 0, 0)),
        pl.BlockSpec((1, tn, GMLP_WIDTH), tok),
        pl.BlockSpec((1, tn, GMLP_WIDTH), tok),
        pl.BlockSpec((1, tn, SWA_WIDTH), tok),
        pl.BlockSpec((1, tn, 128), tok),
        pl.BlockSpec((1, tn, 128), tok),
        pl.BlockSpec((8, LANES), lambda bi, si: (0, 0)),
    )
    in_specs = [
        pl.BlockSpec((1, tn, d), tok),
        _const_spec(g.shape),
        _const_spec(wstd.shape),
        _const_spec(wtr.shape),
        _const_spec(kaug.shape),
        _const_spec(lamp.shape),
    ]
    return pl.pallas_call(
        functools.partial(_proj_kernel, lam_init=lam_init),
        grid=grid, in_specs=in_specs, out_specs=out_specs, out_shape=out_shape,
        compiler_params=_params(("arbitrary", "arbitrary")),
        name="proj",
    )(x, g, wstd, wtr, kaug, lamp)


def _diff_attn_kernel(slope_ref, qt_ref, k_ref, vt_ref, qaug_ref, diag_ref, lam_ref, gsub_ref,
                      o_ref, m_ref, acc_ref, *, out_scale):
    hh = pl.program_id(1)
    qi = pl.program_id(2)
    npairs = k_ref.shape[2] // 2
    blk = ATT_BLOCK
    slope_blk = slope_ref[hh] * float(blk)

    qt = qt_ref[0, 0].astype(F32)
    qaug = qaug_ref[0]
    row = lax.broadcasted_iota(jnp.int32, (KDIM, blk), 0)
    comp_mask = (row < 64, row >= 64)
    zero = jnp.zeros_like(qt)
    wf_left = [jnp.where(mk, qt - qaug, zero) for mk in comp_mask]
    wf_right = [jnp.where(mk, qt + qaug, zero) for mk in comp_mask]
    wf_diag = [jnp.where(mk, qt, zero) for mk in comp_mask]
    w_left = [w.astype(BF16) for w in wf_left]
    w_right = [w.astype(BF16) for w in wf_right]

    m_ref[...] = jnp.full(m_ref.shape, NEG_INF, F32)
    acc_ref[...] = jnp.zeros(acc_ref.shape, F32)

    def pair_step(pb, w_a, w_b, bias_a, bias_b):
        kb = 2 * pb
        k_a = k_ref[0, 0, kb]
        k_b = k_ref[0, 0, kb + 1]
        vt = jnp.concatenate([vt_ref[0, 0, kb], vt_ref[0, 0, kb + 1]], axis=1)
        c_a = -slope_blk * jnp.abs(qi - kb).astype(F32)
        c_b = -slope_blk * jnp.abs(qi - kb - 1).astype(F32)
        for c in range(2):
            s_a = _dot(k_a, w_a[c])
            s_b = _dot(k_b, w_b[c])
            if bias_a is not None:
                s_a = s_a + bias_a
                s_b = s_b + bias_b
            cm = jnp.maximum(jnp.max(s_a, axis=0, keepdims=True) + c_a,
                             jnp.max(s_b, axis=0, keepdims=True) + c_b)
            m_old = m_ref[c]
            m_new = jnp.maximum(m_old, cm)
            alpha = jnp.exp2(m_old - m_new)
            p_a = jnp.exp2(s_a - (m_new - c_a))
            p_b = jnp.exp2(s_b - (m_new - c_b))
            p = jnp.concatenate([p_a, p_b], axis=0).astype(BF16)
            acc_ref[c] = acc_ref[c] * alpha + _dot(vt, p)
            m_ref[c] = m_new

    pq = qi // 2

    def left_body(pb, carry):
        pair_step(pb, w_left, w_left, None, None)
        return carry

    def right_body(pb, carry):
        pair_step(pb, w_right, w_right, None, None)
        return carry

    lax.fori_loop(0, pq, left_body, 0)

    even = (qi % 2) == 0
    dg = diag_ref[0]
    zb = jnp.zeros_like(dg)
    w_a = [jnp.where(even, wf_diag[c], wf_left[c]).astype(BF16) for c in range(2)]
    w_b = [jnp.where(even, wf_right[c], wf_diag[c]).astype(BF16) for c in range(2)]
    pair_step(pq, w_a, w_b, jnp.where(even, dg, zb), jnp.where(even, zb, dg))

    lax.fori_loop(pq + 1, npairs, right_body, 0)

    a0 = acc_ref[0]
    a1 = acc_ref[1]
    o0 = a0[0:DIFF_V_DIM] / a0[DIFF_V_DIM:DIFF_V_DIM + 1]
    o1 = a1[0:DIFF_V_DIM] / a1[DIFF_V_DIM:DIFF_V_DIM + 1]
    lam = lam_ref[0:1, 0:1]
    o = o0 - lam * o1
    ms = jnp.mean(o * o, axis=0, keepdims=True)
    y = o * lax.rsqrt(ms + EPS) * gsub_ref[...] * out_scale
    o_ref[0] = y.astype(o_ref.dtype)


def _diff_attn_call(slope2, qt, kblk, vtblk, qaug, diag, lam_tile, gsub_b, out_scale):
    b, nh, kdim, s = qt.shape
    nkb = kblk.shape[2]
    blk = ATT_BLOCK
    grid = (b, nh, s // blk)
    in_specs = [
        pl.BlockSpec(memory_space=pltpu.SMEM),
        pl.BlockSpec((1, 1, kdim, blk), lambda bi, hi, qi: (bi, hi, 0, qi)),
        pl.BlockSpec((1, 1, nkb, blk, kdim), lambda bi, hi, qi: (bi, hi, 0, 0, 0)),
        pl.BlockSpec((1, 1, nkb, VT_ROWS, blk), lambda bi, hi, qi: (bi, hi, 0, 0, 0)),
        pl.BlockSpec((1, kdim, blk), lambda bi, hi, qi: (hi, 0, 0)),
        pl.BlockSpec((1, blk, blk), lambda bi, hi, qi: (hi, 0, 0)),
        _const_spec(lam_tile.shape),
        _const_spec(gsub_b.shape),
    ]
    return pl.pallas_call(
        functools.partial(_diff_attn_kernel, out_scale=out_scale),
        grid=grid, in_specs=in_specs,
        out_specs=pl.BlockSpec((1, DIFF_V_DIM, blk), lambda bi, hi, qi: (bi, hi, qi)),
        out_shape=jax.ShapeDtypeStruct((b, nh * DIFF_V_DIM, s), BF16),
        scratch_shapes=[pltpu.VMEM((2, 1, blk), F32), pltpu.VMEM((2, VT_ROWS, blk), F32)],
        compiler_params=_params(("arbitrary", "arbitrary", "arbitrary")),
        name="diff_attn",
    )(slope2, qt, kblk, vtblk, qaug, diag, lam_tile, gsub_b)


def _swa_kernel(slope_ref, sink_ref, q_ref, kp_ref, kc_ref, kn_ref, vp_ref, vc_ref, vn_ref, o_ref, *, seq_len):
    qi = pl.program_id(1)
    blk = SWA_BLOCK
    q = q_ref[0]
    k3 = jnp.concatenate([kp_ref[0], kc_ref[0], kn_ref[0]], axis=0)
    v3 = jnp.concatenate([vp_ref[0], vc_ref[0], vn_ref[0]], axis=0)
    t = lax.broadcasted_iota(jnp.int32, (blk, 3 * blk), 0)
    cidx = lax.broadcasted_iota(jnp.int32, (blk, 3 * blk), 1)
    dist = jnp.abs(cidx - blk - t)
    key_pos = (qi - 1) * blk + cidx
    valid = (dist <= WINDOW) & (key_pos >= 0) & (key_pos < seq_len)
    distf = dist.astype(F32)
    outs = []
    for kh in range(SWA_KV_HEADS):
        kk = k3[:, kh * HEAD_DIM:(kh + 1) * HEAD_DIM]
        vv = v3[:, kh * HEAD_DIM:(kh + 1) * HEAD_DIM]
        for g in range(SWA_GROUP):
            hq = kh * SWA_GROUP + g
            qh = q[:, hq * HEAD_DIM:(hq + 1) * HEAD_DIM]
            sc = _dot_nt(qh, kk) * (HEAD_DIM ** -0.5)
            sc = sc - slope_ref[hq] * distf
            sc = jnp.where(valid, sc, NEG_INF)
            sink = sink_ref[hq]
            m = jnp.maximum(jnp.max(sc, axis=-1, keepdims=True), sink)
            e = jnp.exp(sc - m)
            pr = e / (jnp.sum(e, axis=-1, keepdims=True) + jnp.exp(sink - m))
            outs.append(_dot(pr.astype(BF16), vv))
    o_ref[0] = jnp.concatenate(outs, axis=-1).astype(o_ref.dtype)


def _swa_call(slopes, sinks, cq, ck, cv):
    b, s, _ = cq.shape
    blk = SWA_BLOCK
    nb = s // blk
    cur = lambda bi, qi: (bi, qi, 0)
    prev = lambda bi, qi: (bi, jnp.maximum(qi - 1, 0), 0)
    nxt = lambda bi, qi: (bi, jnp.minimum(qi + 1, nb - 1), 0)
    kv = lambda im: pl.BlockSpec((1, blk, 128), im)
    in_specs = [
        pl.BlockSpec(memory_space=pltpu.SMEM),
        pl.BlockSpec(memory_space=pltpu.SMEM),
        pl.BlockSpec((1, blk, SWA_WIDTH), cur),
        kv(prev), kv(cur), kv(nxt), kv(prev), kv(cur), kv(nxt),
    ]
    return pl.pallas_call(
        functools.partial(_swa_kernel, seq_len=s),
        grid=(b, nb), in_specs=in_specs,
        out_specs=pl.BlockSpec((1, blk, SWA_WIDTH), cur),
        out_shape=jax.ShapeDtypeStruct((b, s, SWA_WIDTH), BF16),
        compiler_params=_params(("arbitrary", "arbitrary")),
        name="swa",
    )(slopes, sinks, cq, ck, ck, ck, cv, cv, cv)


def _mix_out_kernel(x_ref, yat_ref, bu_ref, bv_ref, yc_ref, lng_ref, lnb_ref, ws_ref, bs_ref,
                    wa_ref, wb_ref, wc_ref, g_ref, o_ref):
    tn = x_ref.shape[1]
    v = bv_ref[0]
    mu = jnp.mean(v, axis=-1, keepdims=True)
    var = jnp.mean(jnp.square(v - mu), axis=-1, keepdims=True)
    vn = ((v - mu) * lax.rsqrt(var + EPS) * lng_ref[...] + lnb_ref[...]).astype(BF16)
    lane_group = lax.broadcasted_iota(jnp.int32, (CHUNK, GMLP_WIDTH), 1) // GMLP_GROUP_DIM
    u = bu_ref[0]
    yb = []
    for c in range(tn // CHUNK):
        vc = vn[c * CHUNK:(c + 1) * CHUNK]
        mixed = bs_ref[...]
        for g in range(GMLP_GROUPS):
            mixed = mixed + jnp.where(lane_group == g, _dot(ws_ref[g], vc), 0.0)
        yb.append(u[c * CHUNK:(c + 1) * CHUNK] * mixed)
    yb = jnp.concatenate(yb, axis=0).astype(BF16)
    y = _dot_tn(yat_ref[0], wa_ref[...]) + _dot(yb, wb_ref[...]) + _dot(yc_ref[0], wc_ref[...])
    o_ref[0] = x_ref[0] + _rms(y, g_ref[...])


def _mix_out_call(x, yat, bu, bv, yc, lng, lnb, ws, bs, wa, wb, wc, g):
    b, s, d = x.shape
    tn = TOKEN_TILE
    tok = lambda bi, si: (bi, si, 0)
    in_specs = [
        pl.BlockSpec((1, tn, d), tok),
        pl.BlockSpec((1, DIFF_WIDTH, tn), lambda bi, si: (bi, 0, si)),
        pl.BlockSpec((1, tn, GMLP_WIDTH), tok),
        pl.BlockSpec((1, tn, GMLP_WIDTH), tok),
        pl.BlockSpec((1, tn, SWA_WIDTH), tok),
    ] + [_const_spec(a.shape) for a in (lng, lnb, ws, bs, wa, wb, wc, g)]
    return pl.pallas_call(
        _mix_out_kernel,
        grid=(b, s // tn), in_specs=in_specs,
        out_specs=pl.BlockSpec((1, tn, d), tok),
        out_shape=jax.ShapeDtypeStruct((b, s, d), F32),
        compiler_params=_params(("arbitrary", "arbitrary")),
        name="mix_out",
    )(x, yat, bu, bv, yc, lng, lnb, ws, bs, wa, wb, wc, g)


def _sigmoid(z):
    return 1.0 / (1.0 + jnp.exp(-z))


def _ffn_ple_kernel(x_ref, p_ref, gpre_ref, wg_ref, wu_ref, wo_ref, gpost_ref,
                    wup_ref, wgate_ref, ggate_ref, gple_ref, o_ref):
    x = x_ref[0]
    h = _rms(x, gpre_ref[...]).astype(BF16)
    f = jnp.zeros(x.shape, F32)
    for j in range(D_FF // FF_CHUNK):
        sl = slice(j * FF_CHUNK, (j + 1) * FF_CHUNK)
        gate = _dot(h, wg_ref[:, sl])
        up = _dot(h, wu_ref[:, sl])
        a = (gate * _sigmoid(gate) * up).astype(BF16)
        f = f + _dot(a, wo_ref[sl, :])
    x = x + _rms(f, gpost_ref[...])
    e = _dot(p_ref[0, 0].astype(BF16), wup_ref[...])
    gt = _sigmoid(_dot(_rms(x, ggate_ref[...]).astype(BF16), wgate_ref[...]))
    o_ref[0] = x + _rms(e * gt, gple_ref[...])


def _ffn_ple_call(x, p, layer, gpre, wg, wu, wo, gpost, wup, wgate, ggate, gple):
    b, s, d = x.shape
    tn = TOKEN_TILE
    tok = lambda bi, si: (bi, si, 0)
    in_specs = [
        pl.BlockSpec((1, tn, d), tok),
        pl.BlockSpec((1, 1, tn, PLE_DIM), lambda bi, si: (layer, bi, si, 0)),
    ] + [_const_spec(a.shape) for a in (gpre, wg, wu, wo, gpost, wup, wgate, ggate, gple)]
    return pl.pallas_call(
        _ffn_ple_kernel,
        grid=(b, s // tn), in_specs=in_specs,
        out_specs=pl.BlockSpec((1, tn, d), tok),
        out_shape=jax.ShapeDtypeStruct((b, s, d), F32),
        compiler_params=_params(("arbitrary", "arbitrary")),
        name="ffn_ple",
    )(x, p, gpre, wg, wu, wo, gpost, wup, wgate, ggate, gple)


def _prep_in_weights(w):
    d = w.shape[0]
    aq = w[:, 0:384].reshape(d, DIFF_HEADS, 2, DIFF_QK_DIM)
    ak = w[:, 384:768].reshape(d, DIFF_HEADS, 2, DIFF_QK_DIM)
    av = w[:, 768:1152].reshape(d, DIFF_HEADS, DIFF_V_DIM)
    pad_qk = ((0, 0), (0, 0), (0, 0), (0, 64 - DIFF_QK_DIM))
    wk = jnp.pad(ak, pad_qk).reshape(d, DIFF_HEADS * KDIM)
    wq = jnp.pad(aq, pad_qk).reshape(d, DIFF_HEADS * KDIM)
    wv = jnp.pad(av, ((0, 0), (0, 0), (0, VT_ROWS - DIFF_V_DIM))).reshape(d, DIFF_HEADS * VT_ROWS)
    wstd = jnp.concatenate([wk, w[:, 1152:]], axis=1).astype(BF16)
    wtr = jnp.concatenate([wq, wv], axis=1).T.astype(BF16)
    return wstd, wtr


def kernel(x, p, g_pre_mix, w_in, lam_q1, lam_k1, lam_q2, lam_k2, g_diff_sub, gmlp_ln_g, gmlp_ln_b,
           w_spatial, b_spatial, swa_sinks, w_out, g_post_mix, g_pre_ffn, w_ffn_in, w_ffn_out,
           g_post_ffn, w_ple_up, w_ple_gate, g_ple_gate, g_ple_post):
    b, s, d = x.shape
    depth = w_in.shape[0]
    assert d == D_MODEL and s % (2 * ATT_BLOCK) == 0 and s % TOKEN_TILE == 0

    slope2_np, qaug_np, kaug_np, diag_np = _diff_bias_constants(TOKEN_TILE)
    _, swa_slopes_np = _alibi_slopes_np()
    slope2 = jnp.asarray(slope2_np)
    qaug = jnp.asarray(qaug_np)
    kaug = jnp.asarray(kaug_np)
    diag = jnp.asarray(diag_np)
    swa_slopes = jnp.asarray(swa_slopes_np)
    row = lambda a: a.reshape(1, -1).astype(F32)

    for l in range(depth):
        lam_init = 0.8 - 0.6 * math.exp(-0.3 * l)
        wstd, wtr = _prep_in_weights(w_in[l])
        lamp = jnp.stack([lam_q1[l], lam_k1[l], lam_q2[l], lam_k2[l]]).astype(F32)
        kblk, qt, vtblk, bu, bv, cq, ck, cv, lam_tile = _proj_call(
            x, row(g_pre_mix[l]), wstd, wtr, kaug, lamp, lam_init)

        gsub_b = jnp.broadcast_to(g_diff_sub[l].astype(F32)[:, None], (DIFF_V_DIM, ATT_BLOCK))
        yat = _diff_attn_call(slope2, qt, kblk, vtblk, qaug, diag, lam_tile, gsub_b, 1.0 - lam_init)
        yc = _swa_call(swa_slopes, swa_sinks[l].astype(F32), cq, ck, cv)

        bs = jnp.broadcast_to(b_spatial[l].T[:, :, None], (CHUNK, GMLP_GROUPS, GMLP_GROUP_DIM))
        bs = bs.reshape(CHUNK, GMLP_WIDTH).astype(F32)
        wo = w_out[l].astype(BF16)
        x = _mix_out_call(
            x, yat, bu, bv, yc, row(gmlp_ln_g[l]), row(gmlp_ln_b[l]), w_spatial[l].astype(BF16), bs,
            wo[0:DIFF_WIDTH], wo[DIFF_WIDTH:DIFF_WIDTH + GMLP_WIDTH], wo[DIFF_WIDTH + GMLP_WIDTH:],
            row(g_post_mix[l]))

        wfi = w_ffn_in[l].astype(BF16)
        x = _ffn_ple_call(
            x, p, l, row(g_pre_ffn[l]), wfi[:, :D_FF], wfi[:, D_FF:], w_ffn_out[l].astype(BF16),
            row(g_post_ffn[l]), w_ple_up[l].astype(BF16), w_ple_gate[l].astype(BF16),
            row(g_ple_gate[l]), row(g_ple_post[l]))
    return x
```

```python
import functools
import math

import numpy as np
import jax
import jax.numpy as jnp
from jax import lax
from jax.experimental import pallas as pl
from jax.experimental.pallas import tpu as pltpu

D_MODEL = 1024
HEAD_DIM = 64
DIFF_HEADS = 6
DIFF_QK_DIM = 32
DIFF_V_DIM = 64
DIFF_WIDTH = DIFF_HEADS * DIFF_V_DIM
GMLP_GROUPS = 4
GMLP_GROUP_DIM = 64
GMLP_WIDTH = GMLP_GROUPS * GMLP_GROUP_DIM
CHUNK = 128
SWA_HEADS = 6
SWA_KV_HEADS = 2
SWA_GROUP = SWA_HEADS // SWA_KV_HEADS
SWA_WIDTH = SWA_HEADS * HEAD_DIM
WINDOW = 128
SWA_BLOCK = 128
D_FF = 2816
PLE_DIM = 256
N_ATTN_HEADS = DIFF_HEADS + SWA_HEADS
ALIBI_MAX_EXP = 8.0
EPS = 1e-6
NEG_INF = -1e30
LOG2E = 1.4426950408889634

LANES = 128
MXU_DIM_V7X = 256
VMEM_LIMIT_BYTES_V7X = 56 * 1024 * 1024

TOKEN_TILE = 512
ATT_BLOCK = MXU_DIM_V7X
ATT_QBLOCK = 2 * ATT_BLOCK
KDIM = 2 * 64
AUG_OFF = DIFF_QK_DIM
VT_ROWS = 80
FF_CHUNK = 256

F32 = jnp.float32
BF16 = jnp.bfloat16


def _bf16_round_np(x):
    u = np.asarray(x, np.float32).view(np.uint32).astype(np.uint64)
    r = ((u >> 16) & 1) + 0x7FFF
    return ((u + r) & 0xFFFF0000).astype(np.uint32).view(np.float32)


def _alibi_slopes_np():
    k = np.arange(1, N_ATTN_HEADS + 1, dtype=np.float64)
    s = np.exp2(-ALIBI_MAX_EXP * k / N_ATTN_HEADS).astype(np.float32)
    return s[SWA_HEADS:], s[:SWA_HEADS]


def _split3_bf16(v):
    v = np.asarray(v, np.float32)
    hi = _bf16_round_np(v)
    mid = _bf16_round_np(v - hi)
    lo = _bf16_round_np(v - hi - mid)
    return hi, mid, lo


def _diff_bias_constants(token_tile):
    slopes, _ = _alibi_slopes_np()
    slope2 = (slopes.astype(np.float64) * LOG2E).astype(np.float32)
    hi, mid, lo = _split3_bf16(slope2)
    parts = np.stack([hi, mid, lo], axis=1)
    rel = np.arange(ATT_BLOCK, dtype=np.float32)
    qaug = np.zeros((DIFF_HEADS, KDIM, ATT_BLOCK), np.float32)
    kaug = np.zeros((DIFF_HEADS, ATT_BLOCK, KDIM), np.float32)
    for c in range(2):
        base = c * 64 + AUG_OFF
        for t in range(3):
            qaug[:, base + t, :] = rel[None, :]
            qaug[:, base + 3 + t, :] = -parts[:, t][:, None]
            kaug[:, :, base + t] = parts[:, t][:, None]
            kaug[:, :, base + 3 + t] = rel[None, :]
    kaug = np.tile(kaug, (1, token_tile // ATT_BLOCK, 1))
    qaug = np.tile(qaug, (1, 1, ATT_QBLOCK // ATT_BLOCK))
    dist = np.abs(rel[:, None] - rel[None, :])
    diag = -(slope2[:, None, None] * dist[None])
    return slope2, qaug, kaug, diag.astype(np.float32)


def _rms(x, g):
    return x * lax.rsqrt(jnp.mean(x * x, axis=-1, keepdims=True) + EPS) * g


def _dot(a, b):
    return jnp.dot(a, b, preferred_element_type=F32)


def _dot_nt(a, b):
    return lax.dot_general(a, b, (((1,), (1,)), ((), ())), preferred_element_type=F32)


def _dot_tn(a, b):
    return lax.dot_general(a, b, (((0,), (0,)), ((), ())), preferred_element_type=F32)


def _const_spec(shape):
    nd = len(shape)
    return pl.BlockSpec(shape, lambda *_: (0,) * nd, pipeline_mode=pl.Buffered(1))


def _params(sem, flags=None):
    return pltpu.CompilerParams(dimension_semantics=sem, vmem_limit_bytes=VMEM_LIMIT_BYTES_V7X, flags=flags)


N_STD = DIFF_HEADS * KDIM + 2 * GMLP_WIDTH + SWA_WIDTH + 2 * SWA_KV_HEADS * HEAD_DIM
N_TR = DIFF_HEADS * KDIM + DIFF_HEADS * VT_ROWS
Q_SCALE = (DIFF_QK_DIM ** -0.5) * LOG2E


def _proj_kernel(x_ref, g_ref, wstd_ref, wtr_ref, kaug_ref, lamp_ref,
                 k_ref, qt_ref, vt_ref, bu_ref, bv_ref, cq_ref, ck_ref, cv_ref, lam_ref, *, lam_init):
    tn = x_ref.shape[1]
    nsub = tn // ATT_BLOCK
    h = _rms(x_ref[0], g_ref[...]).astype(BF16)
    r1 = _dot(h, wstd_ref[...])
    for hh in range(DIFF_HEADS):
        kk = (r1[:, hh * KDIM:(hh + 1) * KDIM] + kaug_ref[hh]).astype(BF16)
        for j in range(nsub):
            k_ref[0, hh, j] = kk[j * ATT_BLOCK:(j + 1) * ATT_BLOCK]
    o = DIFF_HEADS * KDIM
    bu_ref[0] = r1[:, o:o + GMLP_WIDTH]
    o += GMLP_WIDTH
    bv_ref[0] = r1[:, o:o + GMLP_WIDTH]
    o += GMLP_WIDTH
    cq_ref[0] = r1[:, o:o + SWA_WIDTH].astype(BF16)
    o += SWA_WIDTH
    ck_ref[0] = r1[:, o:o + 128].astype(BF16)
    o += 128
    cv_ref[0] = r1[:, o:o + 128].astype(BF16)

    r2 = _dot_nt(wtr_ref[...], h)
    ones_row = jnp.where(lax.broadcasted_iota(jnp.int32, (VT_ROWS, ATT_BLOCK), 0) == DIFF_V_DIM, 1.0, 0.0)
    vo = DIFF_HEADS * KDIM
    for hh in range(DIFF_HEADS):
        qt_ref[0, hh] = (r2[hh * KDIM:(hh + 1) * KDIM] * Q_SCALE).astype(BF16)
        vv = r2[vo + hh * VT_ROWS: vo + (hh + 1) * VT_ROWS]
        for j in range(nsub):
            vt_ref[0, hh, j] = (vv[:, j * ATT_BLOCK:(j + 1) * ATT_BLOCK] + ones_row).astype(BF16)

    lp = lamp_ref[...]
    s1 = jnp.sum(lp[0:1] * lp[1:2], axis=-1, keepdims=True)
    s2 = jnp.sum(lp[2:3] * lp[3:4], axis=-1, keepdims=True)
    lam = jnp.exp(s1) - jnp.exp(s2) + lam_init
    lam_ref[...] = jnp.broadcast_to(lam, lam_ref.shape)


def _proj_call(x, g, wstd, wtr, kaug, lamp, lam_init):
    b, s, d = x.shape
    tn = TOKEN_TILE
    nsub = tn // ATT_BLOCK
    nkb = s // ATT_BLOCK
    grid = (b, s // tn)
    tok = lambda bi, si: (bi, si, 0)
    out_shape = (
        jax.ShapeDtypeStruct((b, DIFF_HEADS, nkb, ATT_BLOCK, KDIM), BF16),
        jax.ShapeDtypeStruct((b, DIFF_HEADS, KDIM, s), BF16),
        jax.ShapeDtypeStruct((b, DIFF_HEADS, nkb, VT_ROWS, ATT_BLOCK), BF16),
        jax.ShapeDtypeStruct((b, s, GMLP_WIDTH), F32),
        jax.ShapeDtypeStruct((b, s, GMLP_WIDTH), F32),
        jax.ShapeDtypeStruct((b, s, SWA_WIDTH), BF16),
        jax.ShapeDtypeStruct((b, s, 128), BF16),
        jax.ShapeDtypeStruct((b, s, 128), BF16),
        jax.ShapeDtypeStruct((8, LANES), F32),
    )
    out_specs = (
        pl.BlockSpec((1, DIFF_HEADS, nsub, ATT_BLOCK, KDIM), lambda bi, si: (bi, 0, si, 0, 0)),
        pl.BlockSpec((1, DIFF_HEADS, KDIM, tn), lambda bi, si: (bi, 0, 0, si)),
        pl.BlockSpec((1, DIFF_HEADS, nsub, VT_ROWS, ATT_BLOCK), lambda bi, si: (bi, 0, si, 0, 0)),
        pl.BlockSpec((1, tn, GMLP_WIDTH), tok),
        pl.BlockSpec((1, tn, GMLP_WIDTH), tok),
        pl.BlockSpec((1, tn, SWA_WIDTH), tok),
        pl.BlockSpec((1, tn, 128), tok),
        pl.BlockSpec((1, tn, 128), tok),
        pl.BlockSpec((8, LANES), lambda bi, si: (0, 0)),
    )
    in_specs = [
        pl.BlockSpec((1, tn, d), tok),
        _const_spec(g.shape),
        _const_spec(wstd.shape),
        _const_spec(wtr.shape),
        _const_spec(kaug.shape),
        _const_spec(lamp.shape),
    ]
    return pl.pallas_call(
        functools.partial(_proj_kernel, lam_init=lam_init),
        grid=grid, in_specs=in_specs, out_specs=out_specs, out_shape=out_shape,
        compiler_params=_params(("arbitrary", "arbitrary")),
        name="proj",
    )(x, g, wstd, wtr, kaug, lamp)


def _diff_attn_kernel(slope_ref, qt_ref, k_ref, vt_ref, qaug_ref, diag_ref, lam_ref, gsub_ref,
                      o_ref, w_ref, s0_ref, s1_ref, p0_ref, p1_ref, acc_ref, *, out_scale):
    hh = pl.program_id(1)
    qi = pl.program_id(2)
    nkb = k_ref.shape[2]
    blk = ATT_BLOCK
    tq = ATT_QBLOCK
    slope_blk = slope_ref[hh] * float(blk)

    qt = qt_ref[0, 0].astype(F32)
    qaug = qaug_ref[0]
    row = lax.broadcasted_iota(jnp.int32, (KDIM, tq), 0)
    comp_mask = (row < 64, row >= 64)
    zero = jnp.zeros_like(qt)
    wf_left = [jnp.where(mk, qt - qaug, zero) for mk in comp_mask]
    wf_right = [jnp.where(mk, qt + qaug, zero) for mk in comp_mask]
    wf_diag = [jnp.where(mk, qt, zero) for mk in comp_mask]

    for c in range(2):
        w_ref[0, c] = wf_left[c].astype(BF16)
        w_ref[1, c] = wf_right[c].astype(BF16)
    lane = lax.broadcasted_iota(jnp.int32, (KDIM, tq), 1)
    first_grp = lane < blk
    w_mid = ([jnp.where(first_grp, wf_diag[c], wf_left[c]).astype(BF16) for c in range(2)],
             [jnp.where(first_grp, wf_right[c], wf_diag[c]).astype(BF16) for c in range(2)])

    acc_ref[...] = jnp.zeros(acc_ref.shape, F32)
    lane_row = lax.broadcasted_iota(jnp.int32, (1, tq), 1)

    def block_offset(kb):
        c0 = -slope_blk * jnp.abs(2 * qi - kb).astype(F32)
        c1 = -slope_blk * jnp.abs(2 * qi + 1 - kb).astype(F32)
        return jnp.where(lane_row < blk, c0, c1)

    def softmax_step(m_old, s, cm, cvec):
        m_new = jnp.maximum(m_old, cm + cvec)
        return jnp.exp2(s - (m_new - cvec)).astype(BF16), jnp.exp2(m_old - m_new), m_new

    def pv_step(c, kb, p, alpha):
        acc_ref[c] = acc_ref[c] * alpha + _dot(vt_ref[0, 0, kb], p)

    dg = diag_ref[0]
    m = [jnp.full((1, tq), NEG_INF, F32)] * 2
    for g in range(2):
        kb = 2 * qi + g
        k_t = k_ref[0, 0, kb]
        cvec = block_offset(kb)
        for c in range(2):
            s = _dot(k_t, w_mid[g][c])
            s = jnp.concatenate([s[:, :blk] + dg, s[:, blk:]] if g == 0 else [s[:, :blk], s[:, blk:] + dg], axis=1)
            p, alpha, m_new = softmax_step(m[c], s, jnp.max(s, axis=0, keepdims=True), cvec)
            m = [m_new if i == c else m[i] for i in range(2)]
            pv_step(c, kb, p, alpha)

    n_rest = nkb - 2

    def rest_block(j):
        after = j >= 2 * qi
        return j + jnp.where(after, 2, 0), after.astype(jnp.int32)

    def stage_scores(j, s_buf):
        kb, side = rest_block(j)
        k_t = k_ref[0, 0, kb]
        cms = []
        for c in range(2):
            s = _dot(k_t, w_ref[side, c])
            s_buf[c] = s
            cms.append(jnp.max(s, axis=0, keepdims=True))
        return cms

    def stage_softmax(j, m_in, s_buf, cms, p_buf):
        kb, _ = rest_block(j)
        cvec = block_offset(kb)
        alphas, m_out = [], []
        for c in range(2):
            p, alpha, m_new = softmax_step(m_in[c], s_buf[c], cms[c], cvec)
            p_buf[c] = p
            alphas.append(alpha)
            m_out.append(m_new)
        return alphas, m_out

    def stage_pv(j, p_buf, alphas):
        kb, _ = rest_block(j)
        for c in range(2):
            pv_step(c, kb, p_buf[c], alphas[c])

    cm_a = stage_scores(0, s0_ref)
    cm_b = stage_scores(1, s1_ref)
    al_a, m = stage_softmax(0, m, s0_ref, cm_a, p0_ref)

    def steady(u, carry):
        m_c, cm_prev, al_prev = carry
        t = 2 * u
        al_b, m_c = stage_softmax(t - 1, m_c, s1_ref, cm_prev, p1_ref)
        stage_pv(t - 2, p0_ref, al_prev)
        cm_0 = stage_scores(t, s0_ref)
        al_0, m_c = stage_softmax(t, m_c, s0_ref, cm_0, p0_ref)
        stage_pv(t - 1, p1_ref, al_b)
        cm_1 = stage_scores(t + 1, s1_ref)
        return m_c, cm_1, al_0

    m, cm_b, al_a = lax.fori_loop(1, n_rest // 2, steady, (m, cm_b, al_a))

    al_b, m = stage_softmax(n_rest - 1, m, s1_ref, cm_b, p1_ref)
    stage_pv(n_rest - 2, p0_ref, al_a)
    stage_pv(n_rest - 1, p1_ref, al_b)

    a0 = acc_ref[0]
    a1 = acc_ref[1]
    o0 = a0[0:DIFF_V_DIM] / a0[DIFF_V_DIM:DIFF_V_DIM + 1]
    o1 = a1[0:DIFF_V_DIM] / a1[DIFF_V_DIM:DIFF_V_DIM + 1]
    lam = lam_ref[0:1, 0:1]
    o = o0 - lam * o1
    ms = jnp.mean(o * o, axis=0, keepdims=True)
    y = o * lax.rsqrt(ms + EPS) * gsub_ref[...] * out_scale
    o_ref[0] = y.astype(o_ref.dtype)


def _diff_attn_call(slope2, qt, kblk, vtblk, qaug, diag, lam_tile, gsub_b, out_scale):
    b, nh, kdim, s = qt.shape
    nkb = kblk.shape[2]
    blk = ATT_BLOCK
    tq = ATT_QBLOCK
    grid = (b, nh, s // tq)
    in_specs = [
        pl.BlockSpec(memory_space=pltpu.SMEM),
        pl.BlockSpec((1, 1, kdim, tq), lambda bi, hi, qi: (bi, hi, 0, qi)),
        pl.BlockSpec((1, 1, nkb, blk, kdim), lambda bi, hi, qi: (bi, hi, 0, 0, 0)),
        pl.BlockSpec((1, 1, nkb, VT_ROWS, blk), lambda bi, hi, qi: (bi, hi, 0, 0, 0)),
        pl.BlockSpec((1, kdim, tq), lambda bi, hi, qi: (hi, 0, 0)),
        pl.BlockSpec((1, blk, blk), lambda bi, hi, qi: (hi, 0, 0)),
        _const_spec(lam_tile.shape),
        _const_spec(gsub_b.shape),
    ]
    s_buf = pltpu.VMEM((2, blk, tq), F32)
    p_buf = pltpu.VMEM((2, blk, tq), BF16)
    return pl.pallas_call(
        functools.partial(_diff_attn_kernel, out_scale=out_scale),
        grid=grid, in_specs=in_specs,
        out_specs=pl.BlockSpec((1, DIFF_V_DIM, tq), lambda bi, hi, qi: (bi, hi, qi)),
        out_shape=jax.ShapeDtypeStruct((b, nh * DIFF_V_DIM, s), BF16),
        scratch_shapes=[pltpu.VMEM((2, 2, kdim, tq), BF16),
                        s_buf, s_buf, p_buf, p_buf, pltpu.VMEM((2, VT_ROWS, tq), F32)],
        compiler_params=_params(("arbitrary", "arbitrary", "arbitrary")),
        name="diff_attn",
    )(slope2, qt, kblk, vtblk, qaug, diag, lam_tile, gsub_b)


def _swa_kernel(slope_ref, sink_ref, q_ref, kp_ref, kc_ref, kn_ref, vp_ref, vc_ref, vn_ref, o_ref, *, seq_len):
    qi = pl.program_id(1)
    blk = SWA_BLOCK
    q = q_ref[0]
    k3 = jnp.concatenate([kp_ref[0], kc_ref[0], kn_ref[0]], axis=0)
    v3 = jnp.concatenate([vp_ref[0], vc_ref[0], vn_ref[0]], axis=0)
    t = lax.broadcasted_iota(jnp.int32, (blk, 3 * blk), 0)
    cidx = lax.broadcasted_iota(jnp.int32, (blk, 3 * blk), 1)
    dist = jnp.abs(cidx - blk - t)
    key_pos = (qi - 1) * blk + cidx
    valid = (dist <= WINDOW) & (key_pos >= 0) & (key_pos < seq_len)
    distf = dist.astype(F32)
    outs = []
    for kh in range(SWA_KV_HEADS):
        kk = k3[:, kh * HEAD_DIM:(kh + 1) * HEAD_DIM]
        vv = v3[:, kh * HEAD_DIM:(kh + 1) * HEAD_DIM]
        for g in range(SWA_GROUP):
            hq = kh * SWA_GROUP + g
            qh = q[:, hq * HEAD_DIM:(hq + 1) * HEAD_DIM]
            sc = _dot_nt(qh, kk) * (HEAD_DIM ** -0.5)
            sc = sc - slope_ref[hq] * distf
            sc = jnp.where(valid, sc, NEG_INF)
            sink = sink_ref[hq]
            m = jnp.maximum(jnp.max(sc, axis=-1, keepdims=True), sink)
            e = jnp.exp(sc - m)
            pr = e / (jnp.sum(e, axis=-1, keepdims=True) + jnp.exp(sink - m))
            outs.append(_dot(pr.astype(BF16), vv))
    o_ref[0] = jnp.concatenate(outs, axis=-1).astype(o_ref.dtype)


def _swa_call(slopes, sinks, cq, ck, cv):
    b, s, _ = cq.shape
    blk = SWA_BLOCK
    nb = s // blk
    cur = lambda bi, qi: (bi, qi, 0)
    prev = lambda bi, qi: (bi, jnp.maximum(qi - 1, 0), 0)
    nxt = lambda bi, qi: (bi, jnp.minimum(qi + 1, nb - 1), 0)
    kv = lambda im: pl.BlockSpec((1, blk, 128), im)
    in_specs = [
        pl.BlockSpec(memory_space=pltpu.SMEM),
        pl.BlockSpec(memory_space=pltpu.SMEM),
        pl.BlockSpec((1, blk, SWA_WIDTH), cur),
        kv(prev), kv(cur), kv(nxt), kv(prev), kv(cur), kv(nxt),
    ]
    return pl.pallas_call(
        functools.partial(_swa_kernel, seq_len=s),
        grid=(b, nb), in_specs=in_specs,
        out_specs=pl.BlockSpec((1, blk, SWA_WIDTH), cur),
        out_shape=jax.ShapeDtypeStruct((b, s, SWA_WIDTH), BF16),
        compiler_params=_params(("arbitrary", "arbitrary")),
        name="swa",
    )(slopes, sinks, cq, ck, ck, ck, cv, cv, cv)


def _mix_out_kernel(x_ref, yat_ref, bu_ref, bv_ref, yc_ref, lng_ref, lnb_ref, ws_ref, bs_ref,
                    wa_ref, wb_ref, wc_ref, g_ref, o_ref):
    tn = x_ref.shape[1]
    v = bv_ref[0]
    mu = jnp.mean(v, axis=-1, keepdims=True)
    var = jnp.mean(jnp.square(v - mu), axis=-1, keepdims=True)
    vn = ((v - mu) * lax.rsqrt(var + EPS) * lng_ref[...] + lnb_ref[...]).astype(BF16)
    lane_group = lax.broadcasted_iota(jnp.int32, (CHUNK, GMLP_WIDTH), 1) // GMLP_GROUP_DIM
    u = bu_ref[0]
    yb = []
    for c in range(tn // CHUNK):
        vc = vn[c * CHUNK:(c + 1) * CHUNK]
        mixed = bs_ref[...]
        for g in range(GMLP_GROUPS):
            mixed = mixed + jnp.where(lane_group == g, _dot(ws_ref[g], vc), 0.0)
        yb.append(u[c * CHUNK:(c + 1) * CHUNK] * mixed)
    yb = jnp.concatenate(yb, axis=0).astype(BF16)
    y = _dot_tn(yat_ref[0], wa_ref[...]) + _dot(yb, wb_ref[...]) + _dot(yc_ref[0], wc_ref[...])
    o_ref[0] = x_ref[0] + _rms(y, g_ref[...])


def _mix_out_call(x, yat, bu, bv, yc, lng, lnb, ws, bs, wa, wb, wc, g):
    b, s, d = x.shape
    tn = TOKEN_TILE
    tok = lambda bi, si: (bi, si, 0)
    in_specs = [
        pl.BlockSpec((1, tn, d), tok),
        pl.BlockSpec((1, DIFF_WIDTH, tn), lambda bi, si: (bi, 0, si)),
        pl.BlockSpec((1, tn, GMLP_WIDTH), tok),
        pl.BlockSpec((1, tn, GMLP_WIDTH), tok),
        pl.BlockSpec((1, tn, SWA_WIDTH), tok),
    ] + [_const_spec(a.shape) for a in (lng, lnb, ws, bs, wa, wb, wc, g)]
    return pl.pallas_call(
        _mix_out_kernel,
        grid=(b, s // tn), in_specs=in_specs,
        out_specs=pl.BlockSpec((1, tn, d), tok),
        out_shape=jax.ShapeDtypeStruct((b, s, d), F32),
        compiler_params=_params(("arbitrary", "arbitrary")),
        name="mix_out",
    )(x, yat, bu, bv, yc, lng, lnb, ws, bs, wa, wb, wc, g)


def _sigmoid(z):
    return 1.0 / (1.0 + jnp.exp(-z))


def _ffn_ple_kernel(x_ref, p_ref, gpre_ref, wg_ref, wu_ref, wo_ref, gpost_ref,
                    wup_ref, wgate_ref, ggate_ref, gple_ref, o_ref):
    x = x_ref[0]
    h = _rms(x, gpre_ref[...]).astype(BF16)
    f = jnp.zeros(x.shape, F32)
    for j in range(D_FF // FF_CHUNK):
        sl = slice(j * FF_CHUNK, (j + 1) * FF_CHUNK)
        gate = _dot(h, wg_ref[:, sl])
        up = _dot(h, wu_ref[:, sl])
        a = (gate * _sigmoid(gate) * up).astype(BF16)
        f = f + _dot(a, wo_ref[sl, :])
    x = x + _rms(f, gpost_ref[...])
    e = _dot(p_ref[0, 0].astype(BF16), wup_ref[...])
    gt = _sigmoid(_dot(_rms(x, ggate_ref[...]).astype(BF16), wgate_ref[...]))
    o_ref[0] = x + _rms(e * gt, gple_ref[...])


def _ffn_ple_call(x, p, layer, gpre, wg, wu, wo, gpost, wup, wgate, ggate, gple):
    b, s, d = x.shape
    tn = TOKEN_TILE
    tok = lambda bi, si: (bi, si, 0)
    in_specs = [
        pl.BlockSpec((1, tn, d), tok),
        pl.BlockSpec((1, 1, tn, PLE_DIM), lambda bi, si: (layer, bi, si, 0)),
    ] + [_const_spec(a.shape) for a in (gpre, wg, wu, wo, gpost, wup, wgate, ggate, gple)]
    return pl.pallas_call(
        _ffn_ple_kernel,
        grid=(b, s // tn), in_specs=in_specs,
        out_specs=pl.BlockSpec((1, tn, d), tok),
        out_shape=jax.ShapeDtypeStruct((b, s, d), F32),
        compiler_params=_params(("arbitrary", "arbitrary")),
        name="ffn_ple",
    )(x, p, gpre, wg, wu, wo, gpost, wup, wgate, ggate, gple)


def _prep_in_weights(w):
    d = w.shape[0]
    aq = w[:, 0:384].reshape(d, DIFF_HEADS, 2, DIFF_QK_DIM)
    ak = w[:, 384:768].reshape(d, DIFF_HEADS, 2, DIFF_QK_DIM)
    av = w[:, 768:1152].reshape(d, DIFF_HEADS, DIFF_V_DIM)
    pad_qk = ((0, 0), (0, 0), (0, 0), (0, 64 - DIFF_QK_DIM))
    wk = jnp.pad(ak, pad_qk).reshape(d, DIFF_HEADS * KDIM)
    wq = jnp.pad(aq, pad_qk).reshape(d, DIFF_HEADS * KDIM)
    wv = jnp.pad(av, ((0, 0), (0, 0), (0, VT_ROWS - DIFF_V_DIM))).reshape(d, DIFF_HEADS * VT_ROWS)
    wstd = jnp.concatenate([wk, w[:, 1152:]], axis=1).astype(BF16)
    wtr = jnp.concatenate([wq, wv], axis=1).T.astype(BF16)
    return wstd, wtr


def kernel(x, p, g_pre_mix, w_in, lam_q1, lam_k1, lam_q2, lam_k2, g_diff_sub, gmlp_ln_g, gmlp_ln_b,
           w_spatial, b_spatial, swa_sinks, w_out, g_post_mix, g_pre_ffn, w_ffn_in, w_ffn_out,
           g_post_ffn, w_ple_up, w_ple_gate, g_ple_gate, g_ple_post):
    b, s, d = x.shape
    depth = w_in.shape[0]
    assert d == D_MODEL and s % ATT_QBLOCK == 0 and s >= 2 * ATT_QBLOCK and s % TOKEN_TILE == 0

    slope2_np, qaug_np, kaug_np, diag_np = _diff_bias_constants(TOKEN_TILE)
    _, swa_slopes_np = _alibi_slopes_np()
    slope2 = jnp.asarray(slope2_np)
    qaug = jnp.asarray(qaug_np)
    kaug = jnp.asarray(kaug_np)
    diag = jnp.asarray(diag_np)
    swa_slopes = jnp.asarray(swa_slopes_np)
    row = lambda a: a.reshape(1, -1).astype(F32)

    for l in range(depth):
        lam_init = 0.8 - 0.6 * math.exp(-0.3 * l)
        wstd, wtr = _prep_in_weights(w_in[l])
        lamp = jnp.stack([lam_q1[l], lam_k1[l], lam_q2[l], lam_k2[l]]).astype(F32)
        kblk, qt, vtblk, bu, bv, cq, ck, cv, lam_tile = _proj_call(
            x, row(g_pre_mix[l]), wstd, wtr, kaug, lamp, lam_init)

        gsub_b = jnp.broadcast_to(g_diff_sub[l].astype(F32)[:, None], (DIFF_V_DIM, ATT_QBLOCK))
        yat = _diff_attn_call(slope2, qt, kblk, vtblk, qaug, diag, lam_tile, gsub_b, 1.0 - lam_init)
        yc = _swa_call(swa_slopes, swa_sinks[l].astype(F32), cq, ck, cv)

        bs = jnp.broadcast_to(b_spatial[l].T[:, :, None], (CHUNK, GMLP_GROUPS, GMLP_GROUP_DIM))
        bs = bs.reshape(CHUNK, GMLP_WIDTH).astype(F32)
        wo = w_out[l].astype(BF16)
        x = _mix_out_call(
            x, yat, bu, bv, yc, row(gmlp_ln_g[l]), row(gmlp_ln_b[l]), w_spatial[l].astype(BF16), bs,
            wo[0:DIFF_WIDTH], wo[DIFF_WIDTH:DIFF_WIDTH + GMLP_WIDTH], wo[DIFF_WIDTH + GMLP_WIDTH:],
            row(g_post_mix[l]))

        wfi = w_ffn_in[l].astype(BF16)
        x = _ffn_ple_call(
            x, p, l, row(g_pre_ffn[l]), wfi[:, :D_FF], wfi[:, D_FF:], w_ffn_out[l].astype(BF16),
            row(g_post_ffn[l]), w_ple_up[l].astype(BF16), w_ple_gate[l].astype(BF16),
            row(g_ple_gate[l]), row(g_ple_post[l]))
    return x
```

```python
import functools
import math

import numpy as np
import jax
import jax.numpy as jnp
from jax import lax
from jax.experimental import pallas as pl
from jax.experimental.pallas import tpu as pltpu

D_MODEL = 1024
HEAD_DIM = 64
DIFF_HEADS = 6
DIFF_QK_DIM = 32
DIFF_V_DIM = 64
DIFF_WIDTH = DIFF_HEADS * DIFF_V_DIM
GMLP_GROUPS = 4
GMLP_GROUP_DIM = 64
GMLP_WIDTH = GMLP_GROUPS * GMLP_GROUP_DIM
CHUNK = 128
SWA_HEADS = 6
SWA_KV_HEADS = 2
SWA_GROUP = SWA_HEADS // SWA_KV_HEADS
SWA_WIDTH = SWA_HEADS * HEAD_DIM
WINDOW = 128
SWA_BLOCK = 128
D_FF = 2816
PLE_DIM = 256
N_ATTN_HEADS = DIFF_HEADS + SWA_HEADS
ALIBI_MAX_EXP = 8.0
EPS = 1e-6
NEG_INF = -1e30
LOG2E = 1.4426950408889634

LANES = 128
MXU_DIM_V7X = 256
VMEM_LIMIT_BYTES_V7X = 56 * 1024 * 1024

TOKEN_TILE = 512
ATT_BLOCK = MXU_DIM_V7X
ATT_QBLOCK = 2 * ATT_BLOCK
KDIM = 2 * 64
AUG_OFF = DIFF_QK_DIM
ATT_UNROLL_CHOICES = (6, 4, 2)
DYN_ROW0 = 112
DYN_ROWS = KDIM - DYN_ROW0
VT_ROWS = 80
FF_CHUNK = 256

F32 = jnp.float32
BF16 = jnp.bfloat16


def _bf16_round_np(x):
    u = np.asarray(x, np.float32).view(np.uint32).astype(np.uint64)
    r = ((u >> 16) & 1) + 0x7FFF
    return ((u + r) & 0xFFFF0000).astype(np.uint32).view(np.float32)


def _alibi_slopes_np():
    k = np.arange(1, N_ATTN_HEADS + 1, dtype=np.float64)
    s = np.exp2(-ALIBI_MAX_EXP * k / N_ATTN_HEADS).astype(np.float32)
    return s[SWA_HEADS:], s[:SWA_HEADS]


def _split3_bf16(v):
    v = np.asarray(v, np.float32)
    hi = _bf16_round_np(v)
    mid = _bf16_round_np(v - hi)
    lo = _bf16_round_np(v - hi - mid)
    return hi, mid, lo


def _diff_bias_constants(token_tile):
    slopes, _ = _alibi_slopes_np()
    slope2 = (slopes.astype(np.float64) * LOG2E).astype(np.float32)
    hi, mid, lo = _split3_bf16(slope2)
    parts = np.stack([hi, mid, lo], axis=1)
    rel = np.arange(ATT_BLOCK, dtype=np.float32)
    qaug = np.zeros((DIFF_HEADS, KDIM, ATT_BLOCK), np.float32)
    kaug = np.zeros((DIFF_HEADS, ATT_BLOCK, KDIM), np.float32)
    for c in range(2):
        base = c * 64 + AUG_OFF
        for t in range(3):
            qaug[:, base + t, :] = rel[None, :]
            qaug[:, base + 3 + t, :] = -parts[:, t][:, None]
            kaug[:, :, base + t] = parts[:, t][:, None]
            kaug[:, :, base + 3 + t] = rel[None, :]
    kaug[:, :, DYN_ROW0:DYN_ROW0 + 3] = 1.0
    kaug = np.tile(kaug, (1, token_tile // ATT_BLOCK, 1))
    qaug = np.tile(qaug, (1, 1, ATT_QBLOCK // ATT_BLOCK))
    dist = np.abs(rel[:, None] - rel[None, :])
    diag = -(slope2[:, None, None] * dist[None])
    return slope2, qaug, kaug, diag.astype(np.float32)


def _rms(x, g):
    return x * lax.rsqrt(jnp.mean(x * x, axis=-1, keepdims=True) + EPS) * g


def _dot(a, b):
    return jnp.dot(a, b, preferred_element_type=F32)


def _dot_nt(a, b):
    return lax.dot_general(a, b, (((1,), (1,)), ((), ())), preferred_element_type=F32)


def _dot_tn(a, b):
    return lax.dot_general(a, b, (((0,), (0,)), ((), ())), preferred_element_type=F32)


def _const_spec(shape):
    nd = len(shape)
    return pl.BlockSpec(shape, lambda *_: (0,) * nd, pipeline_mode=pl.Buffered(1))


def _params(sem, flags=None):
    return pltpu.CompilerParams(dimension_semantics=sem, vmem_limit_bytes=VMEM_LIMIT_BYTES_V7X, flags=flags)


N_STD = DIFF_HEADS * KDIM + 2 * GMLP_WIDTH + SWA_WIDTH + 2 * SWA_KV_HEADS * HEAD_DIM
N_TR = DIFF_HEADS * KDIM + DIFF_HEADS * VT_ROWS
Q_SCALE = (DIFF_QK_DIM ** -0.5) * LOG2E


def _proj_kernel(x_ref, g_ref, wstd_ref, wtr_ref, kaug_ref, lamp_ref,
                 k_ref, qt_ref, vt_ref, bu_ref, bv_ref, cq_ref, ck_ref, cv_ref, lam_ref, *, lam_init):
    tn = x_ref.shape[1]
    nsub = tn // ATT_BLOCK
    h = _rms(x_ref[0], g_ref[...]).astype(BF16)
    r1 = _dot(h, wstd_ref[...])
    for hh in range(DIFF_HEADS):
        kk = (r1[:, hh * KDIM:(hh + 1) * KDIM] + kaug_ref[hh]).astype(BF16)
        for j in range(nsub):
            k_ref[0, hh, j] = kk[j * ATT_BLOCK:(j + 1) * ATT_BLOCK]
    o = DIFF_HEADS * KDIM
    bu_ref[0] = r1[:, o:o + GMLP_WIDTH]
    o += GMLP_WIDTH
    bv_ref[0] = r1[:, o:o + GMLP_WIDTH]
    o += GMLP_WIDTH
    cq_ref[0] = r1[:, o:o + SWA_WIDTH].astype(BF16)
    o += SWA_WIDTH
    ck_ref[0] = r1[:, o:o + 128].astype(BF16)
    o += 128
    cv_ref[0] = r1[:, o:o + 128].astype(BF16)

    r2 = _dot_nt(wtr_ref[...], h)
    ones_row = jnp.where(lax.broadcasted_iota(jnp.int32, (VT_ROWS, ATT_BLOCK), 0) == DIFF_V_DIM, 1.0, 0.0)
    vo = DIFF_HEADS * KDIM
    for hh in range(DIFF_HEADS):
        qt_ref[0, hh] = (r2[hh * KDIM:(hh + 1) * KDIM] * Q_SCALE).astype(BF16)
        vv = r2[vo + hh * VT_ROWS: vo + (hh + 1) * VT_ROWS]
        for j in range(nsub):
            vt_ref[0, hh, j] = (vv[:, j * ATT_BLOCK:(j + 1) * ATT_BLOCK] + ones_row).astype(BF16)

    lp = lamp_ref[...]
    s1 = jnp.sum(lp[0:1] * lp[1:2], axis=-1, keepdims=True)
    s2 = jnp.sum(lp[2:3] * lp[3:4], axis=-1, keepdims=True)
    lam = jnp.exp(s1) - jnp.exp(s2) + lam_init
    lam_ref[...] = jnp.broadcast_to(lam, lam_ref.shape)


def _proj_call(x, g, wstd, wtr, kaug, lamp, lam_init):
    b, s, d = x.shape
    tn = TOKEN_TILE
    nsub = tn // ATT_BLOCK
    nkb = s // ATT_BLOCK
    grid = (b, s // tn)
    tok = lambda bi, si: (bi, si, 0)
    out_shape = (
        jax.ShapeDtypeStruct((b, DIFF_HEADS, nkb, ATT_BLOCK, KDIM), BF16),
        jax.ShapeDtypeStruct((b, DIFF_HEADS, KDIM, s), BF16),
        jax.ShapeDtypeStruct((b, DIFF_HEADS, nkb, VT_ROWS, ATT_BLOCK), BF16),
        jax.ShapeDtypeStruct((b, s, GMLP_WIDTH), F32),
        jax.ShapeDtypeStruct((b, s, GMLP_WIDTH), F32),
        jax.ShapeDtypeStruct((b, s, SWA_WIDTH), BF16),
        jax.ShapeDtypeStruct((b, s, 128), BF16),
        jax.ShapeDtypeStruct((b, s, 128), BF16),
        jax.ShapeDtypeStruct((8, LANES), F32),
    )
    out_specs = (
        pl.BlockSpec((1, DIFF_HEADS, nsub, ATT_BLOCK, KDIM), lambda bi, si: (bi, 0, si, 0, 0)),
        pl.BlockSpec((1, DIFF_HEADS, KDIM, tn), lambda bi, si: (bi, 0, 0, si)),
        pl.BlockSpec((1, DIFF_HEADS, nsub, VT_ROWS, ATT_BLOCK), lambda bi, si: (bi, 0, si, 0, 0)),
        pl.BlockSpec((1, tn, GMLP_WIDTH), tok),
        pl.BlockSpec((1, tn, GMLP_WIDTH), tok),
        pl.BlockSpec((1, tn, SWA_WIDTH), tok),
        pl.BlockSpec((1, tn, 128), tok),
        pl.BlockSpec((1, tn, 128), tok),
        pl.BlockSpec((8, LANES), lambda bi, si: (0, 0)),
    )
    in_specs = [
        pl.BlockSpec((1, tn, d), tok),
        _const_spec(g.shape),
        _const_spec(wstd.shape),
        _const_spec(wtr.shape),
        _const_spec(kaug.shape),
        _const_spec(lamp.shape),
    ]
    return pl.pallas_call(
        functools.partial(_proj_kernel, lam_init=lam_init),
        grid=grid, in_specs=in_specs, out_specs=out_specs, out_shape=out_shape,
        compiler_params=_params(("arbitrary", "arbitrary")),
        name="proj",
    )(x, g, wstd, wtr, kaug, lamp)


def _diff_attn_kernel(slope_ref, qt_ref, k_ref, vt_ref, qaug_ref, diag_ref, lam_ref, gsub_ref,
                      o_ref, w_ref, s0_ref, s1_ref, p0_ref, p1_ref, acc_ref, *, out_scale):
    hh = pl.program_id(1)
    qi = pl.program_id(2)
    nkb = k_ref.shape[2]
    blk = ATT_BLOCK
    tq = ATT_QBLOCK
    slope_blk = slope_ref[hh] * float(blk)

    qt = qt_ref[0, 0].astype(F32)
    qaug = qaug_ref[0]
    row = lax.broadcasted_iota(jnp.int32, (KDIM, tq), 0)
    comp_mask = (row < 64, row >= 64)
    zero = jnp.zeros_like(qt)
    wf_left = [jnp.where(mk, qt - qaug, zero) for mk in comp_mask]
    wf_right = [jnp.where(mk, qt + qaug, zero) for mk in comp_mask]
    wf_diag = [jnp.where(mk, qt, zero) for mk in comp_mask]

    for c in range(2):
        w_ref[0, c] = wf_left[c].astype(BF16)
        w_ref[1, c] = wf_right[c].astype(BF16)
    lane = lax.broadcasted_iota(jnp.int32, (KDIM, tq), 1)
    first_grp = lane < blk
    w_mid = ([jnp.where(first_grp, wf_diag[c], wf_left[c]).astype(BF16) for c in range(2)],
             [jnp.where(first_grp, wf_right[c], wf_diag[c]).astype(BF16) for c in range(2)])

    acc_ref[...] = jnp.zeros(acc_ref.shape, F32)
    lane_row = lax.broadcasted_iota(jnp.int32, (1, tq), 1)

    def block_offset(kb):
        c0 = -slope_blk * jnp.abs(2 * qi - kb).astype(F32)
        c1 = -slope_blk * jnp.abs(2 * qi + 1 - kb).astype(F32)
        return jnp.where(lane_row < blk, c0, c1)

    def softmax_step(m_old, s, cm, cvec):
        m_new = jnp.maximum(m_old, cm + cvec)
        return jnp.exp2(s - (m_new - cvec)).astype(BF16), jnp.exp2(m_old - m_new), m_new

    def pv_step(c, kb, p, alpha):
        acc_ref[c] = acc_ref[c] * alpha + _dot(vt_ref[0, 0, kb], p)

    dg = diag_ref[0]
    m = [jnp.full((1, tq), NEG_INF, F32)] * 2
    for g in range(2):
        kb = 2 * qi + g
        k_t = k_ref[0, 0, kb]
        cvec = block_offset(kb)
        for c in range(2):
            s = _dot(k_t, w_mid[g][c])
            s = jnp.concatenate([s[:, :blk] + dg, s[:, blk:]] if g == 0 else [s[:, :blk], s[:, blk:] + dg], axis=1)
            p, alpha, m_new = softmax_step(m[c], s, jnp.max(s, axis=0, keepdims=True), cvec)
            m = [m_new if i == c else m[i] for i in range(2)]
            pv_step(c, kb, p, alpha)

    n_rest = nkb - 2
    unroll = next(u for u in ATT_UNROLL_CHOICES if (n_rest - 2) % u == 0)
    dyn_row =lax.broadcasted_iota(jnp.int32, (DYN_ROWS, tq), 0)

    def rest_block(j):
        after = j >= 2 * qi
        return j + jnp.where(after, 2, 0), after.astype(jnp.int32)

    def reference_rows(ref_max, cvec):
        rr = cvec - ref_max
        hi = rr.astype(BF16).astype(F32)
        mid = (rr - hi).astype(BF16).astype(F32)
        lo = rr - hi - mid
        rows = jnp.where(dyn_row == 0, hi, jnp.where(dyn_row == 1, mid, jnp.where(dyn_row == 2, lo, 0.0)))
        return rows.astype(BF16)

    def column_max(z):
        parts = [z[i * 16:(i + 1) * 16] for i in range(z.shape[0] // 16)]
        while len(parts) > 1:
            parts = [jnp.maximum(parts[i], parts[i + 1]) for i in range(0, len(parts), 2)]
        return jnp.max(parts[0].astype(F32), axis=0, keepdims=True)

    def stage_scores(j, ref_max, s_buf):
        kb, side = rest_block(j)
        cvec = block_offset(kb)
        k_t = k_ref[0, 0, kb]
        cms = []
        for c in range(2):
            w = jnp.concatenate([w_ref[side, c, 0:DYN_ROW0, :], reference_rows(ref_max[c], cvec)], axis=0)
            z = _dot(k_t, w).astype(BF16)
            s_buf[c] = z
            cms.append(column_max(z))
        return cms

    def stage_softmax(m2, m1, s_buf, cms, p_buf):
        alphas, m_out = [], []
        for c in range(2):
            d = jnp.maximum(m1[c] - m2[c], cms[c]).astype(BF16)
            m_new = m2[c] + d.astype(F32)
            p_buf[c] = jnp.exp2(s_buf[c] - d)
            alphas.append(jnp.exp2(m1[c] - m_new))
            m_out.append(m_new)
        return alphas, m_out

    def stage_pv(j, p_buf, alphas):
        kb, _ = rest_block(j)
        for c in range(2):
            pv_step(c, kb, p_buf[c], alphas[c])

    s_bufs = (s0_ref, s1_ref)
    p_bufs = (p0_ref, p1_ref)

    cm_a = stage_scores(0, m, s0_ref)
    cm_b = stage_scores(1, m, s1_ref)
    al_a, m_a = stage_softmax(m, m, s0_ref, cm_a, p0_ref)

    def full_step(t, slot, state):
        m_t3, m_t2, cm_prev, al_prev = state
        al_new, m_t1 = stage_softmax(m_t3, m_t2, s_bufs[1 - slot], cm_prev, p_bufs[1 - slot])
        stage_pv(t - 2, p_bufs[slot], al_prev)
        cm_new = stage_scores(t, m_t2, s_bufs[slot])
        return m_t2, m_t1, cm_new, al_new

    def steady(u, state):
        for i in range(unroll):
            state = full_step(2 + unroll * u + i, i % 2, state)
        return state

    state = lax.fori_loop(0, (n_rest - 2) // unroll, steady, (m, m_a, cm_b, al_a))

    m_t3, m_t2, cm_b, al_a = state
    al_b, _ = stage_softmax(m_t3, m_t2, s1_ref, cm_b, p1_ref)
    stage_pv(n_rest - 2, p0_ref, al_a)
    stage_pv(n_rest - 1, p1_ref, al_b)

    a0 = acc_ref[0]
    a1 = acc_ref[1]
    o0 = a0[0:DIFF_V_DIM] / a0[DIFF_V_DIM:DIFF_V_DIM + 1]
    o1 = a1[0:DIFF_V_DIM] / a1[DIFF_V_DIM:DIFF_V_DIM + 1]
    lam = lam_ref[0:1, 0:1]
    o = o0 - lam * o1
    ms = jnp.mean(o * o, axis=0, keepdims=True)
    y = o * lax.rsqrt(ms + EPS) * gsub_ref[...] * out_scale
    o_ref[0] = y.astype(o_ref.dtype)


def _diff_attn_call(slope2, qt, kblk, vtblk, qaug, diag, lam_tile, gsub_b, out_scale):
    b, nh, kdim, s = qt.shape
    nkb = kblk.shape[2]
    blk = ATT_BLOCK
    tq = ATT_QBLOCK
    grid = (b, nh, s // tq)
    in_specs = [
        pl.BlockSpec(memory_space=pltpu.SMEM),
        pl.BlockSpec((1, 1, kdim, tq), lambda bi, hi, qi: (bi, hi, 0, qi)),
        pl.BlockSpec((1, 1, nkb, blk, kdim), lambda bi, hi, qi: (bi, hi, 0, 0, 0)),
        pl.BlockSpec((1, 1, nkb, VT_ROWS, blk), lambda bi, hi, qi: (bi, hi, 0, 0, 0)),
        pl.BlockSpec((1, kdim, tq), lambda bi, hi, qi: (hi, 0, 0)),
        pl.BlockSpec((1, blk, blk), lambda bi, hi, qi: (hi, 0, 0)),
        _const_spec(lam_tile.shape),
        _const_spec(gsub_b.shape),
    ]
    s_buf = pltpu.VMEM((2, blk, tq), BF16)
    p_buf = pltpu.VMEM((2, blk, tq), BF16)
    return pl.pallas_call(
        functools.partial(_diff_attn_kernel, out_scale=out_scale),
        grid=grid, in_specs=in_specs,
        out_specs=pl.BlockSpec((1, DIFF_V_DIM, tq), lambda bi, hi, qi: (bi, hi, qi)),
        out_shape=jax.ShapeDtypeStruct((b, nh * DIFF_V_DIM, s), BF16),
        scratch_shapes=[pltpu.VMEM((2, 2, kdim, tq), BF16),
                        s_buf, s_buf, p_buf, p_buf, pltpu.VMEM((2, VT_ROWS, tq), F32)],
        compiler_params=_params(("arbitrary", "arbitrary", "arbitrary")),
        name="diff_attn",
    )(slope2, qt, kblk, vtblk, qaug, diag, lam_tile, gsub_b)


def _swa_kernel(slope_ref, sink_ref, q_ref, kp_ref, kc_ref, kn_ref, vp_ref, vc_ref, vn_ref, o_ref, *, seq_len):
    qi = pl.program_id(1)
    blk = SWA_BLOCK
    q = q_ref[0]
    k3 = jnp.concatenate([kp_ref[0], kc_ref[0], kn_ref[0]], axis=0)
    v3 = jnp.concatenate([vp_ref[0], vc_ref[0], vn_ref[0]], axis=0)
    t = lax.broadcasted_iota(jnp.int32, (blk, 3 * blk), 0)
    cidx = lax.broadcasted_iota(jnp.int32, (blk, 3 * blk), 1)
    dist = jnp.abs(cidx - blk - t)
    key_pos = (qi - 1) * blk + cidx
    valid = (dist <= WINDOW) & (key_pos >= 0) & (key_pos < seq_len)
    distf = dist.astype(F32)
    outs = []
    for kh in range(SWA_KV_HEADS):
        kk = k3[:, kh * HEAD_DIM:(kh + 1) * HEAD_DIM]
        vv = v3[:, kh * HEAD_DIM:(kh + 1) * HEAD_DIM]
        for g in range(SWA_GROUP):
            hq = kh * SWA_GROUP + g
            qh = q[:, hq * HEAD_DIM:(hq + 1) * HEAD_DIM]
            sc = _dot_nt(qh, kk) * (HEAD_DIM ** -0.5)
            sc = sc - slope_ref[hq] * distf
            sc = jnp.where(valid, sc, NEG_INF)
            sink = sink_ref[hq]
            m = jnp.maximum(jnp.max(sc, axis=-1, keepdims=True), sink)
            e = jnp.exp(sc - m)
            pr = e / (jnp.sum(e, axis=-1, keepdims=True) + jnp.exp(sink - m))
            outs.append(_dot(pr.astype(BF16), vv))
    o_ref[0] = jnp.concatenate(outs, axis=-1).astype(o_ref.dtype)


def _swa_call(slopes, sinks, cq, ck, cv):
    b, s, _ = cq.shape
    blk = SWA_BLOCK
    nb = s // blk
    cur = lambda bi, qi: (bi, qi, 0)
    prev = lambda bi, qi: (bi, jnp.maximum(qi - 1, 0), 0)
    nxt = lambda bi, qi: (bi, jnp.minimum(qi + 1, nb - 1), 0)
    kv = lambda im: pl.BlockSpec((1, blk, 128), im)
    in_specs = [
        pl.BlockSpec(memory_space=pltpu.SMEM),
        pl.BlockSpec(memory_space=pltpu.SMEM),
        pl.BlockSpec((1, blk, SWA_WIDTH), cur),
        kv(prev), kv(cur), kv(nxt), kv(prev), kv(cur), kv(nxt),
    ]
    return pl.pallas_call(
        functools.partial(_swa_kernel, seq_len=s),
        grid=(b, nb), in_specs=in_specs,
        out_specs=pl.BlockSpec((1, blk, SWA_WIDTH), cur),
        out_shape=jax.ShapeDtypeStruct((b, s, SWA_WIDTH), BF16),
        compiler_params=_params(("arbitrary", "arbitrary")),
        name="swa",
    )(slopes, sinks, cq, ck, ck, ck, cv, cv, cv)


def _mix_out_kernel(x_ref, yat_ref, bu_ref, bv_ref, yc_ref, lng_ref, lnb_ref, ws_ref, bs_ref,
                    wa_ref, wb_ref, wc_ref, g_ref, o_ref):
    tn = x_ref.shape[1]
    v = bv_ref[0]
    mu = jnp.mean(v, axis=-1, keepdims=True)
    var = jnp.mean(jnp.square(v - mu), axis=-1, keepdims=True)
    vn = ((v - mu) * lax.rsqrt(var + EPS) * lng_ref[...] + lnb_ref[...]).astype(BF16)
    lane_group = lax.broadcasted_iota(jnp.int32, (CHUNK, GMLP_WIDTH), 1) // GMLP_GROUP_DIM
    u = bu_ref[0]
    yb = []
    for c in range(tn // CHUNK):
        vc = vn[c * CHUNK:(c + 1) * CHUNK]
        mixed = bs_ref[...]
        for g in range(GMLP_GROUPS):
            mixed = mixed + jnp.where(lane_group == g, _dot(ws_ref[g], vc), 0.0)
        yb.append(u[c * CHUNK:(c + 1) * CHUNK] * mixed)
    yb = jnp.concatenate(yb, axis=0).astype(BF16)
    y = _dot_tn(yat_ref[0], wa_ref[...]) + _dot(yb, wb_ref[...]) + _dot(yc_ref[0], wc_ref[...])
    o_ref[0] = x_ref[0] + _rms(y, g_ref[...])


def _mix_out_call(x, yat, bu, bv, yc, lng, lnb, ws, bs, wa, wb, wc, g):
    b, s, d = x.shape
    tn = TOKEN_TILE
    tok = lambda bi, si: (bi, si, 0)
    in_specs = [
        pl.BlockSpec((1, tn, d), tok),
        pl.BlockSpec((1, DIFF_WIDTH, tn), lambda bi, si: (bi, 0, si)),
        pl.BlockSpec((1, tn, GMLP_WIDTH), tok),
        pl.BlockSpec((1, tn, GMLP_WIDTH), tok),
        pl.BlockSpec((1, tn, SWA_WIDTH), tok),
    ] + [_const_spec(a.shape) for a in (lng, lnb, ws, bs, wa, wb, wc, g)]
    return pl.pallas_call(
        _mix_out_kernel,
        grid=(b, s // tn), in_specs=in_specs,
        out_specs=pl.BlockSpec((1, tn, d), tok),
        out_shape=jax.ShapeDtypeStruct((b, s, d), F32),
        compiler_params=_params(("arbitrary", "arbitrary")),
        name="mix_out",
    )(x, yat, bu, bv, yc, lng, lnb, ws, bs, wa, wb, wc, g)


def _sigmoid(z):
    return 1.0 / (1.0 + jnp.exp(-z))


def _ffn_ple_kernel(x_ref, p_ref, gpre_ref, wg_ref, wu_ref, wo_ref, gpost_ref,
                    wup_ref, wgate_ref, ggate_ref, gple_ref, o_ref):
    x = x_ref[0]
    h = _rms(x, gpre_ref[...]).astype(BF16)
    f = jnp.zeros(x.shape, F32)
    for j in range(D_FF // FF_CHUNK):
        sl = slice(j * FF_CHUNK, (j + 1) * FF_CHUNK)
        gate = _dot(h, wg_ref[:, sl])
        up = _dot(h, wu_ref[:, sl])
        a = (gate * _sigmoid(gate) * up).astype(BF16)
        f = f + _dot(a, wo_ref[sl, :])
    x = x + _rms(f, gpost_ref[...])
    e = _dot(p_ref[0, 0].astype(BF16), wup_ref[...])
    gt = _sigmoid(_dot(_rms(x, ggate_ref[...]).astype(BF16), wgate_ref[...]))
    o_ref[0] = x + _rms(e * gt, gple_ref[...])


def _ffn_ple_call(x, p, layer, gpre, wg, wu, wo, gpost, wup, wgate, ggate, gple):
    b, s, d = x.shape
    tn = TOKEN_TILE
    tok = lambda bi, si: (bi, si, 0)
    in_specs = [
        pl.BlockSpec((1, tn, d), tok),
        pl.BlockSpec((1, 1, tn, PLE_DIM), lambda bi, si: (layer, bi, si, 0)),
    ] + [_const_spec(a.shape) for a in (gpre, wg, wu, wo, gpost, wup, wgate, ggate, gple)]
    return pl.pallas_call(
        _ffn_ple_kernel,
        grid=(b, s // tn), in_specs=in_specs,
        out_specs=pl.BlockSpec((1, tn, d), tok),
        out_shape=jax.ShapeDtypeStruct((b, s, d), F32),
        compiler_params=_params(("arbitrary", "arbitrary")),
        name="ffn_ple",
    )(x, p, gpre, wg, wu, wo, gpost, wup, wgate, ggate, gple)


def _prep_in_weights(w):
    d = w.shape[0]
    aq = w[:, 0:384].reshape(d, DIFF_HEADS, 2, DIFF_QK_DIM)
    ak = w[:, 384:768].reshape(d, DIFF_HEADS, 2, DIFF_QK_DIM)
    av = w[:, 768:1152].reshape(d, DIFF_HEADS, DIFF_V_DIM)
    pad_qk = ((0, 0), (0, 0), (0, 0), (0, 64 - DIFF_QK_DIM))
    wk = jnp.pad(ak, pad_qk).reshape(d, DIFF_HEADS * KDIM)
    wq = jnp.pad(aq, pad_qk).reshape(d, DIFF_HEADS * KDIM)
    wv = jnp.pad(av, ((0, 0), (0, 0), (0, VT_ROWS - DIFF_V_DIM))).reshape(d, DIFF_HEADS * VT_ROWS)
    wstd = jnp.concatenate([wk, w[:, 1152:]], axis=1).astype(BF16)
    wtr = jnp.concatenate([wq, wv], axis=1).T.astype(BF16)
    return wstd, wtr


def kernel(x, p, g_pre_mix, w_in, lam_q1, lam_k1, lam_q2, lam_k2, g_diff_sub, gmlp_ln_g, gmlp_ln_b,
           w_spatial, b_spatial, swa_sinks, w_out, g_post_mix, g_pre_ffn, w_ffn_in, w_ffn_out,
           g_post_ffn, w_ple_up, w_ple_gate, g_ple_gate, g_ple_post):
    b, s, d = x.shape
    depth = w_in.shape[0]
    assert d == D_MODEL and s % ATT_QBLOCK == 0 and s >= 2 * ATT_QBLOCK and s % TOKEN_TILE == 0

    slope2_np, qaug_np, kaug_np, diag_np = _diff_bias_constants(TOKEN_TILE)
    _, swa_slopes_np = _alibi_slopes_np()
    slope2 = jnp.asarray(slope2_np)
    qaug = jnp.asarray(qaug_np)
    kaug = jnp.asarray(kaug_np)
    diag = jnp.asarray(diag_np)
    swa_slopes = jnp.asarray(swa_slopes_np)
    row = lambda a: a.reshape(1, -1).astype(F32)

    for l in range(depth):
        lam_init = 0.8 - 0.6 * math.exp(-0.3 * l)
        wstd, wtr = _prep_in_weights(w_in[l])
        lamp = jnp.stack([lam_q1[l], lam_k1[l], lam_q2[l], lam_k2[l]]).astype(F32)
        kblk, qt, vtblk, bu, bv, cq, ck, cv, lam_tile = _proj_call(
            x, row(g_pre_mix[l]), wstd, wtr, kaug, lamp, lam_init)

        gsub_b = jnp.broadcast_to(g_diff_sub[l].astype(F32)[:, None], (DIFF_V_DIM, ATT_QBLOCK))
        yat = _diff_attn_call(slope2, qt, kblk, vtblk, qaug, diag, lam_tile, gsub_b, 1.0 - lam_init)
        yc = _swa_call(swa_slopes, swa_sinks[l].astype(F32), cq, ck, cv)

        bs = jnp.broadcast_to(b_spatial[l].T[:, :, None], (CHUNK, GMLP_GROUPS, GMLP_GROUP_DIM))
        bs = bs.reshape(CHUNK, GMLP_WIDTH).astype(F32)
        wo = w_out[l].astype(BF16)
        x = _mix_out_call(
            x, yat, bu, bv, yc, row(gmlp_ln_g[l]), row(gmlp_ln_b[l]), w_spatial[l].astype(BF16), bs,
            wo[0:DIFF_WIDTH], wo[DIFF_WIDTH:DIFF_WIDTH + GMLP_WIDTH], wo[DIFF_WIDTH + GMLP_WIDTH:],
            row(g_post_mix[l]))

        wfi = w_ffn_in[l].astype(BF16)
        x = _ffn_ple_call(
            x, p, l, row(g_pre_ffn[l]), wfi[:, :D_FF], wfi[:, D_FF:], w_ffn_out[l].astype(BF16),
            row(g_post_ffn[l]), w_ple_up[l].astype(BF16), w_ple_gate[l].astype(BF16),
            row(g_ple_gate[l]), row(g_ple_post[l]))
    return x
```

```python
import functools
import math

import numpy as np
import jax
import jax.numpy as jnp
from jax import lax
from jax.experimental import pallas as pl
from jax.experimental.pallas import tpu as pltpu

D_MODEL = 1024
HEAD_DIM = 64
DIFF_HEADS = 6
DIFF_QK_DIM = 32
DIFF_V_DIM = 64
DIFF_WIDTH = DIFF_HEADS * DIFF_V_DIM
GMLP_GROUPS = 4
GMLP_GROUP_DIM = 64
GMLP_WIDTH = GMLP_GROUPS * GMLP_GROUP_DIM
CHUNK = 128
SWA_HEADS = 6
SWA_KV_HEADS = 2
SWA_GROUP = SWA_HEADS // SWA_KV_HEADS
SWA_WIDTH = SWA_HEADS * HEAD_DIM
WINDOW = 128
SWA_BLOCK = 128
D_FF = 2816
PLE_DIM = 256
N_ATTN_HEADS = DIFF_HEADS + SWA_HEADS
ALIBI_MAX_EXP = 8.0
EPS = 1e-6
NEG_INF = -1e30
LOG2E = 1.4426950408889634

LANES = 128
MXU_DIM_V7X = 256
VMEM_LIMIT_BYTES_V7X = 56 * 1024 * 1024

TOKEN_TILE = 512
ATT_BLOCK = MXU_DIM_V7X
ATT_QBLOCK = 2 * ATT_BLOCK
KDIM = 2 * 64
AUG_OFF = DIFF_QK_DIM
ATT_UNROLL_CHOICES = (4, 2)
DYN_ROW0 = 112
DYN_ROWS = KDIM - DYN_ROW0
VT_ROWS = 80
FF_CHUNK = 256

F32 = jnp.float32
BF16 = jnp.bfloat16


def _bf16_round_np(x):
    u = np.asarray(x, np.float32).view(np.uint32).astype(np.uint64)
    r = ((u >> 16) & 1) + 0x7FFF
    return ((u + r) & 0xFFFF0000).astype(np.uint32).view(np.float32)


def _alibi_slopes_np():
    k = np.arange(1, N_ATTN_HEADS + 1, dtype=np.float64)
    s = np.exp2(-ALIBI_MAX_EXP * k / N_ATTN_HEADS).astype(np.float32)
    return s[SWA_HEADS:], s[:SWA_HEADS]


def _split3_bf16(v):
    v = np.asarray(v, np.float32)
    hi = _bf16_round_np(v)
    mid = _bf16_round_np(v - hi)
    lo = _bf16_round_np(v - hi - mid)
    return hi, mid, lo


def _diff_bias_constants(token_tile, seq_len):
    slopes, _ = _alibi_slopes_np()
    slope2 = (slopes.astype(np.float64) * LOG2E).astype(np.float32)
    hi, mid, lo = _split3_bf16(slope2)
    parts = np.stack([hi, mid, lo], axis=1)
    rel = np.arange(ATT_BLOCK, dtype=np.float32)
    qaug = np.zeros((DIFF_HEADS, KDIM, ATT_BLOCK), np.float32)
    kaug = np.zeros((DIFF_HEADS, ATT_BLOCK, KDIM), np.float32)
    for c in range(2):
        base = c * 64 + AUG_OFF
        for t in range(3):
            qaug[:, base + t, :] = rel[None, :]
            qaug[:, base + 3 + t, :] = -parts[:, t][:, None]
            kaug[:, :, base + t] = parts[:, t][:, None]
            kaug[:, :, base + 3 + t] = rel[None, :]
    kaug[:, :, DYN_ROW0:DYN_ROW0 + 3] = 1.0
    kaug = np.tile(kaug, (1, token_tile // ATT_BLOCK, 1))
    qaug = np.tile(qaug, (1, 1, ATT_QBLOCK // ATT_BLOCK))
    n_kb = seq_len // ATT_BLOCK
    sigma_j = (slope2[:, None] * np.float32(ATT_BLOCK)) * np.arange(n_kb, dtype=np.float32)[None, :]
    ktab = np.zeros((DIFF_HEADS, n_kb, 1, KDIM), np.float32)
    for t, piece in enumerate(_split3_bf16(sigma_j)):
        ktab[:, :, 0, DYN_ROW0 + 3 + t] = piece
    pos = np.arange(ATT_QBLOCK, dtype=np.float32)
    dist = np.abs(pos[:, None] - pos[None, :])
    diag = -(slope2[:, None, None] * dist[None])
    return slope2, qaug, kaug, ktab, diag.astype(np.float32)


def _rms(x, g):
    return x * lax.rsqrt(jnp.mean(x * x, axis=-1, keepdims=True) + EPS) * g


def _dot(a, b):
    return jnp.dot(a, b, preferred_element_type=F32)


def _dot_nt(a, b):
    return lax.dot_general(a, b, (((1,), (1,)), ((), ())), preferred_element_type=F32)


def _dot_tn(a, b):
    return lax.dot_general(a, b, (((0,), (0,)), ((), ())), preferred_element_type=F32)


def _const_spec(shape):
    nd = len(shape)
    return pl.BlockSpec(shape, lambda *_: (0,) * nd, pipeline_mode=pl.Buffered(1))


def _params(sem, flags=None):
    return pltpu.CompilerParams(dimension_semantics=sem, vmem_limit_bytes=VMEM_LIMIT_BYTES_V7X, flags=flags)


N_STD = DIFF_HEADS * KDIM + 2 * GMLP_WIDTH + SWA_WIDTH + 2 * SWA_KV_HEADS * HEAD_DIM
N_TR = DIFF_HEADS * KDIM + DIFF_HEADS * VT_ROWS
Q_SCALE = (DIFF_QK_DIM ** -0.5) * LOG2E


def _proj_kernel(x_ref, g_ref, wstd_ref, wtr_ref, kaug_ref, ktab_ref, lamp_ref,
                 k_ref, qt_ref, vt_ref, bu_ref, bv_ref, cq_ref, ck_ref, cv_ref, lam_ref, *, lam_init):
    tn = x_ref.shape[1]
    nsub = tn // ATT_BLOCK
    h = _rms(x_ref[0], g_ref[...]).astype(BF16)
    r1 = _dot(h, wstd_ref[...])
    for hh in range(DIFF_HEADS):
        kk = r1[:, hh * KDIM:(hh + 1) * KDIM] + kaug_ref[hh]
        for j in range(nsub):
            k_ref[0, hh, j] = (kk[j * ATT_BLOCK:(j + 1) * ATT_BLOCK] + ktab_ref[hh, j]).astype(BF16)
    o = DIFF_HEADS * KDIM
    bu_ref[0] = r1[:, o:o + GMLP_WIDTH]
    o += GMLP_WIDTH
    bv_ref[0] = r1[:, o:o + GMLP_WIDTH]
    o += GMLP_WIDTH
    cq_ref[0] = r1[:, o:o + SWA_WIDTH].astype(BF16)
    o += SWA_WIDTH
    ck_ref[0] = r1[:, o:o + 128].astype(BF16)
    o += 128
    cv_ref[0] = r1[:, o:o + 128].astype(BF16)

    r2 = _dot_nt(wtr_ref[...], h)
    ones_row = jnp.where(lax.broadcasted_iota(jnp.int32, (VT_ROWS, ATT_BLOCK), 0) == DIFF_V_DIM, 1.0, 0.0)
    vo = DIFF_HEADS * KDIM
    for hh in range(DIFF_HEADS):
        qt_ref[0, hh] = (r2[hh * KDIM:(hh + 1) * KDIM] * Q_SCALE).astype(BF16)
        vv = r2[vo + hh * VT_ROWS: vo + (hh + 1) * VT_ROWS]
        for j in range(nsub):
            vt_ref[0, hh, j] = (vv[:, j * ATT_BLOCK:(j + 1) * ATT_BLOCK] + ones_row).astype(BF16)

    lp = lamp_ref[...]
    s1 = jnp.sum(lp[0:1] * lp[1:2], axis=-1, keepdims=True)
    s2 = jnp.sum(lp[2:3] * lp[3:4], axis=-1, keepdims=True)
    lam = jnp.exp(s1) - jnp.exp(s2) + lam_init
    lam_ref[...] = jnp.broadcast_to(lam, lam_ref.shape)


def _proj_call(x, g, wstd, wtr, kaug, ktab, lamp, lam_init):
    b, s, d = x.shape
    tn = TOKEN_TILE
    nsub = tn // ATT_BLOCK
    nkb = s // ATT_BLOCK
    grid = (b, s // tn)
    tok = lambda bi, si: (bi, si, 0)
    out_shape = (
        jax.ShapeDtypeStruct((b, DIFF_HEADS, nkb, ATT_BLOCK, KDIM), BF16),
        jax.ShapeDtypeStruct((b, DIFF_HEADS, KDIM, s), BF16),
        jax.ShapeDtypeStruct((b, DIFF_HEADS, nkb, VT_ROWS, ATT_BLOCK), BF16),
        jax.ShapeDtypeStruct((b, s, GMLP_WIDTH), F32),
        jax.ShapeDtypeStruct((b, s, GMLP_WIDTH), F32),
        jax.ShapeDtypeStruct((b, s, SWA_WIDTH), BF16),
        jax.ShapeDtypeStruct((b, s, 128), BF16),
        jax.ShapeDtypeStruct((b, s, 128), BF16),
        jax.ShapeDtypeStruct((8, LANES), F32),
    )
    out_specs = (
        pl.BlockSpec((1, DIFF_HEADS, nsub, ATT_BLOCK, KDIM), lambda bi, si: (bi, 0, si, 0, 0)),
        pl.BlockSpec((1, DIFF_HEADS, KDIM, tn), lambda bi, si: (bi, 0, 0, si)),
        pl.BlockSpec((1, DIFF_HEADS, nsub, VT_ROWS, ATT_BLOCK), lambda bi, si: (bi, 0, si, 0, 0)),
        pl.BlockSpec((1, tn, GMLP_WIDTH), tok),
        pl.BlockSpec((1, tn, GMLP_WIDTH), tok),
        pl.BlockSpec((1, tn, SWA_WIDTH), tok),
        pl.BlockSpec((1, tn, 128), tok),
        pl.BlockSpec((1, tn, 128), tok),
        pl.BlockSpec((8, LANES), lambda bi, si: (0, 0)),
    )
    in_specs = [
        pl.BlockSpec((1, tn, d), tok),
        _const_spec(g.shape),
        _const_spec(wstd.shape),
        _const_spec(wtr.shape),
        _const_spec(kaug.shape),
        pl.BlockSpec((DIFF_HEADS, nsub, 1, KDIM), lambda bi, si: (0, si, 0, 0)),
        _const_spec(lamp.shape),
    ]
    return pl.pallas_call(
        functools.partial(_proj_kernel, lam_init=lam_init),
        grid=grid, in_specs=in_specs, out_specs=out_specs, out_shape=out_shape,
        compiler_params=_params(("arbitrary", "arbitrary")),
        name="proj",
    )(x, g, wstd, wtr, kaug, ktab, lamp)


def _diff_attn_kernel(slope_ref, qt_ref, k_ref, vt_ref, qaug_ref, diag_ref, lam_ref, gsub_ref,
                      o_ref, w_ref, s0_ref, s1_ref, p0_ref, p1_ref, acc_ref, *, out_scale):
    hh = pl.program_id(1)
    qi = pl.program_id(2)
    blk = ATT_BLOCK
    tq = ATT_QBLOCK
    n_rest = k_ref.shape[2] // 2 - 1
    sigma = slope_ref[hh] * float(blk)

    qt = qt_ref[0, 0].astype(F32)
    qaug = qaug_ref[0]
    row = lax.broadcasted_iota(jnp.int32, (KDIM, tq), 0)
    comp_mask = (row < 64, row >= 64)
    zero = jnp.zeros_like(qt)
    for c in range(2):
        w_ref[0, c] = jnp.where(comp_mask[c], qt - qaug, zero).astype(BF16)
        w_ref[1, c] = jnp.where(comp_mask[c], qt + qaug, zero).astype(BF16)

    def key_rows(kb):
        return jnp.concatenate([k_ref[0, 0, 2 * kb], k_ref[0, 0, 2 * kb + 1]], axis=0)

    def value_cols(kb):
        return jnp.concatenate([vt_ref[0, 0, 2 * kb], vt_ref[0, 0, 2 * kb + 1]], axis=1)

    k_t = key_rows(qi)
    vt = value_cols(qi)
    m = []
    for c in range(2):
        s = _dot(k_t, jnp.where(comp_mask[c], qt, zero).astype(BF16)) + diag_ref[0]
        m_c = jnp.max(s, axis=0, keepdims=True)
        acc_ref[c] = _dot(vt, jnp.exp2(s - m_c).astype(BF16))
        m.append(m_c)

    dyn_row = lax.broadcasted_iota(jnp.int32, (DYN_ROWS, tq), 0)
    lane_row = lax.broadcasted_iota(jnp.int32, (1, tq), 1)
    q_origin = sigma * jnp.where(lane_row < blk, 2 * qi, 2 * qi + 1).astype(F32)

    def rest_block(j):
        after = (j >= qi).astype(jnp.int32)
        return j + after, after

    def reference_rows(rr, key_sign):
        hi = rr.astype(BF16).astype(F32)
        mid = (rr - hi).astype(BF16).astype(F32)
        lo = rr - hi - mid
        rows = jnp.where(dyn_row == 0, hi, jnp.where(dyn_row == 1, mid, jnp.where(dyn_row == 2, lo, 0.0)))
        rows = jnp.where((dyn_row >= 3) & (dyn_row < 6), key_sign, rows)
        return rows.astype(BF16)

    def column_max(z):
        parts = [z[i * 16:(i + 1) * 16] for i in range(z.shape[0] // 16)]
        while len(parts) > 1:
            parts = [jnp.maximum(parts[i], parts[i + 1]) for i in range(0, len(parts), 2)]
        return jnp.max(parts[0].astype(F32), axis=0, keepdims=True)

    def stage_scores(j, ref_max, s_buf):
        kb, side = rest_block(j)
        sgn = jnp.where(side == 1, 1.0, -1.0)
        k_t = key_rows(kb)
        cms = []
        for c in range(2):
            dyn = reference_rows(sgn * q_origin - ref_max[c], -sgn)
            w = jnp.concatenate([w_ref[side, c, 0:DYN_ROW0, :], dyn], axis=0)
            z = _dot(k_t, w).astype(BF16)
            s_buf[c] = z
            cms.append(column_max(z))
        return cms

    def stage_softmax(m2, m1, s_buf, cms, p_buf):
        alphas, m_out = [], []
        for c in range(2):
            d = jnp.maximum(m1[c] - m2[c], cms[c]).astype(BF16)
            m_new = m2[c] + d.astype(F32)
            p_buf[c] = jnp.exp2(s_buf[c] - d)
            alphas.append(jnp.exp2(m1[c] - m_new))
            m_out.append(m_new)
        return alphas, m_out

    def stage_pv(j, p_buf, alphas):
        kb, _ = rest_block(j)
        vt = value_cols(kb)
        for c in range(2):
            acc_ref[c] = acc_ref[c] * alphas[c] + _dot(vt, p_buf[c])

    s_bufs = (s0_ref, s1_ref)
    p_bufs = (p0_ref, p1_ref)

    def full_step(t, slot, state):
        m_t3, m_t2, cm_prev, al_prev = state
        al_new, m_t1 = stage_softmax(m_t3, m_t2, s_bufs[1 - slot], cm_prev, p_bufs[1 - slot])
        stage_pv(t - 2, p_bufs[slot], al_prev)
        cm_new = stage_scores(t, m_t2, s_bufs[slot])
        return m_t2, m_t1, cm_new, al_new

    cm_a = stage_scores(0, m, s0_ref)
    cm_b = stage_scores(1, m, s1_ref)
    al_a, m_a = stage_softmax(m, m, s0_ref, cm_a, p0_ref)
    state = full_step(2, 0, (m, m_a, cm_b, al_a))

    n_loop = n_rest - 3
    unroll = next(u for u in ATT_UNROLL_CHOICES if n_loop % u == 0)

    def steady(u, state):
        for i in range(unroll):
            state = full_step(3 + unroll * u + i, (1 + i) % 2, state)
        return state

    state = lax.fori_loop(0, n_loop // unroll, steady, state)

    m_t3, m_t2, cm_last, al_prev = state
    al_last, _ = stage_softmax(m_t3, m_t2, s0_ref, cm_last, p0_ref)
    stage_pv(n_rest - 2, p1_ref, al_prev)
    stage_pv(n_rest - 1, p0_ref, al_last)

    a0 = acc_ref[0]
    a1 = acc_ref[1]
    o0 = a0[0:DIFF_V_DIM] / a0[DIFF_V_DIM:DIFF_V_DIM + 1]
    o1 = a1[0:DIFF_V_DIM] / a1[DIFF_V_DIM:DIFF_V_DIM + 1]
    lam = lam_ref[0:1, 0:1]
    o = o0 - lam * o1
    ms = jnp.mean(o * o, axis=0, keepdims=True)
    y = o * lax.rsqrt(ms + EPS) * gsub_ref[...] * out_scale
    o_ref[0] = y.astype(o_ref.dtype)


def _diff_attn_call(slope2, qt, kblk, vtblk, qaug, diag, lam_tile, gsub_b, out_scale):
    b, nh, kdim, s = qt.shape
    nkb = kblk.shape[2]
    blk = ATT_BLOCK
    tq = ATT_QBLOCK
    grid = (b, nh, s // tq)
    in_specs = [
        pl.BlockSpec(memory_space=pltpu.SMEM),
        pl.BlockSpec((1, 1, kdim, tq), lambda bi, hi, qi: (bi, hi, 0, qi)),
        pl.BlockSpec((1, 1, nkb, blk, kdim), lambda bi, hi, qi: (bi, hi, 0, 0, 0)),
        pl.BlockSpec((1, 1, nkb, VT_ROWS, blk), lambda bi, hi, qi: (bi, hi, 0, 0, 0)),
        pl.BlockSpec((1, kdim, tq), lambda bi, hi, qi: (hi, 0, 0)),
        pl.BlockSpec((1, tq, tq), lambda bi, hi, qi: (hi, 0, 0)),
        _const_spec(lam_tile.shape),
        _const_spec(gsub_b.shape),
    ]
    s_buf = pltpu.VMEM((2, tq, tq), BF16)
    p_buf = pltpu.VMEM((2, tq, tq), BF16)
    return pl.pallas_call(
        functools.partial(_diff_attn_kernel, out_scale=out_scale),
        grid=grid, in_specs=in_specs,
        out_specs=pl.BlockSpec((1, DIFF_V_DIM, tq), lambda bi, hi, qi: (bi, hi, qi)),
        out_shape=jax.ShapeDtypeStruct((b, nh * DIFF_V_DIM, s), BF16),
        scratch_shapes=[pltpu.VMEM((2, 2, kdim, tq), BF16),
                        s_buf, s_buf, p_buf, p_buf, pltpu.VMEM((2, VT_ROWS, tq), F32)],
        compiler_params=_params(("arbitrary", "arbitrary", "arbitrary")),
        name="diff_attn",
    )(slope2, qt, kblk, vtblk, qaug, diag, lam_tile, gsub_b)


def _swa_kernel(slope_ref, sink_ref, q_ref, kp_ref, kc_ref, kn_ref, vp_ref, vc_ref, vn_ref, o_ref, *, seq_len):
    qi = pl.program_id(1)
    blk = SWA_BLOCK
    q = q_ref[0]
    k3 = jnp.concatenate([kp_ref[0], kc_ref[0], kn_ref[0]], axis=0)
    v3 = jnp.concatenate([vp_ref[0], vc_ref[0], vn_ref[0]], axis=0)
    t = lax.broadcasted_iota(jnp.int32, (blk, 3 * blk), 0)
    cidx = lax.broadcasted_iota(jnp.int32, (blk, 3 * blk), 1)
    dist = jnp.abs(cidx - blk - t)
    key_pos = (qi - 1) * blk + cidx
    valid = (dist <= WINDOW) & (key_pos >= 0) & (key_pos < seq_len)
    distf = dist.astype(F32)
    outs = []
    for kh in range(SWA_KV_HEADS):
        kk = k3[:, kh * HEAD_DIM:(kh + 1) * HEAD_DIM]
        vv = v3[:, kh * HEAD_DIM:(kh + 1) * HEAD_DIM]
        for g in range(SWA_GROUP):
            hq = kh * SWA_GROUP + g
            qh = q[:, hq * HEAD_DIM:(hq + 1) * HEAD_DIM]
            sc = _dot_nt(qh, kk) * (HEAD_DIM ** -0.5)
            sc = sc - slope_ref[hq] * distf
            sc = jnp.where(valid, sc, NEG_INF)
            sink = sink_ref[hq]
            m = jnp.maximum(jnp.max(sc, axis=-1, keepdims=True), sink)
            e = jnp.exp(sc - m)
            pr = e / (jnp.sum(e, axis=-1, keepdims=True) + jnp.exp(sink - m))
            outs.append(_dot(pr.astype(BF16), vv))
    o_ref[0] = jnp.concatenate(outs, axis=-1).astype(o_ref.dtype)


def _swa_call(slopes, sinks, cq, ck, cv):
    b, s, _ = cq.shape
    blk = SWA_BLOCK
    nb = s // blk
    cur = lambda bi, qi: (bi, qi, 0)
    prev = lambda bi, qi: (bi, jnp.maximum(qi - 1, 0), 0)
    nxt = lambda bi, qi: (bi, jnp.minimum(qi + 1, nb - 1), 0)
    kv = lambda im: pl.BlockSpec((1, blk, 128), im)
    in_specs = [
        pl.BlockSpec(memory_space=pltpu.SMEM),
        pl.BlockSpec(memory_space=pltpu.SMEM),
        pl.BlockSpec((1, blk, SWA_WIDTH), cur),
        kv(prev), kv(cur), kv(nxt), kv(prev), kv(cur), kv(nxt),
    ]
    return pl.pallas_call(
        functools.partial(_swa_kernel, seq_len=s),
        grid=(b, nb), in_specs=in_specs,
        out_specs=pl.BlockSpec((1, blk, SWA_WIDTH), cur),
        out_shape=jax.ShapeDtypeStruct((b, s, SWA_WIDTH), BF16),
        compiler_params=_params(("arbitrary", "arbitrary")),
        name="swa",
    )(slopes, sinks, cq, ck, ck, ck, cv, cv, cv)


def _mix_out_kernel(x_ref, yat_ref, bu_ref, bv_ref, yc_ref, lng_ref, lnb_ref, ws_ref, bs_ref,
                    wa_ref, wb_ref, wc_ref, g_ref, o_ref):
    tn = x_ref.shape[1]
    v = bv_ref[0]
    mu = jnp.mean(v, axis=-1, keepdims=True)
    var = jnp.mean(jnp.square(v - mu), axis=-1, keepdims=True)
    vn = ((v - mu) * lax.rsqrt(var + EPS) * lng_ref[...] + lnb_ref[...]).astype(BF16)
    lane_group = lax.broadcasted_iota(jnp.int32, (CHUNK, GMLP_WIDTH), 1) // GMLP_GROUP_DIM
    u = bu_ref[0]
    yb = []
    for c in range(tn // CHUNK):
        vc = vn[c * CHUNK:(c + 1) * CHUNK]
        mixed = bs_ref[...]
        for g in range(GMLP_GROUPS):
            mixed = mixed + jnp.where(lane_group == g, _dot(ws_ref[g], vc), 0.0)
        yb.append(u[c * CHUNK:(c + 1) * CHUNK] * mixed)
    yb = jnp.concatenate(yb, axis=0).astype(BF16)
    y = _dot_tn(yat_ref[0], wa_ref[...]) + _dot(yb, wb_ref[...]) + _dot(yc_ref[0], wc_ref[...])
    o_ref[0] = x_ref[0] + _rms(y, g_ref[...])


def _mix_out_call(x, yat, bu, bv, yc, lng, lnb, ws, bs, wa, wb, wc, g):
    b, s, d = x.shape
    tn = TOKEN_TILE
    tok = lambda bi, si: (bi, si, 0)
    in_specs = [
        pl.BlockSpec((1, tn, d), tok),
        pl.BlockSpec((1, DIFF_WIDTH, tn), lambda bi, si: (bi, 0, si)),
        pl.BlockSpec((1, tn, GMLP_WIDTH), tok),
        pl.BlockSpec((1, tn, GMLP_WIDTH), tok),
        pl.BlockSpec((1, tn, SWA_WIDTH), tok),
    ] + [_const_spec(a.shape) for a in (lng, lnb, ws, bs, wa, wb, wc, g)]
    return pl.pallas_call(
        _mix_out_kernel,
        grid=(b, s // tn), in_specs=in_specs,
        out_specs=pl.BlockSpec((1, tn, d), tok),
        out_shape=jax.ShapeDtypeStruct((b, s, d), F32),
        compiler_params=_params(("arbitrary", "arbitrary")),
        name="mix_out",
    )(x, yat, bu, bv, yc, lng, lnb, ws, bs, wa, wb, wc, g)


def _sigmoid(z):
    return 1.0 / (1.0 + jnp.exp(-z))


def _ffn_ple_kernel(x_ref, p_ref, gpre_ref, wg_ref, wu_ref, wo_ref, gpost_ref,
                    wup_ref, wgate_ref, ggate_ref, gple_ref, o_ref):
    x = x_ref[0]
    h = _rms(x, gpre_ref[...]).astype(BF16)
    f = jnp.zeros(x.shape, F32)
    for j in range(D_FF // FF_CHUNK):
        sl = slice(j * FF_CHUNK, (j + 1) * FF_CHUNK)
        gate = _dot(h, wg_ref[:, sl])
        up = _dot(h, wu_ref[:, sl])
        a = (gate * _sigmoid(gate) * up).astype(BF16)
        f = f + _dot(a, wo_ref[sl, :])
    x = x + _rms(f, gpost_ref[...])
    e = _dot(p_ref[0, 0].astype(BF16), wup_ref[...])
    gt = _sigmoid(_dot(_rms(x, ggate_ref[...]).astype(BF16), wgate_ref[...]))
    o_ref[0] = x + _rms(e * gt, gple_ref[...])


def _ffn_ple_call(x, p, layer, gpre, wg, wu, wo, gpost, wup, wgate, ggate, gple):
    b, s, d = x.shape
    tn = TOKEN_TILE
    tok = lambda bi, si: (bi, si, 0)
    in_specs = [
        pl.BlockSpec((1, tn, d), tok),
        pl.BlockSpec((1, 1, tn, PLE_DIM), lambda bi, si: (layer, bi, si, 0)),
    ] + [_const_spec(a.shape) for a in (gpre, wg, wu, wo, gpost, wup, wgate, ggate, gple)]
    return pl.pallas_call(
        _ffn_ple_kernel,
        grid=(b, s // tn), in_specs=in_specs,
        out_specs=pl.BlockSpec((1, tn, d), tok),
        out_shape=jax.ShapeDtypeStruct((b, s, d), F32),
        compiler_params=_params(("arbitrary", "arbitrary")),
        name="ffn_ple",
    )(x, p, gpre, wg, wu, wo, gpost, wup, wgate, ggate, gple)


def _prep_in_weights(w):
    d = w.shape[0]
    aq = w[:, 0:384].reshape(d, DIFF_HEADS, 2, DIFF_QK_DIM)
    ak = w[:, 384:768].reshape(d, DIFF_HEADS, 2, DIFF_QK_DIM)
    av = w[:, 768:1152].reshape(d, DIFF_HEADS, DIFF_V_DIM)
    pad_qk = ((0, 0), (0, 0), (0, 0), (0, 64 - DIFF_QK_DIM))
    wk = jnp.pad(ak, pad_qk).reshape(d, DIFF_HEADS * KDIM)
    wq = jnp.pad(aq, pad_qk).reshape(d, DIFF_HEADS * KDIM)
    wv = jnp.pad(av, ((0, 0), (0, 0), (0, VT_ROWS - DIFF_V_DIM))).reshape(d, DIFF_HEADS * VT_ROWS)
    wstd = jnp.concatenate([wk, w[:, 1152:]], axis=1).astype(BF16)
    wtr = jnp.concatenate([wq, wv], axis=1).T.astype(BF16)
    return wstd, wtr


def kernel(x, p, g_pre_mix, w_in, lam_q1, lam_k1, lam_q2, lam_k2, g_diff_sub, gmlp_ln_g, gmlp_ln_b,
           w_spatial, b_spatial, swa_sinks, w_out, g_post_mix, g_pre_ffn, w_ffn_in, w_ffn_out,
           g_post_ffn, w_ple_up, w_ple_gate, g_ple_gate, g_ple_post):
    b, s, d = x.shape
    depth = w_in.shape[0]
    assert d == D_MODEL and s % (2 * ATT_QBLOCK) == 0 and s >= 4 * ATT_QBLOCK and s % TOKEN_TILE == 0

    slope2_np, qaug_np, kaug_np, ktab_np, diag_np = _diff_bias_constants(TOKEN_TILE, s)
    _, swa_slopes_np = _alibi_slopes_np()
    slope2 = jnp.asarray(slope2_np)
    qaug = jnp.asarray(qaug_np)
    kaug = jnp.asarray(kaug_np)
    ktab = jnp.asarray(ktab_np)
    diag = jnp.asarray(diag_np)
    swa_slopes = jnp.asarray(swa_slopes_np)
    row = lambda a: a.reshape(1, -1).astype(F32)

    for l in range(depth):
        lam_init = 0.8 - 0.6 * math.exp(-0.3 * l)
        wstd, wtr = _prep_in_weights(w_in[l])
        lamp = jnp.stack([lam_q1[l], lam_k1[l], lam_q2[l], lam_k2[l]]).astype(F32)
        kblk, qt, vtblk, bu, bv, cq, ck, cv, lam_tile = _proj_call(
            x, row(g_pre_mix[l]), wstd, wtr, kaug, ktab, lamp, lam_init)

        gsub_b = jnp.broadcast_to(g_diff_sub[l].astype(F32)[:, None], (DIFF_V_DIM, ATT_QBLOCK))
        yat = _diff_attn_call(slope2, qt, kblk, vtblk, qaug, diag, lam_tile, gsub_b, 1.0 - lam_init)
        yc = _swa_call(swa_slopes, swa_sinks[l].astype(F32), cq, ck, cv)

        bs = jnp.broadcast_to(b_spatial[l].T[:, :, None], (CHUNK, GMLP_GROUPS, GMLP_GROUP_DIM))
        bs = bs.reshape(CHUNK, GMLP_WIDTH).astype(F32)
        wo = w_out[l].astype(BF16)
        x = _mix_out_call(
            x, yat, bu, bv, yc, row(gmlp_ln_g[l]), row(gmlp_ln_b[l]), w_spatial[l].astype(BF16), bs,
            wo[0:DIFF_WIDTH], wo[DIFF_WIDTH:DIFF_WIDTH + GMLP_WIDTH], wo[DIFF_WIDTH + GMLP_WIDTH:],
            row(g_post_mix[l]))

        wfi = w_ffn_in[l].astype(BF16)
        x = _ffn_ple_call(
            x, p, l, row(g_pre_ffn[l]), wfi[:, :D_FF], wfi[:, D_FF:], w_ffn_out[l].astype(BF16),
            row(g_post_ffn[l]), w_ple_up[l].astype(BF16), w_ple_gate[l].astype(BF16),
            row(g_ple_gate[l]), row(g_ple_post[l]))
    return x
```

```python
import functools
import math

import numpy as np
import jax
import jax.numpy as jnp
from jax import lax
from jax.experimental import pallas as pl
from jax.experimental.pallas import tpu as pltpu

D_MODEL = 1024
HEAD_DIM = 64
DIFF_HEADS = 6
DIFF_QK_DIM = 32
DIFF_V_DIM = 64
DIFF_WIDTH = DIFF_HEADS * DIFF_V_DIM
GMLP_GROUPS = 4
GMLP_GROUP_DIM = 64
GMLP_WIDTH = GMLP_GROUPS * GMLP_GROUP_DIM
CHUNK = 128
SWA_HEADS = 6
SWA_KV_HEADS = 2
SWA_GROUP = SWA_HEADS // SWA_KV_HEADS
SWA_WIDTH = SWA_HEADS * HEAD_DIM
WINDOW = 128
SWA_BLOCK = 128
SWA_TILE = 512
D_FF = 2816
PLE_DIM = 256
N_ATTN_HEADS = DIFF_HEADS + SWA_HEADS
ALIBI_MAX_EXP = 8.0
EPS = 1e-6
NEG_INF = -1e30
LOG2E = 1.4426950408889634

LANES = 128
MXU_DIM_V7X = 256
VMEM_LIMIT_BYTES_V7X = 56 * 1024 * 1024

TOKEN_TILE = 512
ATT_BLOCK = MXU_DIM_V7X
ATT_QBLOCK = 2 * ATT_BLOCK
KDIM = 2 * 64
AUG_OFF = DIFF_QK_DIM
ATT_UNROLL_CHOICES = (4, 2)
DYN_ROW0 = 112
DYN_ROWS = KDIM - DYN_ROW0
VT_ROWS = 80
FF_CHUNK = 256

F32 = jnp.float32
BF16 = jnp.bfloat16


def _bf16_round_np(x):
    u = np.asarray(x, np.float32).view(np.uint32).astype(np.uint64)
    r = ((u >> 16) & 1) + 0x7FFF
    return ((u + r) & 0xFFFF0000).astype(np.uint32).view(np.float32)


def _alibi_slopes_np():
    k = np.arange(1, N_ATTN_HEADS + 1, dtype=np.float64)
    s = np.exp2(-ALIBI_MAX_EXP * k / N_ATTN_HEADS).astype(np.float32)
    return s[SWA_HEADS:], s[:SWA_HEADS]


def _split3_bf16(v):
    v = np.asarray(v, np.float32)
    hi = _bf16_round_np(v)
    mid = _bf16_round_np(v - hi)
    lo = _bf16_round_np(v - hi - mid)
    return hi, mid, lo


def _diff_bias_constants(token_tile, seq_len):
    slopes, _ = _alibi_slopes_np()
    slope2 = (slopes.astype(np.float64) * LOG2E).astype(np.float32)
    hi, mid, lo = _split3_bf16(slope2)
    parts = np.stack([hi, mid, lo], axis=1)
    rel = np.arange(ATT_BLOCK, dtype=np.float32)
    qaug = np.zeros((DIFF_HEADS, KDIM, ATT_BLOCK), np.float32)
    kaug = np.zeros((DIFF_HEADS, ATT_BLOCK, KDIM), np.float32)
    for c in range(2):
        base = c * 64 + AUG_OFF
        for t in range(3):
            qaug[:, base + t, :] = rel[None, :]
            qaug[:, base + 3 + t, :] = -parts[:, t][:, None]
            kaug[:, :, base + t] = parts[:, t][:, None]
            kaug[:, :, base + 3 + t] = rel[None, :]
    kaug[:, :, DYN_ROW0:DYN_ROW0 + 3] = 1.0
    kaug = np.tile(kaug, (1, token_tile // ATT_BLOCK, 1))
    qaug = np.tile(qaug, (1, 1, ATT_QBLOCK // ATT_BLOCK))
    n_kb = seq_len // ATT_BLOCK
    sigma_j = (slope2[:, None] * np.float32(ATT_BLOCK)) * np.arange(n_kb, dtype=np.float32)[None, :]
    ktab = np.zeros((DIFF_HEADS, n_kb, 1, KDIM), np.float32)
    for t, piece in enumerate(_split3_bf16(sigma_j)):
        ktab[:, :, 0, DYN_ROW0 + 3 + t] = piece
    pos = np.arange(ATT_QBLOCK, dtype=np.float32)
    dist = np.abs(pos[:, None] - pos[None, :])
    diag = -(slope2[:, None, None] * dist[None])
    return slope2, qaug, kaug, ktab, diag.astype(np.float32)


def _swa_bias_np(slopes):
    key = np.arange(3 * SWA_BLOCK, dtype=np.float32)[:, None]
    qry = np.arange(SWA_BLOCK, dtype=np.float32)[None, :]
    dist = np.abs(key - SWA_BLOCK - qry)
    bias = -(slopes.astype(np.float32)[:, None, None] * dist[None])
    bias = np.where(dist[None] <= WINDOW, bias, np.float32(NEG_INF)).astype(np.float32)
    return np.concatenate(list(bias), axis=1)


def _rms(x, g):
    return x * lax.rsqrt(jnp.mean(x * x, axis=-1, keepdims=True) + EPS) * g


def _dot(a, b):
    return jnp.dot(a, b, preferred_element_type=F32)


def _dot_nt(a, b):
    return lax.dot_general(a, b, (((1,), (1,)), ((), ())), preferred_element_type=F32)


def _dot_tn(a, b):
    return lax.dot_general(a, b, (((0,), (0,)), ((), ())), preferred_element_type=F32)


def _const_spec(shape):
    nd = len(shape)
    return pl.BlockSpec(shape, lambda *_: (0,) * nd, pipeline_mode=pl.Buffered(1))


def _params(sem, flags=None):
    return pltpu.CompilerParams(dimension_semantics=sem, vmem_limit_bytes=VMEM_LIMIT_BYTES_V7X, flags=flags)


N_STD = DIFF_HEADS * KDIM + 2 * GMLP_WIDTH + SWA_KV_HEADS * HEAD_DIM
N_TR = DIFF_HEADS * KDIM + DIFF_HEADS * VT_ROWS + SWA_WIDTH + SWA_KV_HEADS * VT_ROWS
Q_SCALE = (DIFF_QK_DIM ** -0.5) * LOG2E


def _proj_kernel(x_ref, g_ref, wstd_ref, wtr_ref, kaug_ref, ktab_ref, lamp_ref,
                 k_ref, qt_ref, vt_ref, bu_ref, bv_ref, cqt_ref, ck_ref, cvt_ref, lam_ref, *, lam_init):
    tn = x_ref.shape[1]
    nsub = tn // ATT_BLOCK
    h = _rms(x_ref[0], g_ref[...]).astype(BF16)
    r1 = _dot(h, wstd_ref[...])
    for hh in range(DIFF_HEADS):
        kk = r1[:, hh * KDIM:(hh + 1) * KDIM] + kaug_ref[hh]
        for j in range(nsub):
            k_ref[0, hh, j] = (kk[j * ATT_BLOCK:(j + 1) * ATT_BLOCK] + ktab_ref[hh, j]).astype(BF16)
    o = DIFF_HEADS * KDIM
    bu_ref[0] = r1[:, o:o + GMLP_WIDTH]
    o += GMLP_WIDTH
    bv_ref[0] = r1[:, o:o + GMLP_WIDTH]
    o += GMLP_WIDTH
    ck_ref[0] = r1[:, o:o + 128].astype(BF16)

    r2 = _dot_nt(wtr_ref[...], h)
    ones_row = jnp.where(lax.broadcasted_iota(jnp.int32, (VT_ROWS, ATT_BLOCK), 0) == DIFF_V_DIM, 1.0, 0.0)
    vo = DIFF_HEADS * KDIM
    for hh in range(DIFF_HEADS):
        qt_ref[0, hh] = (r2[hh * KDIM:(hh + 1) * KDIM] * Q_SCALE).astype(BF16)
        vv = r2[vo + hh * VT_ROWS: vo + (hh + 1) * VT_ROWS]
        for j in range(nsub):
            vt_ref[0, hh, j] = (vv[:, j * ATT_BLOCK:(j + 1) * ATT_BLOCK] + ones_row).astype(BF16)
    o = vo + DIFF_HEADS * VT_ROWS
    cqt_ref[0] = r2[o:o + SWA_WIDTH].astype(BF16)
    o += SWA_WIDTH
    ones_rows = jnp.where(lax.broadcasted_iota(jnp.int32, (SWA_KV_HEADS * VT_ROWS, tn), 0) % VT_ROWS == HEAD_DIM,
                          1.0, 0.0)
    cvt_ref[0] = (r2[o:o + SWA_KV_HEADS * VT_ROWS] + ones_rows).astype(BF16)

    lp = lamp_ref[...]
    s1 = jnp.sum(lp[0:1] * lp[1:2], axis=-1, keepdims=True)
    s2 = jnp.sum(lp[2:3] * lp[3:4], axis=-1, keepdims=True)
    lam = jnp.exp(s1) - jnp.exp(s2) + lam_init
    lam_ref[...] = jnp.broadcast_to(lam, lam_ref.shape)


def _proj_call(x, g, wstd, wtr, kaug, ktab, lamp, lam_init):
    b, s, d = x.shape
    tn = TOKEN_TILE
    nsub = tn // ATT_BLOCK
    nkb = s // ATT_BLOCK
    grid = (b, s // tn)
    tok = lambda bi, si: (bi, si, 0)
    out_shape = (
        jax.ShapeDtypeStruct((b, DIFF_HEADS, nkb, ATT_BLOCK, KDIM), BF16),
        jax.ShapeDtypeStruct((b, DIFF_HEADS, KDIM, s), BF16),
        jax.ShapeDtypeStruct((b, DIFF_HEADS, nkb, VT_ROWS, ATT_BLOCK), BF16),
        jax.ShapeDtypeStruct((b, s, GMLP_WIDTH), F32),
        jax.ShapeDtypeStruct((b, s, GMLP_WIDTH), F32),
        jax.ShapeDtypeStruct((b, SWA_WIDTH, s), BF16),
        jax.ShapeDtypeStruct((b, s, 128), BF16),
        jax.ShapeDtypeStruct((b, SWA_KV_HEADS * VT_ROWS, s), BF16),
        jax.ShapeDtypeStruct((8, LANES), F32),
    )
    out_specs = (
        pl.BlockSpec((1, DIFF_HEADS, nsub, ATT_BLOCK, KDIM), lambda bi, si: (bi, 0, si, 0, 0)),
        pl.BlockSpec((1, DIFF_HEADS, KDIM, tn), lambda bi, si: (bi, 0, 0, si)),
        pl.BlockSpec((1, DIFF_HEADS, nsub, VT_ROWS, ATT_BLOCK), lambda bi, si: (bi, 0, si, 0, 0)),
        pl.BlockSpec((1, tn, GMLP_WIDTH), tok),
        pl.BlockSpec((1, tn, GMLP_WIDTH), tok),
        pl.BlockSpec((1, SWA_WIDTH, tn), lambda bi, si: (bi, 0, si)),
        pl.BlockSpec((1, tn, 128), tok),
        pl.BlockSpec((1, SWA_KV_HEADS * VT_ROWS, tn), lambda bi, si: (bi, 0, si)),
        pl.BlockSpec((8, LANES), lambda bi, si: (0, 0)),
    )
    in_specs = [
        pl.BlockSpec((1, tn, d), tok),
        _const_spec(g.shape),
        _const_spec(wstd.shape),
        _const_spec(wtr.shape),
        _const_spec(kaug.shape),
        pl.BlockSpec((DIFF_HEADS, nsub, 1, KDIM), lambda bi, si: (0, si, 0, 0)),
        _const_spec(lamp.shape),
    ]
    return pl.pallas_call(
        functools.partial(_proj_kernel, lam_init=lam_init),
        grid=grid, in_specs=in_specs, out_specs=out_specs, out_shape=out_shape,
        compiler_params=_params(("arbitrary", "arbitrary")),
        name="proj",
    )(x, g, wstd, wtr, kaug, ktab, lamp)


def _diff_attn_kernel(slope_ref, qt_ref, k_ref, vt_ref, qaug_ref, diag_ref, lam_ref, gsub_ref,
                      o_ref, w_ref, s0_ref, s1_ref, p0_ref, p1_ref, acc_ref, *, out_scale):
    hh = pl.program_id(1)
    qi = pl.program_id(2)
    blk = ATT_BLOCK
    tq = ATT_QBLOCK
    n_rest = k_ref.shape[2] // 2 - 1
    sigma = slope_ref[hh] * float(blk)

    qt = qt_ref[0, 0].astype(F32)
    qaug = qaug_ref[0]
    row = lax.broadcasted_iota(jnp.int32, (KDIM, tq), 0)
    comp_mask = (row < 64, row >= 64)
    zero = jnp.zeros_like(qt)
    for c in range(2):
        w_ref[0, c] = jnp.where(comp_mask[c], qt - qaug, zero).astype(BF16)
        w_ref[1, c] = jnp.where(comp_mask[c], qt + qaug, zero).astype(BF16)

    def key_rows(kb):
        return jnp.concatenate([k_ref[0, 0, 2 * kb], k_ref[0, 0, 2 * kb + 1]], axis=0)

    def value_cols(kb):
        return jnp.concatenate([vt_ref[0, 0, 2 * kb], vt_ref[0, 0, 2 * kb + 1]], axis=1)

    k_t = key_rows(qi)
    vt = value_cols(qi)
    m = []
    for c in range(2):
        s = _dot(k_t, jnp.where(comp_mask[c], qt, zero).astype(BF16)) + diag_ref[0]
        m_c = jnp.max(s, axis=0, keepdims=True)
        acc_ref[c] = _dot(vt, jnp.exp2(s - m_c).astype(BF16))
        m.append(m_c)

    dyn_row = lax.broadcasted_iota(jnp.int32, (DYN_ROWS, tq), 0)
    lane_row = lax.broadcasted_iota(jnp.int32, (1, tq), 1)
    q_origin = sigma * jnp.where(lane_row < blk, 2 * qi, 2 * qi + 1).astype(F32)

    def rest_block(j):
        after = (j >= qi).astype(jnp.int32)
        return j + after, after

    def reference_rows(rr, key_sign):
        hi = rr.astype(BF16).astype(F32)
        mid = (rr - hi).astype(BF16).astype(F32)
        lo = rr - hi - mid
        rows = jnp.where(dyn_row == 0, hi, jnp.where(dyn_row == 1, mid, jnp.where(dyn_row == 2, lo, 0.0)))
        rows = jnp.where((dyn_row >= 3) & (dyn_row < 6), key_sign, rows)
        return rows.astype(BF16)

    def column_max(z):
        parts = [z[i * 16:(i + 1) * 16] for i in range(z.shape[0] // 16)]
        while len(parts) > 1:
            parts = [jnp.maximum(parts[i], parts[i + 1]) for i in range(0, len(parts), 2)]
        return jnp.max(parts[0].astype(F32), axis=0, keepdims=True)

    def stage_scores(j, ref_max, s_buf):
        kb, side = rest_block(j)
        sgn = jnp.where(side == 1, 1.0, -1.0)
        k_t = key_rows(kb)
        cms = []
        for c in range(2):
            dyn = reference_rows(sgn * q_origin - ref_max[c], -sgn)
            w = jnp.concatenate([w_ref[side, c, 0:DYN_ROW0, :], dyn], axis=0)
            z = _dot(k_t, w).astype(BF16)
            s_buf[c] = z
            cms.append(column_max(z))
        return cms

    def stage_softmax(m2, m1, s_buf, cms, p_buf):
        alphas, m_out = [], []
        for c in range(2):
            d = jnp.maximum(m1[c] - m2[c], cms[c]).astype(BF16)
            m_new = m2[c] + d.astype(F32)
            p_buf[c] = jnp.exp2(s_buf[c] - d)
            alphas.append(jnp.exp2(m1[c] - m_new))
            m_out.append(m_new)
        return alphas, m_out

    def stage_pv(j, p_buf, alphas):
        kb, _ = rest_block(j)
        vt = value_cols(kb)
        for c in range(2):
            acc_ref[c] = acc_ref[c] * alphas[c] + _dot(vt, p_buf[c])

    s_bufs = (s0_ref, s1_ref)
    p_bufs = (p0_ref, p1_ref)

    def full_step(t, slot, state):
        m_t3, m_t2, cm_prev, al_prev = state
        al_new, m_t1 = stage_softmax(m_t3, m_t2, s_bufs[1 - slot], cm_prev, p_bufs[1 - slot])
        stage_pv(t - 2, p_bufs[slot], al_prev)
        cm_new = stage_scores(t, m_t2, s_bufs[slot])
        return m_t2, m_t1, cm_new, al_new

    cm_a = stage_scores(0, m, s0_ref)
    cm_b = stage_scores(1, m, s1_ref)
    al_a, m_a = stage_softmax(m, m, s0_ref, cm_a, p0_ref)
    state = full_step(2, 0, (m, m_a, cm_b, al_a))

    n_loop = n_rest - 3
    unroll = next(u for u in ATT_UNROLL_CHOICES if n_loop % u == 0)

    def steady(u, state):
        for i in range(unroll):
            state = full_step(3 + unroll * u + i, (1 + i) % 2, state)
        return state

    state = lax.fori_loop(0, n_loop // unroll, steady, state)

    m_t3, m_t2, cm_last, al_prev = state
    al_last, _ = stage_softmax(m_t3, m_t2, s0_ref, cm_last, p0_ref)
    stage_pv(n_rest - 2, p1_ref, al_prev)
    stage_pv(n_rest - 1, p0_ref, al_last)

    a0 = acc_ref[0]
    a1 = acc_ref[1]
    o0 = a0[0:DIFF_V_DIM] / a0[DIFF_V_DIM:DIFF_V_DIM + 1]
    o1 = a1[0:DIFF_V_DIM] / a1[DIFF_V_DIM:DIFF_V_DIM + 1]
    lam = lam_ref[0:1, 0:1]
    o = o0 - lam * o1
    ms = jnp.mean(o * o, axis=0, keepdims=True)
    y = o * lax.rsqrt(ms + EPS) * gsub_ref[...] * out_scale
    o_ref[0] = y.astype(o_ref.dtype)


def _diff_attn_call(slope2, qt, kblk, vtblk, qaug, diag, lam_tile, gsub_b, out_scale):
    b, nh, kdim, s = qt.shape
    nkb = kblk.shape[2]
    blk = ATT_BLOCK
    tq = ATT_QBLOCK
    grid = (b, nh, s // tq)
    in_specs = [
        pl.BlockSpec(memory_space=pltpu.SMEM),
        pl.BlockSpec((1, 1, kdim, tq), lambda bi, hi, qi: (bi, hi, 0, qi)),
        pl.BlockSpec((1, 1, nkb, blk, kdim), lambda bi, hi, qi: (bi, hi, 0, 0, 0)),
        pl.BlockSpec((1, 1, nkb, VT_ROWS, blk), lambda bi, hi, qi: (bi, hi, 0, 0, 0)),
        pl.BlockSpec((1, kdim, tq), lambda bi, hi, qi: (hi, 0, 0)),
        pl.BlockSpec((1, tq, tq), lambda bi, hi, qi: (hi, 0, 0)),
        _const_spec(lam_tile.shape),
        _const_spec(gsub_b.shape),
    ]
    s_buf = pltpu.VMEM((2, tq, tq), BF16)
    p_buf = pltpu.VMEM((2, tq, tq), BF16)
    return pl.pallas_call(
        functools.partial(_diff_attn_kernel, out_scale=out_scale),
        grid=grid, in_specs=in_specs,
        out_specs=pl.BlockSpec((1, DIFF_V_DIM, tq), lambda bi, hi, qi: (bi, hi, qi)),
        out_shape=jax.ShapeDtypeStruct((b, nh * DIFF_V_DIM, s), BF16),
        scratch_shapes=[pltpu.VMEM((2, 2, kdim, tq), BF16),
                        s_buf, s_buf, p_buf, p_buf, pltpu.VMEM((2, VT_ROWS, tq), F32)],
        compiler_params=_params(("arbitrary", "arbitrary", "arbitrary")),
        name="diff_attn",
    )(slope2, qt, kblk, vtblk, qaug, diag, lam_tile, gsub_b)


def _swa_kernel(sink_ref, qt_ref, kp_ref, kc_ref, kn_ref, vp_ref, vc_ref, vn_ref, bias_ref, o_ref, *, seq_len):
    qi = pl.program_id(1)
    blk = SWA_BLOCK
    nsub = SWA_TILE // blk
    k_all = jnp.concatenate([kp_ref[0], kc_ref[0], kn_ref[0]], axis=0)
    vt_all = jnp.concatenate([vp_ref[0], vc_ref[0], vn_ref[0]], axis=1)
    key_row = lax.broadcasted_iota(jnp.int32, (3 * blk, SWA_HEADS * blk), 0)
    no_q = jnp.zeros((HEAD_DIM, SWA_GROUP * blk), BF16)
    sink = sink_ref[...]
    gw = SWA_GROUP * blk
    for sub in range(nsub):
        key_pos = (qi * nsub + sub - 1) * blk + key_row
        in_seq = (key_pos >= 0) & (key_pos < seq_len)
        kk = k_all[sub * blk:(sub + 3) * blk]
        q_t = [qt_ref[0, hq * HEAD_DIM:(hq + 1) * HEAD_DIM, sub * blk:(sub + 1) * blk] for hq in range(SWA_HEADS)]
        w = jnp.concatenate([jnp.concatenate(q_t[:SWA_GROUP] + [no_q], axis=1),
                             jnp.concatenate([no_q] + q_t[SWA_GROUP:], axis=1)], axis=0)
        sc = _dot(kk, w) + bias_ref[...]
        sc = jnp.where(in_seq, sc, NEG_INF)
        m = jnp.maximum(jnp.max(sc, axis=0, keepdims=True), sink)
        e = jnp.exp(sc - m).astype(BF16)
        tail = jnp.exp(sink - m)
        for kh in range(SWA_KV_HEADS):
            vt = vt_all[kh * VT_ROWS:(kh + 1) * VT_ROWS, sub * blk:(sub + 3) * blk]
            acc = _dot(vt, e[:, kh * gw:(kh + 1) * gw])
            o = acc[0:HEAD_DIM] / (acc[HEAD_DIM:HEAD_DIM + 1] + tail[:, kh * gw:(kh + 1) * gw])
            for g in range(SWA_GROUP):
                hq = kh * SWA_GROUP + g
                o_ref[0, hq * HEAD_DIM:(hq + 1) * HEAD_DIM, sub * blk:(sub + 1) * blk] = (
                    o[:, g * blk:(g + 1) * blk].astype(o_ref.dtype))


def _swa_call(sinks, cqt, ck, cvt, bias):
    b, _, s = cqt.shape
    blk = SWA_BLOCK
    tile = SWA_TILE
    nsub = tile // blk
    nb = s // blk
    prev_i = lambda qi: jnp.maximum(qi * nsub - 1, 0)
    next_i = lambda qi: jnp.minimum((qi + 1) * nsub, nb - 1)
    vrows = SWA_KV_HEADS * VT_ROWS
    in_specs = [
        _const_spec(sinks.shape),
        pl.BlockSpec((1, SWA_WIDTH, tile), lambda bi, qi: (bi, 0, qi)),
        pl.BlockSpec((1, blk, 128), lambda bi, qi: (bi, prev_i(qi), 0)),
        pl.BlockSpec((1, tile, 128), lambda bi, qi: (bi, qi, 0)),
        pl.BlockSpec((1, blk, 128), lambda bi, qi: (bi, next_i(qi), 0)),
        pl.BlockSpec((1, vrows, blk), lambda bi, qi: (bi, 0, prev_i(qi))),
        pl.BlockSpec((1, vrows, tile), lambda bi, qi: (bi, 0, qi)),
        pl.BlockSpec((1, vrows, blk), lambda bi, qi: (bi, 0, next_i(qi))),
        _const_spec(bias.shape),
    ]
    return pl.pallas_call(
        functools.partial(_swa_kernel, seq_len=s),
        grid=(b, s // tile), in_specs=in_specs,
        out_specs=pl.BlockSpec((1, SWA_WIDTH, tile), lambda bi, qi: (bi, 0, qi)),
        out_shape=jax.ShapeDtypeStruct((b, SWA_WIDTH, s), BF16),
        compiler_params=_params(("arbitrary", "arbitrary")),
        name="swa",
    )(sinks, cqt, ck, ck, ck, cvt, cvt, cvt, bias)


def _mix_out_kernel(x_ref, yat_ref, bu_ref, bv_ref, yct_ref, lng_ref, lnb_ref, ws_ref, bs_ref,
                    wa_ref, wb_ref, wc_ref, g_ref, o_ref):
    tn = x_ref.shape[1]
    v = bv_ref[0]
    mu = jnp.mean(v, axis=-1, keepdims=True)
    var = jnp.mean(jnp.square(v - mu), axis=-1, keepdims=True)
    vn = ((v - mu) * lax.rsqrt(var + EPS) * lng_ref[...] + lnb_ref[...]).astype(BF16)
    lane_group = lax.broadcasted_iota(jnp.int32, (CHUNK, GMLP_WIDTH), 1) // GMLP_GROUP_DIM
    u = bu_ref[0]
    yb = []
    for c in range(tn // CHUNK):
        vc = vn[c * CHUNK:(c + 1) * CHUNK]
        mixed = bs_ref[...]
        for g in range(GMLP_GROUPS):
            mixed = mixed + jnp.where(lane_group == g, _dot(ws_ref[g], vc), 0.0)
        yb.append(u[c * CHUNK:(c + 1) * CHUNK] * mixed)
    yb = jnp.concatenate(yb, axis=0).astype(BF16)
    y = _dot_tn(yat_ref[0], wa_ref[...]) + _dot(yb, wb_ref[...]) + _dot_tn(yct_ref[0], wc_ref[...])
    o_ref[0] = x_ref[0] + _rms(y, g_ref[...])


def _mix_out_call(x, yat, bu, bv, yc, lng, lnb, ws, bs, wa, wb, wc, g):
    b, s, d = x.shape
    tn = TOKEN_TILE
    tok = lambda bi, si: (bi, si, 0)
    in_specs = [
        pl.BlockSpec((1, tn, d), tok),
        pl.BlockSpec((1, DIFF_WIDTH, tn), lambda bi, si: (bi, 0, si)),
        pl.BlockSpec((1, tn, GMLP_WIDTH), tok),
        pl.BlockSpec((1, tn, GMLP_WIDTH), tok),
        pl.BlockSpec((1, SWA_WIDTH, tn), lambda bi, si: (bi, 0, si)),
    ] + [_const_spec(a.shape) for a in (lng, lnb, ws, bs, wa, wb, wc, g)]
    return pl.pallas_call(
        _mix_out_kernel,
        grid=(b, s // tn), in_specs=in_specs,
        out_specs=pl.BlockSpec((1, tn, d), tok),
        out_shape=jax.ShapeDtypeStruct((b, s, d), F32),
        compiler_params=_params(("arbitrary", "arbitrary")),
        name="mix_out",
    )(x, yat, bu, bv, yc, lng, lnb, ws, bs, wa, wb, wc, g)


def _sigmoid(z):
    return 1.0 / (1.0 + jnp.exp(-z))


def _ffn_ple_kernel(x_ref, p_ref, gpre_ref, wg_ref, wu_ref, wo_ref, gpost_ref,
                    wup_ref, wgate_ref, ggate_ref, gple_ref, o_ref):
    x = x_ref[0]
    h = _rms(x, gpre_ref[...]).astype(BF16)
    f = jnp.zeros(x.shape, F32)
    for j in range(D_FF // FF_CHUNK):
        sl = slice(j * FF_CHUNK, (j + 1) * FF_CHUNK)
        gate = _dot(h, wg_ref[:, sl])
        up = _dot(h, wu_ref[:, sl])
        a = (gate * _sigmoid(gate) * up).astype(BF16)
        f = f + _dot(a, wo_ref[sl, :])
    x = x + _rms(f, gpost_ref[...])
    e = _dot(p_ref[0, 0].astype(BF16), wup_ref[...])
    gt = _sigmoid(_dot(_rms(x, ggate_ref[...]).astype(BF16), wgate_ref[...]))
    o_ref[0] = x + _rms(e * gt, gple_ref[...])


def _ffn_ple_call(x, p, layer, gpre, wg, wu, wo, gpost, wup, wgate, ggate, gple):
    b, s, d = x.shape
    tn = TOKEN_TILE
    tok = lambda bi, si: (bi, si, 0)
    in_specs = [
        pl.BlockSpec((1, tn, d), tok),
        pl.BlockSpec((1, 1, tn, PLE_DIM), lambda bi, si: (layer, bi, si, 0)),
    ] + [_const_spec(a.shape) for a in (gpre, wg, wu, wo, gpost, wup, wgate, ggate, gple)]
    return pl.pallas_call(
        _ffn_ple_kernel,
        grid=(b, s // tn), in_specs=in_specs,
        out_specs=pl.BlockSpec((1, tn, d), tok),
        out_shape=jax.ShapeDtypeStruct((b, s, d), F32),
        compiler_params=_params(("arbitrary", "arbitrary")),
        name="ffn_ple",
    )(x, p, gpre, wg, wu, wo, gpost, wup, wgate, ggate, gple)


def _prep_in_weights(w):
    d = w.shape[0]
    aq = w[:, 0:384].reshape(d, DIFF_HEADS, 2, DIFF_QK_DIM)
    ak = w[:, 384:768].reshape(d, DIFF_HEADS, 2, DIFF_QK_DIM)
    av = w[:, 768:1152].reshape(d, DIFF_HEADS, DIFF_V_DIM)
    pad_qk = ((0, 0), (0, 0), (0, 0), (0, 64 - DIFF_QK_DIM))
    wk = jnp.pad(ak, pad_qk).reshape(d, DIFF_HEADS * KDIM)
    wq = jnp.pad(aq, pad_qk).reshape(d, DIFF_HEADS * KDIM)
    wv = jnp.pad(av, ((0, 0), (0, 0), (0, VT_ROWS - DIFF_V_DIM))).reshape(d, DIFF_HEADS * VT_ROWS)
    cq = w[:, 1664:2048] * (HEAD_DIM ** -0.5)
    cv = w[:, 2176:2304].reshape(d, SWA_KV_HEADS, HEAD_DIM)
    cv = jnp.pad(cv, ((0, 0), (0, 0), (0, VT_ROWS - HEAD_DIM))).reshape(d, SWA_KV_HEADS * VT_ROWS)
    wstd = jnp.concatenate([wk, w[:, 1152:1664], w[:, 2048:2176]], axis=1).astype(BF16)
    wtr = jnp.concatenate([wq, wv, cq, cv], axis=1).T.astype(BF16)
    return wstd, wtr


def kernel(x, p, g_pre_mix, w_in, lam_q1, lam_k1, lam_q2, lam_k2, g_diff_sub, gmlp_ln_g, gmlp_ln_b,
           w_spatial, b_spatial, swa_sinks, w_out, g_post_mix, g_pre_ffn, w_ffn_in, w_ffn_out,
           g_post_ffn, w_ple_up, w_ple_gate, g_ple_gate, g_ple_post):
    b, s, d = x.shape
    depth = w_in.shape[0]
    assert d == D_MODEL and s % (2 * ATT_QBLOCK) == 0 and s >= 4 * ATT_QBLOCK and s % TOKEN_TILE == 0

    slope2_np, qaug_np, kaug_np, ktab_np, diag_np = _diff_bias_constants(TOKEN_TILE, s)
    _, swa_slopes_np = _alibi_slopes_np()
    slope2 = jnp.asarray(slope2_np)
    qaug = jnp.asarray(qaug_np)
    kaug = jnp.asarray(kaug_np)
    ktab = jnp.asarray(ktab_np)
    diag = jnp.asarray(diag_np)
    swa_bias = jnp.asarray(_swa_bias_np(swa_slopes_np))
    row = lambda a: a.reshape(1, -1).astype(F32)

    for l in range(depth):
        lam_init = 0.8 - 0.6 * math.exp(-0.3 * l)
        wstd, wtr = _prep_in_weights(w_in[l])
        lamp = jnp.stack([lam_q1[l], lam_k1[l], lam_q2[l], lam_k2[l]]).astype(F32)
        kblk, qt, vtblk, bu, bv, cq, ck, cv, lam_tile = _proj_call(
            x, row(g_pre_mix[l]), wstd, wtr, kaug, ktab, lamp, lam_init)

        gsub_b = jnp.broadcast_to(g_diff_sub[l].astype(F32)[:, None], (DIFF_V_DIM, ATT_QBLOCK))
        yat = _diff_attn_call(slope2, qt, kblk, vtblk, qaug, diag, lam_tile, gsub_b, 1.0 - lam_init)
        sink_row = jnp.repeat(swa_sinks[l].astype(F32), SWA_BLOCK).reshape(1, SWA_HEADS * SWA_BLOCK)
        yc = _swa_call(sink_row, cq, ck, cv, swa_bias)

        bs = jnp.broadcast_to(b_spatial[l].T[:, :, None], (CHUNK, GMLP_GROUPS, GMLP_GROUP_DIM))
        bs = bs.reshape(CHUNK, GMLP_WIDTH).astype(F32)
        wo = w_out[l].astype(BF16)
        x = _mix_out_call(
            x, yat, bu, bv, yc, row(gmlp_ln_g[l]), row(gmlp_ln_b[l]), w_spatial[l].astype(BF16), bs,
            wo[0:DIFF_WIDTH], wo[DIFF_WIDTH:DIFF_WIDTH + GMLP_WIDTH], wo[DIFF_WIDTH + GMLP_WIDTH:],
            row(g_post_mix[l]))

        wfi = w_ffn_in[l].astype(BF16)
        x = _ffn_ple_call(
            x, p, l, row(g_pre_ffn[l]), wfi[:, :D_FF], wfi[:, D_FF:], w_ffn_out[l].astype(BF16),
            row(g_post_ffn[l]), w_ple_up[l].astype(BF16), w_ple_gate[l].astype(BF16),
            row(g_ple_gate[l]), row(g_ple_post[l]))
    return x
```

```python
import functools
import math

import numpy as np
import jax
import jax.numpy as jnp
from jax import lax
from jax.experimental import pallas as pl
from jax.experimental.pallas import tpu as pltpu

D_MODEL = 1024
HEAD_DIM = 64
DIFF_HEADS = 6
DIFF_QK_DIM = 32
DIFF_V_DIM = 64
DIFF_WIDTH = DIFF_HEADS * DIFF_V_DIM
GMLP_GROUPS = 4
GMLP_GROUP_DIM = 64
GMLP_WIDTH = GMLP_GROUPS * GMLP_GROUP_DIM
CHUNK = 128
SWA_HEADS = 6
SWA_KV_HEADS = 2
SWA_GROUP = SWA_HEADS // SWA_KV_HEADS
SWA_WIDTH = SWA_HEADS * HEAD_DIM
WINDOW = 128
SWA_BLOCK = 128
SWA_TILE = 512
D_FF = 2816
PLE_DIM = 256
N_ATTN_HEADS = DIFF_HEADS + SWA_HEADS
ALIBI_MAX_EXP = 8.0
EPS = 1e-6
NEG_INF = -1e30
LOG2E = 1.4426950408889634

LANES = 128
MXU_DIM_V7X = 256
VMEM_LIMIT_BYTES_V7X = 56 * 1024 * 1024

TOKEN_TILE = 512
ATT_BLOCK = MXU_DIM_V7X
ATT_QBLOCK = 2 * ATT_BLOCK
KDIM = 2 * 64
AUG_OFF = DIFF_QK_DIM
ATT_UNROLL_CHOICES = (14, 4, 2)
DYN_ROW0 = 112
DYN_ROWS = KDIM - DYN_ROW0
VT_ROWS = 80
FF_CHUNK = 256

F32 = jnp.float32
BF16 = jnp.bfloat16


def _bf16_round_np(x):
    u = np.asarray(x, np.float32).view(np.uint32).astype(np.uint64)
    r = ((u >> 16) & 1) + 0x7FFF
    return ((u + r) & 0xFFFF0000).astype(np.uint32).view(np.float32)


def _alibi_slopes_np():
    k = np.arange(1, N_ATTN_HEADS + 1, dtype=np.float64)
    s = np.exp2(-ALIBI_MAX_EXP * k / N_ATTN_HEADS).astype(np.float32)
    return s[SWA_HEADS:], s[:SWA_HEADS]


def _split3_bf16(v):
    v = np.asarray(v, np.float32)
    hi = _bf16_round_np(v)
    mid = _bf16_round_np(v - hi)
    lo = _bf16_round_np(v - hi - mid)
    return hi, mid, lo


def _diff_bias_constants(token_tile, seq_len):
    slopes, _ = _alibi_slopes_np()
    slope2 = (slopes.astype(np.float64) * LOG2E).astype(np.float32)
    hi, mid, lo = _split3_bf16(slope2)
    parts = np.stack([hi, mid, lo], axis=1)
    rel = np.arange(ATT_BLOCK, dtype=np.float32)
    qaug = np.zeros((DIFF_HEADS, KDIM, ATT_BLOCK), np.float32)
    kaug = np.zeros((DIFF_HEADS, ATT_BLOCK, KDIM), np.float32)
    for c in range(2):
        base = c * 64 + AUG_OFF
        for t in range(3):
            qaug[:, base + t, :] = rel[None, :]
            qaug[:, base + 3 + t, :] = -parts[:, t][:, None]
            kaug[:, :, base + t] = parts[:, t][:, None]
            kaug[:, :, base + 3 + t] = rel[None, :]
    kaug[:, :, DYN_ROW0:DYN_ROW0 + 3] = 1.0
    kaug = np.tile(kaug, (1, token_tile // ATT_BLOCK, 1))
    qaug = np.tile(qaug, (1, 1, ATT_QBLOCK // ATT_BLOCK))
    n_kb = seq_len // ATT_BLOCK
    sigma_j = (slope2[:, None] * np.float32(ATT_BLOCK)) * np.arange(n_kb, dtype=np.float32)[None, :]
    ktab = np.zeros((DIFF_HEADS, n_kb, 1, KDIM), np.float32)
    for t, piece in enumerate(_split3_bf16(sigma_j)):
        ktab[:, :, 0, DYN_ROW0 + 3 + t] = piece
    pos = np.arange(ATT_QBLOCK, dtype=np.float32)
    dist = np.abs(pos[:, None] - pos[None, :])
    diag = -(slope2[:, None, None] * dist[None])
    return slope2, qaug, kaug, ktab, diag.astype(np.float32)


def _swa_bias_np(slopes):
    key = np.arange(3 * SWA_BLOCK, dtype=np.float32)[:, None]
    qry = np.arange(SWA_BLOCK, dtype=np.float32)[None, :]
    dist = np.abs(key - SWA_BLOCK - qry)
    bias = -(slopes.astype(np.float32)[:, None, None] * dist[None])
    bias = np.where(dist[None] <= WINDOW, bias, np.float32(NEG_INF)).astype(np.float32)
    return np.concatenate(list(bias), axis=1)


def _rms(x, g):
    return x * lax.rsqrt(jnp.mean(x * x, axis=-1, keepdims=True) + EPS) * g


def _dot(a, b):
    return jnp.dot(a, b, preferred_element_type=F32)


def _dot_nt(a, b):
    return lax.dot_general(a, b, (((1,), (1,)), ((), ())), preferred_element_type=F32)


def _dot_tn(a, b):
    return lax.dot_general(a, b, (((0,), (0,)), ((), ())), preferred_element_type=F32)


def _const_spec(shape):
    nd = len(shape)
    return pl.BlockSpec(shape, lambda *_: (0,) * nd, pipeline_mode=pl.Buffered(1))


def _params(sem, flags=None):
    return pltpu.CompilerParams(dimension_semantics=sem, vmem_limit_bytes=VMEM_LIMIT_BYTES_V7X, flags=flags)


N_STD = DIFF_HEADS * KDIM + 2 * GMLP_WIDTH + SWA_KV_HEADS * HEAD_DIM
N_TR = DIFF_HEADS * KDIM + DIFF_HEADS * VT_ROWS + SWA_WIDTH + SWA_KV_HEADS * VT_ROWS
Q_SCALE = (DIFF_QK_DIM ** -0.5) * LOG2E


def _proj_kernel(x_ref, g_ref, wstd_ref, wtr_ref, kaug_ref, ktab_ref, lamp_ref,
                 k_ref, qt_ref, vt_ref, bu_ref, bv_ref, cqt_ref, ck_ref, cvt_ref, lam_ref, *, lam_init):
    tn = x_ref.shape[1]
    nsub = tn // ATT_BLOCK
    h = _rms(x_ref[0], g_ref[...]).astype(BF16)
    r1 = _dot(h, wstd_ref[...])
    for hh in range(DIFF_HEADS):
        kk = r1[:, hh * KDIM:(hh + 1) * KDIM] + kaug_ref[hh]
        for j in range(nsub):
            k_ref[0, hh, j] = (kk[j * ATT_BLOCK:(j + 1) * ATT_BLOCK] + ktab_ref[hh, j]).astype(BF16)
    o = DIFF_HEADS * KDIM
    bu_ref[0] = r1[:, o:o + GMLP_WIDTH]
    o += GMLP_WIDTH
    bv_ref[0] = r1[:, o:o + GMLP_WIDTH]
    o += GMLP_WIDTH
    ck_ref[0] = r1[:, o:o + 128].astype(BF16)

    r2 = _dot_nt(wtr_ref[...], h)
    ones_row = jnp.where(lax.broadcasted_iota(jnp.int32, (VT_ROWS, ATT_BLOCK), 0) == DIFF_V_DIM, 1.0, 0.0)
    vo = DIFF_HEADS * KDIM
    for hh in range(DIFF_HEADS):
        qt_ref[0, hh] = (r2[hh * KDIM:(hh + 1) * KDIM] * Q_SCALE).astype(BF16)
        vv = r2[vo + hh * VT_ROWS: vo + (hh + 1) * VT_ROWS]
        for j in range(nsub):
            vt_ref[0, hh, j] = (vv[:, j * ATT_BLOCK:(j + 1) * ATT_BLOCK] + ones_row).astype(BF16)
    o = vo + DIFF_HEADS * VT_ROWS
    cqt_ref[0] = r2[o:o + SWA_WIDTH].astype(BF16)
    o += SWA_WIDTH
    ones_rows = jnp.where(lax.broadcasted_iota(jnp.int32, (SWA_KV_HEADS * VT_ROWS, tn), 0) % VT_ROWS == HEAD_DIM,
                          1.0, 0.0)
    cvt_ref[0] = (r2[o:o + SWA_KV_HEADS * VT_ROWS] + ones_rows).astype(BF16)

    lp = lamp_ref[...]
    s1 = jnp.sum(lp[0:1] * lp[1:2], axis=-1, keepdims=True)
    s2 = jnp.sum(lp[2:3] * lp[3:4], axis=-1, keepdims=True)
    lam = jnp.exp(s1) - jnp.exp(s2) + lam_init
    lam_ref[...] = jnp.broadcast_to(lam, lam_ref.shape)


def _proj_call(x, g, wstd, wtr, kaug, ktab, lamp, lam_init):
    b, s, d = x.shape
    tn = TOKEN_TILE
    nsub = tn // ATT_BLOCK
    nkb = s // ATT_BLOCK
    grid = (b, s // tn)
    tok = lambda bi, si: (bi, si, 0)
    out_shape = (
        jax.ShapeDtypeStruct((b, DIFF_HEADS, nkb, ATT_BLOCK, KDIM), BF16),
        jax.ShapeDtypeStruct((b, DIFF_HEADS, KDIM, s), BF16),
        jax.ShapeDtypeStruct((b, DIFF_HEADS, nkb, VT_ROWS, ATT_BLOCK), BF16),
        jax.ShapeDtypeStruct((b, s, GMLP_WIDTH), F32),
        jax.ShapeDtypeStruct((b, s, GMLP_WIDTH), F32),
        jax.ShapeDtypeStruct((b, SWA_WIDTH, s), BF16),
        jax.ShapeDtypeStruct((b, s, 128), BF16),
        jax.ShapeDtypeStruct((b, SWA_KV_HEADS * VT_ROWS, s), BF16),
        jax.ShapeDtypeStruct((8, LANES), F32),
    )
    out_specs = (
        pl.BlockSpec((1, DIFF_HEADS, nsub, ATT_BLOCK, KDIM), lambda bi, si: (bi, 0, si, 0, 0)),
        pl.BlockSpec((1, DIFF_HEADS, KDIM, tn), lambda bi, si: (bi, 0, 0, si)),
        pl.BlockSpec((1, DIFF_HEADS, nsub, VT_ROWS, ATT_BLOCK), lambda bi, si: (bi, 0, si, 0, 0)),
        pl.BlockSpec((1, tn, GMLP_WIDTH), tok),
        pl.BlockSpec((1, tn, GMLP_WIDTH), tok),
        pl.BlockSpec((1, SWA_WIDTH, tn), lambda bi, si: (bi, 0, si)),
        pl.BlockSpec((1, tn, 128), tok),
        pl.BlockSpec((1, SWA_KV_HEADS * VT_ROWS, tn), lambda bi, si: (bi, 0, si)),
        pl.BlockSpec((8, LANES), lambda bi, si: (0, 0)),
    )
    in_specs = [
        pl.BlockSpec((1, tn, d), tok),
        _const_spec(g.shape),
        _const_spec(wstd.shape),
        _const_spec(wtr.shape),
        _const_spec(kaug.shape),
        pl.BlockSpec((DIFF_HEADS, nsub, 1, KDIM), lambda bi, si: (0, si, 0, 0)),
        _const_spec(lamp.shape),
    ]
    return pl.pallas_call(
        functools.partial(_proj_kernel, lam_init=lam_init),
        grid=grid, in_specs=in_specs, out_specs=out_specs, out_shape=out_shape,
        compiler_params=_params(("arbitrary", "arbitrary")),
        name="proj",
    )(x, g, wstd, wtr, kaug, ktab, lamp)


def _diff_attn_kernel(slope_ref, qt_ref, k_ref, vt_ref, qaug_ref, diag_ref, lam_ref, gsub_ref,
                      o_ref, w_ref, s0_ref, s1_ref, p0_ref, p1_ref, acc_ref, *, out_scale):
    hh = pl.program_id(1)
    qi = pl.program_id(2)
    blk = ATT_BLOCK
    tq = ATT_QBLOCK
    n_rest = k_ref.shape[2] // 2 - 1
    sigma = slope_ref[hh] * float(blk)

    qt = qt_ref[0, 0].astype(F32)
    qaug = qaug_ref[0]
    row = lax.broadcasted_iota(jnp.int32, (KDIM, tq), 0)
    comp_mask = (row < 64, row >= 64)
    zero = jnp.zeros_like(qt)
    for c in range(2):
        w_ref[0, c] = jnp.where(comp_mask[c], qt - qaug, zero).astype(BF16)
        w_ref[1, c] = jnp.where(comp_mask[c], qt + qaug, zero).astype(BF16)

    def key_rows(kb):
        return jnp.concatenate([k_ref[0, 0, 2 * kb], k_ref[0, 0, 2 * kb + 1]], axis=0)

    def value_cols(kb):
        return jnp.concatenate([vt_ref[0, 0, 2 * kb], vt_ref[0, 0, 2 * kb + 1]], axis=1)

    k_t = key_rows(qi)
    vt = value_cols(qi)
    m = []
    for c in range(2):
        s = _dot(k_t, jnp.where(comp_mask[c], qt, zero).astype(BF16)) + diag_ref[0]
        m_c = jnp.max(s, axis=0, keepdims=True)
        acc_ref[c] = _dot(vt, jnp.exp2((s - m_c).astype(BF16)))
        m.append(m_c)

    dyn_row = lax.broadcasted_iota(jnp.int32, (DYN_ROWS, tq), 0)
    lane_row = lax.broadcasted_iota(jnp.int32, (1, tq), 1)
    q_origin = sigma * jnp.where(lane_row < blk, 2 * qi, 2 * qi + 1).astype(F32)

    def rest_block(j):
        after = (j >= qi).astype(jnp.int32)
        return j + after, after

    def reference_rows(rr, key_sign):
        hi = rr.astype(BF16).astype(F32)
        mid = (rr - hi).astype(BF16).astype(F32)
        lo = rr - hi - mid
        rows = jnp.where(dyn_row == 0, hi, jnp.where(dyn_row == 1, mid, jnp.where(dyn_row == 2, lo, 0.0)))
        rows = jnp.where((dyn_row >= 3) & (dyn_row < 6), key_sign, rows)
        return rows.astype(BF16)

    def column_max(z):
        parts = [z[i * 16:(i + 1) * 16] for i in range(z.shape[0] // 16)]
        while len(parts) > 1:
            parts = [jnp.maximum(parts[i], parts[i + 1]) for i in range(0, len(parts), 2)]
        return jnp.max(parts[0].astype(F32), axis=0, keepdims=True)

    def stage_scores(j, ref_max, s_buf):
        kb, side = rest_block(j)
        sgn = jnp.where(side == 1, 1.0, -1.0)
        k_t = key_rows(kb)
        cms = []
        for c in range(2):
            dyn = reference_rows(sgn * q_origin - ref_max[c], -sgn)
            w = jnp.concatenate([w_ref[side, c, 0:DYN_ROW0, :], dyn], axis=0)
            z = _dot(k_t, w).astype(BF16)
            s_buf[c] = z
            cms.append(column_max(z))
        return cms

    def stage_softmax(m2, m1, s_buf, cms, p_buf):
        alphas, m_out = [], []
        for c in range(2):
            d = jnp.maximum(m1[c] - m2[c], cms[c]).astype(BF16)
            m_new = m2[c] + d.astype(F32)
            p_buf[c] = jnp.exp2(s_buf[c] - d)
            alphas.append(jnp.exp2(m1[c] - m_new))
            m_out.append(m_new)
        return alphas, m_out

    def stage_pv(j, p_buf, alphas):
        kb, _ = rest_block(j)
        vt = value_cols(kb)
        for c in range(2):
            acc_ref[c] = acc_ref[c] * alphas[c] + _dot(vt, p_buf[c])

    s_bufs = (s0_ref, s1_ref)
    p_bufs = (p0_ref, p1_ref)

    def full_step(t, slot, state):
        m_t3, m_t2, cm_prev, al_prev = state
        al_new, m_t1 = stage_softmax(m_t3, m_t2, s_bufs[1 - slot], cm_prev, p_bufs[1 - slot])
        stage_pv(t - 2, p_bufs[slot], al_prev)
        cm_new = stage_scores(t, m_t2, s_bufs[slot])
        return m_t2, m_t1, cm_new, al_new

    cm_a = stage_scores(0, m, s0_ref)
    cm_b = stage_scores(1, m, s1_ref)
    al_a, m_a = stage_softmax(m, m, s0_ref, cm_a, p0_ref)
    state = full_step(2, 0, (m, m_a, cm_b, al_a))

    n_loop = n_rest - 3
    unroll = next(u for u in ATT_UNROLL_CHOICES if n_loop % u == 0)

    def steady(u, state):
        for i in range(unroll):
            state = full_step(3 + unroll * u + i, (1 + i) % 2, state)
        return state

    state = lax.fori_loop(0, n_loop // unroll, steady, state)

    m_t3, m_t2, cm_last, al_prev = state
    al_last, _ = stage_softmax(m_t3, m_t2, s0_ref, cm_last, p0_ref)
    stage_pv(n_rest - 2, p1_ref, al_prev)
    stage_pv(n_rest - 1, p0_ref, al_last)

    a0 = acc_ref[0]
    a1 = acc_ref[1]
    o0 = a0[0:DIFF_V_DIM] / a0[DIFF_V_DIM:DIFF_V_DIM + 1]
    o1 = a1[0:DIFF_V_DIM] / a1[DIFF_V_DIM:DIFF_V_DIM + 1]
    lam = lam_ref[0:1, 0:1]
    o = o0 - lam * o1
    ms = jnp.mean(o * o, axis=0, keepdims=True)
    y = o * lax.rsqrt(ms + EPS) * gsub_ref[...] * out_scale
    o_ref[0] = y.astype(o_ref.dtype)


def _diff_attn_call(slope2, qt, kblk, vtblk, qaug, diag, lam_tile, gsub_b, out_scale):
    b, nh, kdim, s = qt.shape
    nkb = kblk.shape[2]
    blk = ATT_BLOCK
    tq = ATT_QBLOCK
    grid = (b, nh, s // tq)
    in_specs = [
        pl.BlockSpec(memory_space=pltpu.SMEM),
        pl.BlockSpec((1, 1, kdim, tq), lambda bi, hi, qi: (bi, hi, 0, qi)),
        pl.BlockSpec((1, 1, nkb, blk, kdim), lambda bi, hi, qi: (bi, hi, 0, 0, 0)),
        pl.BlockSpec((1, 1, nkb, VT_ROWS, blk), lambda bi, hi, qi: (bi, hi, 0, 0, 0)),
        pl.BlockSpec((1, kdim, tq), lambda bi, hi, qi: (hi, 0, 0)),
        pl.BlockSpec((1, tq, tq), lambda bi, hi, qi: (hi, 0, 0)),
        _const_spec(lam_tile.shape),
        _const_spec(gsub_b.shape),
    ]
    s_buf = pltpu.VMEM((2, tq, tq), BF16)
    p_buf = pltpu.VMEM((2, tq, tq), BF16)
    return pl.pallas_call(
        functools.partial(_diff_attn_kernel, out_scale=out_scale),
        grid=grid, in_specs=in_specs,
        out_specs=pl.BlockSpec((1, DIFF_V_DIM, tq), lambda bi, hi, qi: (bi, hi, qi)),
        out_shape=jax.ShapeDtypeStruct((b, nh * DIFF_V_DIM, s), BF16),
        scratch_shapes=[pltpu.VMEM((2, 2, kdim, tq), BF16),
                        s_buf, s_buf, p_buf, p_buf, pltpu.VMEM((2, VT_ROWS, tq), F32)],
        compiler_params=_params(("arbitrary", "arbitrary", "arbitrary")),
        name="diff_attn",
    )(slope2, qt, kblk, vtblk, qaug, diag, lam_tile, gsub_b)


def _swa_kernel(sink_ref, qt_ref, kp_ref, kc_ref, kn_ref, vp_ref, vc_ref, vn_ref, bias_ref, o_ref, *, seq_len):
    qi = pl.program_id(1)
    blk = SWA_BLOCK
    nsub = SWA_TILE // blk
    k_all = jnp.concatenate([kp_ref[0], kc_ref[0], kn_ref[0]], axis=0)
    vt_all = jnp.concatenate([vp_ref[0], vc_ref[0], vn_ref[0]], axis=1)
    key_row = lax.broadcasted_iota(jnp.int32, (3 * blk, SWA_HEADS * blk), 0)
    no_q = jnp.zeros((HEAD_DIM, SWA_GROUP * blk), BF16)
    sink = sink_ref[...]
    gw = SWA_GROUP * blk
    for sub in range(nsub):
        key_pos = (qi * nsub + sub - 1) * blk + key_row
        in_seq = (key_pos >= 0) & (key_pos < seq_len)
        kk = k_all[sub * blk:(sub + 3) * blk]
        q_t = [qt_ref[0, hq * HEAD_DIM:(hq + 1) * HEAD_DIM, sub * blk:(sub + 1) * blk] for hq in range(SWA_HEADS)]
        w = jnp.concatenate([jnp.concatenate(q_t[:SWA_GROUP] + [no_q], axis=1),
                             jnp.concatenate([no_q] + q_t[SWA_GROUP:], axis=1)], axis=0)
        sc = _dot(kk, w) + bias_ref[...]
        sc = jnp.where(in_seq, sc, NEG_INF)
        m = jnp.maximum(jnp.max(sc, axis=0, keepdims=True), sink)
        e = jnp.exp(sc - m).astype(BF16)
        tail = jnp.exp(sink - m)
        for kh in range(SWA_KV_HEADS):
            vt = vt_all[kh * VT_ROWS:(kh + 1) * VT_ROWS, sub * blk:(sub + 3) * blk]
            acc = _dot(vt, e[:, kh * gw:(kh + 1) * gw])
            o = acc[0:HEAD_DIM] / (acc[HEAD_DIM:HEAD_DIM + 1] + tail[:, kh * gw:(kh + 1) * gw])
            for g in range(SWA_GROUP):
                hq = kh * SWA_GROUP + g
                o_ref[0, hq * HEAD_DIM:(hq + 1) * HEAD_DIM, sub * blk:(sub + 1) * blk] = (
                    o[:, g * blk:(g + 1) * blk].astype(o_ref.dtype))


def _swa_call(sinks, cqt, ck, cvt, bias):
    b, _, s = cqt.shape
    blk = SWA_BLOCK
    tile = SWA_TILE
    nsub = tile // blk
    nb = s // blk
    prev_i = lambda qi: jnp.maximum(qi * nsub - 1, 0)
    next_i = lambda qi: jnp.minimum((qi + 1) * nsub, nb - 1)
    vrows = SWA_KV_HEADS * VT_ROWS
    in_specs = [
        _const_spec(sinks.shape),
        pl.BlockSpec((1, SWA_WIDTH, tile), lambda bi, qi: (bi, 0, qi)),
        pl.BlockSpec((1, blk, 128), lambda bi, qi: (bi, prev_i(qi), 0)),
        pl.BlockSpec((1, tile, 128), lambda bi, qi: (bi, qi, 0)),
        pl.BlockSpec((1, blk, 128), lambda bi, qi: (bi, next_i(qi), 0)),
        pl.BlockSpec((1, vrows, blk), lambda bi, qi: (bi, 0, prev_i(qi))),
        pl.BlockSpec((1, vrows, tile), lambda bi, qi: (bi, 0, qi)),
        pl.BlockSpec((1, vrows, blk), lambda bi, qi: (bi, 0, next_i(qi))),
        _const_spec(bias.shape),
    ]
    return pl.pallas_call(
        functools.partial(_swa_kernel, seq_len=s),
        grid=(b, s // tile), in_specs=in_specs,
        out_specs=pl.BlockSpec((1, SWA_WIDTH, tile), lambda bi, qi: (bi, 0, qi)),
        out_shape=jax.ShapeDtypeStruct((b, SWA_WIDTH, s), BF16),
        compiler_params=_params(("arbitrary", "arbitrary")),
        name="swa",
    )(sinks, cqt, ck, ck, ck, cvt, cvt, cvt, bias)


def _mix_out_kernel(x_ref, yat_ref, bu_ref, bv_ref, yct_ref, lng_ref, lnb_ref, ws_ref, bs_ref,
                    wa_ref, wb_ref, wc_ref, g_ref, o_ref):
    tn = x_ref.shape[1]
    v = bv_ref[0]
    mu = jnp.mean(v, axis=-1, keepdims=True)
    var = jnp.mean(jnp.square(v - mu), axis=-1, keepdims=True)
    vn = ((v - mu) * lax.rsqrt(var + EPS) * lng_ref[...] + lnb_ref[...]).astype(BF16)
    lane_group = lax.broadcasted_iota(jnp.int32, (CHUNK, GMLP_WIDTH), 1) // GMLP_GROUP_DIM
    u = bu_ref[0]
    yb = []
    for c in range(tn // CHUNK):
        vc = vn[c * CHUNK:(c + 1) * CHUNK]
        mixed = bs_ref[...]
        for g in range(GMLP_GROUPS):
            mixed = mixed + jnp.where(lane_group == g, _dot(ws_ref[g], vc), 0.0)
        yb.append(u[c * CHUNK:(c + 1) * CHUNK] * mixed)
    yb = jnp.concatenate(yb, axis=0).astype(BF16)
    y = _dot_tn(yat_ref[0], wa_ref[...]) + _dot(yb, wb_ref[...]) + _dot_tn(yct_ref[0], wc_ref[...])
    o_ref[0] = x_ref[0] + _rms(y, g_ref[...])


def _mix_out_call(x, yat, bu, bv, yc, lng, lnb, ws, bs, wa, wb, wc, g):
    b, s, d = x.shape
    tn = TOKEN_TILE
    tok = lambda bi, si: (bi, si, 0)
    in_specs = [
        pl.BlockSpec((1, tn, d), tok),
        pl.BlockSpec((1, DIFF_WIDTH, tn), lambda bi, si: (bi, 0, si)),
        pl.BlockSpec((1, tn, GMLP_WIDTH), tok),
        pl.BlockSpec((1, tn, GMLP_WIDTH), tok),
        pl.BlockSpec((1, SWA_WIDTH, tn), lambda bi, si: (bi, 0, si)),
    ] + [_const_spec(a.shape) for a in (lng, lnb, ws, bs, wa, wb, wc, g)]
    return pl.pallas_call(
        _mix_out_kernel,
        grid=(b, s // tn), in_specs=in_specs,
        out_specs=pl.BlockSpec((1, tn, d), tok),
        out_shape=jax.ShapeDtypeStruct((b, s, d), F32),
        compiler_params=_params(("arbitrary", "arbitrary")),
        name="mix_out",
    )(x, yat, bu, bv, yc, lng, lnb, ws, bs, wa, wb, wc, g)


def _sigmoid(z):
    return 1.0 / (1.0 + jnp.exp(-z))


def _ffn_ple_kernel(x_ref, p_ref, gpre_ref, wg_ref, wu_ref, wo_ref, gpost_ref,
                    wup_ref, wgate_ref, ggate_ref, gple_ref, o_ref):
    x = x_ref[0]
    h = _rms(x, gpre_ref[...]).astype(BF16)
    f = jnp.zeros(x.shape, F32)
    for j in range(D_FF // FF_CHUNK):
        sl = slice(j * FF_CHUNK, (j + 1) * FF_CHUNK)
        gate = _dot(h, wg_ref[:, sl])
        up = _dot(h, wu_ref[:, sl])
        a = (gate * _sigmoid(gate) * up).astype(BF16)
        f = f + _dot(a, wo_ref[sl, :])
    x = x + _rms(f, gpost_ref[...])
    e = _dot(p_ref[0, 0].astype(BF16), wup_ref[...])
    gt = _sigmoid(_dot(_rms(x, ggate_ref[...]).astype(BF16), wgate_ref[...]))
    o_ref[0] = x + _rms(e * gt, gple_ref[...])


def _ffn_ple_call(x, p, layer, gpre, wg, wu, wo, gpost, wup, wgate, ggate, gple):
    b, s, d = x.shape
    tn = TOKEN_TILE
    tok = lambda bi, si: (bi, si, 0)
    in_specs = [
        pl.BlockSpec((1, tn, d), tok),
        pl.BlockSpec((1, 1, tn, PLE_DIM), lambda bi, si: (layer, bi, si, 0)),
    ] + [_const_spec(a.shape) for a in (gpre, wg, wu, wo, gpost, wup, wgate, ggate, gple)]
    return pl.pallas_call(
        _ffn_ple_kernel,
        grid=(b, s // tn), in_specs=in_specs,
        out_specs=pl.BlockSpec((1, tn, d), tok),
        out_shape=jax.ShapeDtypeStruct((b, s, d), F32),
        compiler_params=_params(("arbitrary", "arbitrary")),
        name="ffn_ple",
    )(x, p, gpre, wg, wu, wo, gpost, wup, wgate, ggate, gple)


def _prep_in_weights(w):
    d = w.shape[0]
    aq = w[:, 0:384].reshape(d, DIFF_HEADS, 2, DIFF_QK_DIM)
    ak = w[:, 384:768].reshape(d, DIFF_HEADS, 2, DIFF_QK_DIM)
    av = w[:, 768:1152].reshape(d, DIFF_HEADS, DIFF_V_DIM)
    pad_qk = ((0, 0), (0, 0), (0, 0), (0, 64 - DIFF_QK_DIM))
    wk = jnp.pad(ak, pad_qk).reshape(d, DIFF_HEADS * KDIM)
    wq = jnp.pad(aq, pad_qk).reshape(d, DIFF_HEADS * KDIM)
    wv = jnp.pad(av, ((0, 0), (0, 0), (0, VT_ROWS - DIFF_V_DIM))).reshape(d, DIFF_HEADS * VT_ROWS)
    cq = w[:, 1664:2048] * (HEAD_DIM ** -0.5)
    cv = w[:, 2176:2304].reshape(d, SWA_KV_HEADS, HEAD_DIM)
    cv = jnp.pad(cv, ((0, 0), (0, 0), (0, VT_ROWS - HEAD_DIM))).reshape(d, SWA_KV_HEADS * VT_ROWS)
    wstd = jnp.concatenate([wk, w[:, 1152:1664], w[:, 2048:2176]], axis=1).astype(BF16)
    wtr = jnp.concatenate([wq, wv, cq, cv], axis=1).T.astype(BF16)
    return wstd, wtr


def kernel(x, p, g_pre_mix, w_in, lam_q1, lam_k1, lam_q2, lam_k2, g_diff_sub, gmlp_ln_g, gmlp_ln_b,
           w_spatial, b_spatial, swa_sinks, w_out, g_post_mix, g_pre_ffn, w_ffn_in, w_ffn_out,
           g_post_ffn, w_ple_up, w_ple_gate, g_ple_gate, g_ple_post):
    b, s, d = x.shape
    depth = w_in.shape[0]
    assert d == D_MODEL and s % (2 * ATT_QBLOCK) == 0 and s >= 4 * ATT_QBLOCK and s % TOKEN_TILE == 0

    slope2_np, qaug_np, kaug_np, ktab_np, diag_np = _diff_bias_constants(TOKEN_TILE, s)
    _, swa_slopes_np = _alibi_slopes_np()
    slope2 = jnp.asarray(slope2_np)
    qaug = jnp.asarray(qaug_np)
    kaug = jnp.asarray(kaug_np)
    ktab = jnp.asarray(ktab_np)
    diag = jnp.asarray(diag_np)
    swa_bias = jnp.asarray(_swa_bias_np(swa_slopes_np))
    row = lambda a: a.reshape(1, -1).astype(F32)

    for l in range(depth):
        lam_init = 0.8 - 0.6 * math.exp(-0.3 * l)
        wstd, wtr = _prep_in_weights(w_in[l])
        lamp = jnp.stack([lam_q1[l], lam_k1[l], lam_q2[l], lam_k2[l]]).astype(F32)
        kblk, qt, vtblk, bu, bv, cq, ck, cv, lam_tile = _proj_call(
            x, row(g_pre_mix[l]), wstd, wtr, kaug, ktab, lamp, lam_init)

        gsub_b = jnp.broadcast_to(g_diff_sub[l].astype(F32)[:, None], (DIFF_V_DIM, ATT_QBLOCK))
        yat = _diff_attn_call(slope2, qt, kblk, vtblk, qaug, diag, lam_tile, gsub_b, 1.0 - lam_init)
        sink_row = jnp.repeat(swa_sinks[l].astype(F32), SWA_BLOCK).reshape(1, SWA_HEADS * SWA_BLOCK)
        yc = _swa_call(sink_row, cq, ck, cv, swa_bias)

        bs = jnp.broadcast_to(b_spatial[l].T[:, :, None], (CHUNK, GMLP_GROUPS, GMLP_GROUP_DIM))
        bs = bs.reshape(CHUNK, GMLP_WIDTH).astype(F32)
        wo = w_out[l].astype(BF16)
        x = _mix_out_call(
            x, yat, bu, bv, yc, row(gmlp_ln_g[l]), row(gmlp_ln_b[l]), w_spatial[l].astype(BF16), bs,
            wo[0:DIFF_WIDTH], wo[DIFF_WIDTH:DIFF_WIDTH + GMLP_WIDTH], wo[DIFF_WIDTH + GMLP_WIDTH:],
            row(g_post_mix[l]))

        wfi = w_ffn_in[l].astype(BF16)
        x = _ffn_ple_call(
            x, p, l, row(g_pre_ffn[l]), wfi[:, :D_FF], wfi[:, D_FF:], w_ffn_out[l].astype(BF16),
            row(g_post_ffn[l]), w_ple_up[l].astype(BF16), w_ple_gate[l].astype(BF16),
            row(g_ple_gate[l]), row(g_ple_post[l]))
    return x
```

```python
import functools
import math

import numpy as np
import jax
import jax.numpy as jnp
from jax import lax
from jax.experimental import pallas as pl
from jax.experimental.pallas import tpu as pltpu

D_MODEL = 1024
HEAD_DIM = 64
DIFF_HEADS = 6
DIFF_QK_DIM = 32
DIFF_V_DIM = 64
DIFF_WIDTH = DIFF_HEADS * DIFF_V_DIM
GMLP_GROUPS = 4
GMLP_GROUP_DIM = 64
GMLP_WIDTH = GMLP_GROUPS * GMLP_GROUP_DIM
CHUNK = 128
SWA_HEADS = 6
SWA_KV_HEADS = 2
SWA_GROUP = SWA_HEADS // SWA_KV_HEADS
SWA_WIDTH = SWA_HEADS * HEAD_DIM
WINDOW = 128
SWA_BLOCK = 128
SWA_TILE = 512
D_FF = 2816
PLE_DIM = 256
N_ATTN_HEADS = DIFF_HEADS + SWA_HEADS
ALIBI_MAX_EXP = 8.0
EPS = 1e-6
NEG_INF = -1e30
LOG2E = 1.4426950408889634

LANES = 128
MXU_DIM_V7X = 256
VMEM_LIMIT_BYTES_V7X = 56 * 1024 * 1024

TOKEN_TILE = 512
ATT_BLOCK = MXU_DIM_V7X
ATT_QBLOCK = 2 * ATT_BLOCK
KDIM = 2 * 64
AUG_OFF = DIFF_QK_DIM
ATT_BUFFERS = 3
ATT_UNROLL_CHOICES = (9, 3)
DYN_ROW0 = 112
DYN_ROWS = KDIM - DYN_ROW0
VT_ROWS = 80
FF_CHUNK = 256

F32 = jnp.float32
BF16 = jnp.bfloat16


def _bf16_round_np(x):
    u = np.asarray(x, np.float32).view(np.uint32).astype(np.uint64)
    r = ((u >> 16) & 1) + 0x7FFF
    return ((u + r) & 0xFFFF0000).astype(np.uint32).view(np.float32)


def _alibi_slopes_np():
    k = np.arange(1, N_ATTN_HEADS + 1, dtype=np.float64)
    s = np.exp2(-ALIBI_MAX_EXP * k / N_ATTN_HEADS).astype(np.float32)
    return s[SWA_HEADS:], s[:SWA_HEADS]


def _split3_bf16(v):
    v = np.asarray(v, np.float32)
    hi = _bf16_round_np(v)
    mid = _bf16_round_np(v - hi)
    lo = _bf16_round_np(v - hi - mid)
    return hi, mid, lo


def _diff_bias_constants(token_tile, seq_len):
    slopes, _ = _alibi_slopes_np()
    slope2 = (slopes.astype(np.float64) * LOG2E).astype(np.float32)
    hi, mid, lo = _split3_bf16(slope2)
    parts = np.stack([hi, mid, lo], axis=1)
    rel = np.arange(ATT_BLOCK, dtype=np.float32)
    qaug = np.zeros((DIFF_HEADS, KDIM, ATT_BLOCK), np.float32)
    kaug = np.zeros((DIFF_HEADS, ATT_BLOCK, KDIM), np.float32)
    for c in range(2):
        base = c * 64 + AUG_OFF
        for t in range(3):
            qaug[:, base + t, :] = rel[None, :]
            qaug[:, base + 3 + t, :] = -parts[:, t][:, None]
            kaug[:, :, base + t] = parts[:, t][:, None]
            kaug[:, :, base + 3 + t] = rel[None, :]
    kaug[:, :, DYN_ROW0:DYN_ROW0 + 3] = 1.0
    kaug = np.tile(kaug, (1, token_tile // ATT_BLOCK, 1))
    qaug = np.tile(qaug, (1, 1, ATT_QBLOCK // ATT_BLOCK))
    n_kb = seq_len // ATT_BLOCK
    sigma_j = (slope2[:, None] * np.float32(ATT_BLOCK)) * np.arange(n_kb, dtype=np.float32)[None, :]
    ktab = np.zeros((DIFF_HEADS, n_kb, 1, KDIM), np.float32)
    for t, piece in enumerate(_split3_bf16(sigma_j)):
        ktab[:, :, 0, DYN_ROW0 + 3 + t] = piece
    pos = np.arange(ATT_QBLOCK, dtype=np.float32)
    dist = np.abs(pos[:, None] - pos[None, :])
    diag = -(slope2[:, None, None] * dist[None])
    return slope2, qaug, kaug, ktab, diag.astype(np.float32)


def _swa_bias_np(slopes):
    key = np.arange(3 * SWA_BLOCK, dtype=np.float32)[:, None]
    qry = np.arange(SWA_BLOCK, dtype=np.float32)[None, :]
    dist = np.abs(key - SWA_BLOCK - qry)
    bias = -(slopes.astype(np.float32)[:, None, None] * dist[None])
    bias = np.where(dist[None] <= WINDOW, bias, np.float32(NEG_INF)).astype(np.float32)
    return np.concatenate(list(bias), axis=1)


def _rms(x, g):
    return x * lax.rsqrt(jnp.mean(x * x, axis=-1, keepdims=True) + EPS) * g


def _dot(a, b):
    return jnp.dot(a, b, preferred_element_type=F32)


def _dot_nt(a, b):
    return lax.dot_general(a, b, (((1,), (1,)), ((), ())), preferred_element_type=F32)


def _dot_tn(a, b):
    return lax.dot_general(a, b, (((0,), (0,)), ((), ())), preferred_element_type=F32)


def _const_spec(shape):
    nd = len(shape)
    return pl.BlockSpec(shape, lambda *_: (0,) * nd, pipeline_mode=pl.Buffered(1))


def _params(sem, flags=None):
    return pltpu.CompilerParams(dimension_semantics=sem, vmem_limit_bytes=VMEM_LIMIT_BYTES_V7X, flags=flags)


N_STD = DIFF_HEADS * KDIM + 2 * GMLP_WIDTH + SWA_KV_HEADS * HEAD_DIM
N_TR = DIFF_HEADS * KDIM + DIFF_HEADS * VT_ROWS + SWA_WIDTH + SWA_KV_HEADS * VT_ROWS
Q_SCALE = (DIFF_QK_DIM ** -0.5) * LOG2E


def _proj_kernel(x_ref, g_ref, wstd_ref, wtr_ref, kaug_ref, ktab_ref, lamp_ref,
                 k_ref, qt_ref, vt_ref, bu_ref, bv_ref, cqt_ref, ck_ref, cvt_ref, lam_ref, *, lam_init):
    tn = x_ref.shape[1]
    nsub = tn // ATT_BLOCK
    h = _rms(x_ref[0], g_ref[...]).astype(BF16)
    r1 = _dot(h, wstd_ref[...])
    for hh in range(DIFF_HEADS):
        kk = r1[:, hh * KDIM:(hh + 1) * KDIM] + kaug_ref[hh]
        for j in range(nsub):
            k_ref[0, hh, j] = (kk[j * ATT_BLOCK:(j + 1) * ATT_BLOCK] + ktab_ref[hh, j]).astype(BF16)
    o = DIFF_HEADS * KDIM
    bu_ref[0] = r1[:, o:o + GMLP_WIDTH]
    o += GMLP_WIDTH
    bv_ref[0] = r1[:, o:o + GMLP_WIDTH]
    o += GMLP_WIDTH
    ck_ref[0] = r1[:, o:o + 128].astype(BF16)

    r2 = _dot_nt(wtr_ref[...], h)
    ones_row = jnp.where(lax.broadcasted_iota(jnp.int32, (VT_ROWS, ATT_BLOCK), 0) == DIFF_V_DIM, 1.0, 0.0)
    vo = DIFF_HEADS * KDIM
    for hh in range(DIFF_HEADS):
        qt_ref[0, hh] = (r2[hh * KDIM:(hh + 1) * KDIM] * Q_SCALE).astype(BF16)
        vv = r2[vo + hh * VT_ROWS: vo + (hh + 1) * VT_ROWS]
        for j in range(nsub):
            vt_ref[0, hh, j] = (vv[:, j * ATT_BLOCK:(j + 1) * ATT_BLOCK] + ones_row).astype(BF16)
    o = vo + DIFF_HEADS * VT_ROWS
    cqt_ref[0] = r2[o:o + SWA_WIDTH].astype(BF16)
    o += SWA_WIDTH
    ones_rows = jnp.where(lax.broadcasted_iota(jnp.int32, (SWA_KV_HEADS * VT_ROWS, tn), 0) % VT_ROWS == HEAD_DIM,
                          1.0, 0.0)
    cvt_ref[0] = (r2[o:o + SWA_KV_HEADS * VT_ROWS] + ones_rows).astype(BF16)

    lp = lamp_ref[...]
    s1 = jnp.sum(lp[0:1] * lp[1:2], axis=-1, keepdims=True)
    s2 = jnp.sum(lp[2:3] * lp[3:4], axis=-1, keepdims=True)
    lam = jnp.exp(s1) - jnp.exp(s2) + lam_init
    lam_ref[...] = jnp.broadcast_to(lam, lam_ref.shape)


def _proj_call(x, g, wstd, wtr, kaug, ktab, lamp, lam_init):
    b, s, d = x.shape
    tn = TOKEN_TILE
    nsub = tn // ATT_BLOCK
    nkb = s // ATT_BLOCK
    grid = (b, s // tn)
    tok = lambda bi, si: (bi, si, 0)
    out_shape = (
        jax.ShapeDtypeStruct((b, DIFF_HEADS, nkb, ATT_BLOCK, KDIM), BF16),
        jax.ShapeDtypeStruct((b, DIFF_HEADS, KDIM, s), BF16),
        jax.ShapeDtypeStruct((b, DIFF_HEADS, nkb, VT_ROWS, ATT_BLOCK), BF16),
        jax.ShapeDtypeStruct((b, s, GMLP_WIDTH), F32),
        jax.ShapeDtypeStruct((b, s, GMLP_WIDTH), F32),
        jax.ShapeDtypeStruct((b, SWA_WIDTH, s), BF16),
        jax.ShapeDtypeStruct((b, s, 128), BF16),
        jax.ShapeDtypeStruct((b, SWA_KV_HEADS * VT_ROWS, s), BF16),
        jax.ShapeDtypeStruct((8, LANES), F32),
    )
    out_specs = (
        pl.BlockSpec((1, DIFF_HEADS, nsub, ATT_BLOCK, KDIM), lambda bi, si: (bi, 0, si, 0, 0)),
        pl.BlockSpec((1, DIFF_HEADS, KDIM, tn), lambda bi, si: (bi, 0, 0, si)),
        pl.BlockSpec((1, DIFF_HEADS, nsub, VT_ROWS, ATT_BLOCK), lambda bi, si: (bi, 0, si, 0, 0)),
        pl.BlockSpec((1, tn, GMLP_WIDTH), tok),
        pl.BlockSpec((1, tn, GMLP_WIDTH), tok),
        pl.BlockSpec((1, SWA_WIDTH, tn), lambda bi, si: (bi, 0, si)),
        pl.BlockSpec((1, tn, 128), tok),
        pl.BlockSpec((1, SWA_KV_HEADS * VT_ROWS, tn), lambda bi, si: (bi, 0, si)),
        pl.BlockSpec((8, LANES), lambda bi, si: (0, 0)),
    )
    in_specs = [
        pl.BlockSpec((1, tn, d), tok),
        _const_spec(g.shape),
        _const_spec(wstd.shape),
        _const_spec(wtr.shape),
        _const_spec(kaug.shape),
        pl.BlockSpec((DIFF_HEADS, nsub, 1, KDIM), lambda bi, si: (0, si, 0, 0)),
        _const_spec(lamp.shape),
    ]
    return pl.pallas_call(
        functools.partial(_proj_kernel, lam_init=lam_init),
        grid=grid, in_specs=in_specs, out_specs=out_specs, out_shape=out_shape,
        compiler_params=_params(("arbitrary", "arbitrary")),
        name="proj",
    )(x, g, wstd, wtr, kaug, ktab, lamp)


def _diff_attn_kernel(slope_ref, qt_ref, k_ref, vt_ref, qaug_ref, diag_ref, lam_ref, gsub_ref,
                      o_ref, w_ref, *scratch, out_scale):
    s_bufs = scratch[0:ATT_BUFFERS]
    p_bufs = scratch[ATT_BUFFERS:2 * ATT_BUFFERS]
    acc_ref = scratch[2 * ATT_BUFFERS]
    hh = pl.program_id(1)
    qi = pl.program_id(2)
    blk = ATT_BLOCK
    tq = ATT_QBLOCK
    n_rest = k_ref.shape[2] // 2 - 1
    sigma = slope_ref[hh] * float(blk)

    qt = qt_ref[0, 0].astype(F32)
    qaug = qaug_ref[0]
    row = lax.broadcasted_iota(jnp.int32, (KDIM, tq), 0)
    comp_mask = (row < 64, row >= 64)
    zero = jnp.zeros_like(qt)
    for c in range(2):
        w_ref[0, c] = jnp.where(comp_mask[c], qt - qaug, zero).astype(BF16)
        w_ref[1, c] = jnp.where(comp_mask[c], qt + qaug, zero).astype(BF16)

    def key_rows(kb):
        return jnp.concatenate([k_ref[0, 0, 2 * kb], k_ref[0, 0, 2 * kb + 1]], axis=0)

    def value_cols(kb):
        return jnp.concatenate([vt_ref[0, 0, 2 * kb], vt_ref[0, 0, 2 * kb + 1]], axis=1)

    k_t = key_rows(qi)
    vt = value_cols(qi)
    m = []
    for c in range(2):
        s = _dot(k_t, jnp.where(comp_mask[c], qt, zero).astype(BF16)) + diag_ref[0]
        m_c = jnp.max(s, axis=0, keepdims=True)
        acc_ref[c] = _dot(vt, jnp.exp2((s - m_c).astype(BF16)))
        m.append(m_c)

    dyn_row = lax.broadcasted_iota(jnp.int32, (DYN_ROWS, tq), 0)
    lane_row = lax.broadcasted_iota(jnp.int32, (1, tq), 1)
    q_origin = sigma * jnp.where(lane_row < blk, 2 * qi, 2 * qi + 1).astype(F32)

    def rest_block(j):
        after = (j >= qi).astype(jnp.int32)
        return j + after, after

    def reference_rows(rr, key_sign):
        hi = rr.astype(BF16).astype(F32)
        mid = (rr - hi).astype(BF16).astype(F32)
        lo = rr - hi - mid
        rows = jnp.where(dyn_row == 0, hi, jnp.where(dyn_row == 1, mid, jnp.where(dyn_row == 2, lo, 0.0)))
        rows = jnp.where((dyn_row >= 3) & (dyn_row < 6), key_sign, rows)
        return rows.astype(BF16)

    def column_max(z):
        parts = [z[i * 16:(i + 1) * 16] for i in range(z.shape[0] // 16)]
        while len(parts) > 1:
            parts = [jnp.maximum(parts[i], parts[i + 1]) for i in range(0, len(parts), 2)]
        return jnp.max(parts[0].astype(F32), axis=0, keepdims=True)

    def stage_scores(j, ref_max, s_buf):
        kb, side = rest_block(j)
        sgn = jnp.where(side == 1, 1.0, -1.0)
        k_t = key_rows(kb)
        for c in range(2):
            dyn = reference_rows(sgn * q_origin - ref_max[c], -sgn)
            w = jnp.concatenate([w_ref[side, c, 0:DYN_ROW0, :], dyn], axis=0)
            s_buf[c] = _dot(k_t, w).astype(BF16)

    def stage_softmax(m2, m1, s_buf, p_buf):
        alphas, m_out = [], []
        for c in range(2):
            z = s_buf[c]
            d = jnp.maximum(m1[c] - m2[c], column_max(z)).astype(BF16)
            m_new = m2[c] + d.astype(F32)
            p_buf[c] = jnp.exp2(z - d)
            alphas.append(jnp.exp2(m1[c] - m_new))
            m_out.append(m_new)
        return alphas, m_out

    def stage_pv(j, p_buf, alphas):
        kb, _ = rest_block(j)
        vt = value_cols(kb)
        for c in range(2):
            acc_ref[c] = acc_ref[c] * alphas[c] + _dot(vt, p_buf[c])

    nbuf = ATT_BUFFERS

    def full_step(t, t_mod, state):
        m_t3, m_t2, al_prev = state
        stage_scores(t, m_t2, s_bufs[t_mod])
        stage_pv(t - 2, p_bufs[(t_mod - 2) % nbuf], al_prev)
        al_new, m_t1 = stage_softmax(m_t3, m_t2, s_bufs[(t_mod - 1) % nbuf], p_bufs[(t_mod - 1) % nbuf])
        return m_t2, m_t1, al_new

    stage_scores(0, m, s_bufs[0])
    stage_scores(1, m, s_bufs[1])
    al_a, m_a = stage_softmax(m, m, s_bufs[0], p_bufs[0])
    state = (m, m_a, al_a)

    n_full = n_rest - 2
    unroll = next((u for u in ATT_UNROLL_CHOICES if u <= n_full), 0)
    n_peel = n_full % unroll if unroll else n_full
    for t in range(2, 2 + n_peel):
        state = full_step(t, t % nbuf, state)
    t0 = 2 + n_peel

    def steady(u, state):
        for i in range(unroll):
            state = full_step(t0 + unroll * u + i, (t0 + i) % nbuf, state)
        return state

    if unroll:
        state = lax.fori_loop(0, (n_full - n_peel) // unroll, steady, state)

    m_t3, m_t2, al_prev = state
    last = (n_rest - 1) % nbuf
    al_last, _ = stage_softmax(m_t3, m_t2, s_bufs[last], p_bufs[last])
    stage_pv(n_rest - 2, p_bufs[(n_rest - 2) % nbuf], al_prev)
    stage_pv(n_rest - 1, p_bufs[last], al_last)

    a0 = acc_ref[0]
    a1 = acc_ref[1]
    o0 = a0[0:DIFF_V_DIM] / a0[DIFF_V_DIM:DIFF_V_DIM + 1]
    o1 = a1[0:DIFF_V_DIM] / a1[DIFF_V_DIM:DIFF_V_DIM + 1]
    lam = lam_ref[0:1, 0:1]
    o = o0 - lam * o1
    ms = jnp.mean(o * o, axis=0, keepdims=True)
    y = o * lax.rsqrt(ms + EPS) * gsub_ref[...] * out_scale
    o_ref[0] = y.astype(o_ref.dtype)


def _diff_attn_call(slope2, qt, kblk, vtblk, qaug, diag, lam_tile, gsub_b, out_scale):
    b, nh, kdim, s = qt.shape
    nkb = kblk.shape[2]
    blk = ATT_BLOCK
    tq = ATT_QBLOCK
    grid = (b, nh, s // tq)
    in_specs = [
        pl.BlockSpec(memory_space=pltpu.SMEM),
        pl.BlockSpec((1, 1, kdim, tq), lambda bi, hi, qi: (bi, hi, 0, qi)),
        pl.BlockSpec((1, 1, nkb, blk, kdim), lambda bi, hi, qi: (bi, hi, 0, 0, 0)),
        pl.BlockSpec((1, 1, nkb, VT_ROWS, blk), lambda bi, hi, qi: (bi, hi, 0, 0, 0)),
        pl.BlockSpec((1, kdim, tq), lambda bi, hi, qi: (hi, 0, 0)),
        pl.BlockSpec((1, tq, tq), lambda bi, hi, qi: (hi, 0, 0)),
        _const_spec(lam_tile.shape),
        _const_spec(gsub_b.shape),
    ]
    s_buf = pltpu.VMEM((2, tq, tq), BF16)
    p_buf = pltpu.VMEM((2, tq, tq), BF16)
    return pl.pallas_call(
        functools.partial(_diff_attn_kernel, out_scale=out_scale),
        grid=grid, in_specs=in_specs,
        out_specs=pl.BlockSpec((1, DIFF_V_DIM, tq), lambda bi, hi, qi: (bi, hi, qi)),
        out_shape=jax.ShapeDtypeStruct((b, nh * DIFF_V_DIM, s), BF16),
        scratch_shapes=[pltpu.VMEM((2, 2, kdim, tq), BF16),
                        *([s_buf] * ATT_BUFFERS), *([p_buf] * ATT_BUFFERS),
                        pltpu.VMEM((2, VT_ROWS, tq), F32)],
        compiler_params=_params(("arbitrary", "arbitrary", "arbitrary")),
        name="diff_attn",
    )(slope2, qt, kblk, vtblk, qaug, diag, lam_tile, gsub_b)


def _swa_kernel(sink_ref, qt_ref, kp_ref, kc_ref, kn_ref, vp_ref, vc_ref, vn_ref, bias_ref, o_ref, *, seq_len):
    qi = pl.program_id(1)
    blk = SWA_BLOCK
    nsub = SWA_TILE // blk
    k_all = jnp.concatenate([kp_ref[0], kc_ref[0], kn_ref[0]], axis=0)
    vt_all = jnp.concatenate([vp_ref[0], vc_ref[0], vn_ref[0]], axis=1)
    key_row = lax.broadcasted_iota(jnp.int32, (3 * blk, SWA_HEADS * blk), 0)
    no_q = jnp.zeros((HEAD_DIM, SWA_GROUP * blk), BF16)
    sink = sink_ref[...]
    gw = SWA_GROUP * blk
    for sub in range(nsub):
        key_pos = (qi * nsub + sub - 1) * blk + key_row
        in_seq = (key_pos >= 0) & (key_pos < seq_len)
        kk = k_all[sub * blk:(sub + 3) * blk]
        q_t = [qt_ref[0, hq * HEAD_DIM:(hq + 1) * HEAD_DIM, sub * blk:(sub + 1) * blk] for hq in range(SWA_HEADS)]
        w = jnp.concatenate([jnp.concatenate(q_t[:SWA_GROUP] + [no_q], axis=1),
                             jnp.concatenate([no_q] + q_t[SWA_GROUP:], axis=1)], axis=0)
        sc = _dot(kk, w) + bias_ref[...]
        sc = jnp.where(in_seq, sc, NEG_INF)
        m = jnp.maximum(jnp.max(sc, axis=0, keepdims=True), sink)
        e = jnp.exp(sc - m).astype(BF16)
        tail = jnp.exp(sink - m)
        for kh in range(SWA_KV_HEADS):
            vt = vt_all[kh * VT_ROWS:(kh + 1) * VT_ROWS, sub * blk:(sub + 3) * blk]
            acc = _dot(vt, e[:, kh * gw:(kh + 1) * gw])
            o = acc[0:HEAD_DIM] / (acc[HEAD_DIM:HEAD_DIM + 1] + tail[:, kh * gw:(kh + 1) * gw])
            for g in range(SWA_GROUP):
                hq = kh * SWA_GROUP + g
                o_ref[0, hq * HEAD_DIM:(hq + 1) * HEAD_DIM, sub * blk:(sub + 1) * blk] = (
                    o[:, g * blk:(g + 1) * blk].astype(o_ref.dtype))


def _swa_call(sinks, cqt, ck, cvt, bias):
    b, _, s = cqt.shape
    blk = SWA_BLOCK
    tile = SWA_TILE
    nsub = tile // blk
    nb = s // blk
    prev_i = lambda qi: jnp.maximum(qi * nsub - 1, 0)
    next_i = lambda qi: jnp.minimum((qi + 1) * nsub, nb - 1)
    vrows = SWA_KV_HEADS * VT_ROWS
    in_specs = [
        _const_spec(sinks.shape),
        pl.BlockSpec((1, SWA_WIDTH, tile), lambda bi, qi: (bi, 0, qi)),
        pl.BlockSpec((1, blk, 128), lambda bi, qi: (bi, prev_i(qi), 0)),
        pl.BlockSpec((1, tile, 128), lambda bi, qi: (bi, qi, 0)),
        pl.BlockSpec((1, blk, 128), lambda bi, qi: (bi, next_i(qi), 0)),
        pl.BlockSpec((1, vrows, blk), lambda bi, qi: (bi, 0, prev_i(qi))),
        pl.BlockSpec((1, vrows, tile), lambda bi, qi: (bi, 0, qi)),
        pl.BlockSpec((1, vrows, blk), lambda bi, qi: (bi, 0, next_i(qi))),
        _const_spec(bias.shape),
    ]
    return pl.pallas_call(
        functools.partial(_swa_kernel, seq_len=s),
        grid=(b, s // tile), in_specs=in_specs,
        out_specs=pl.BlockSpec((1, SWA_WIDTH, tile), lambda bi, qi: (bi, 0, qi)),
        out_shape=jax.ShapeDtypeStruct((b, SWA_WIDTH, s), BF16),
        compiler_params=_params(("arbitrary", "arbitrary")),
        name="swa",
    )(sinks, cqt, ck, ck, ck, cvt, cvt, cvt, bias)


def _mix_out_kernel(x_ref, yat_ref, bu_ref, bv_ref, yct_ref, lng_ref, lnb_ref, ws_ref, bs_ref,
                    wa_ref, wb_ref, wc_ref, g_ref, o_ref):
    tn = x_ref.shape[1]
    v = bv_ref[0]
    mu = jnp.mean(v, axis=-1, keepdims=True)
    var = jnp.mean(jnp.square(v - mu), axis=-1, keepdims=True)
    vn = ((v - mu) * lax.rsqrt(var + EPS) * lng_ref[...] + lnb_ref[...]).astype(BF16)
    lane_group = lax.broadcasted_iota(jnp.int32, (CHUNK, GMLP_WIDTH), 1) // GMLP_GROUP_DIM
    u = bu_ref[0]
    yb = []
    for c in range(tn // CHUNK):
        vc = vn[c * CHUNK:(c + 1) * CHUNK]
        mixed = bs_ref[...]
        for g in range(GMLP_GROUPS):
            mixed = mixed + jnp.where(lane_group == g, _dot(ws_ref[g], vc), 0.0)
        yb.append(u[c * CHUNK:(c + 1) * CHUNK] * mixed)
    yb = jnp.concatenate(yb, axis=0).astype(BF16)
    y = _dot_tn(yat_ref[0], wa_ref[...]) + _dot(yb, wb_ref[...]) + _dot_tn(yct_ref[0], wc_ref[...])
    o_ref[0] = x_ref[0] + _rms(y, g_ref[...])


def _mix_out_call(x, yat, bu, bv, yc, lng, lnb, ws, bs, wa, wb, wc, g):
    b, s, d = x.shape
    tn = TOKEN_TILE
    tok = lambda bi, si: (bi, si, 0)
    in_specs = [
        pl.BlockSpec((1, tn, d), tok),
        pl.BlockSpec((1, DIFF_WIDTH, tn), lambda bi, si: (bi, 0, si)),
        pl.BlockSpec((1, tn, GMLP_WIDTH), tok),
        pl.BlockSpec((1, tn, GMLP_WIDTH), tok),
        pl.BlockSpec((1, SWA_WIDTH, tn), lambda bi, si: (bi, 0, si)),
    ] + [_const_spec(a.shape) for a in (lng, lnb, ws, bs, wa, wb, wc, g)]
    return pl.pallas_call(
        _mix_out_kernel,
        grid=(b, s // tn), in_specs=in_specs,
        out_specs=pl.BlockSpec((1, tn, d), tok),
        out_shape=jax.ShapeDtypeStruct((b, s, d), F32),
        compiler_params=_params(("arbitrary", "arbitrary")),
        name="mix_out",
    )(x, yat, bu, bv, yc, lng, lnb, ws, bs, wa, wb, wc, g)


def _sigmoid(z):
    return 1.0 / (1.0 + jnp.exp(-z))


def _ffn_ple_kernel(x_ref, p_ref, gpre_ref, wg_ref, wu_ref, wo_ref, gpost_ref,
                    wup_ref, wgate_ref, ggate_ref, gple_ref, o_ref):
    x = x_ref[0]
    h = _rms(x, gpre_ref[...]).astype(BF16)
    f = jnp.zeros(x.shape, F32)
    for j in range(D_FF // FF_CHUNK):
        sl = slice(j * FF_CHUNK, (j + 1) * FF_CHUNK)
        gate = _dot(h, wg_ref[:, sl])
        up = _dot(h, wu_ref[:, sl])
        a = (gate * _sigmoid(gate) * up).astype(BF16)
        f = f + _dot(a, wo_ref[sl, :])
    x = x + _rms(f, gpost_ref[...])
    e = _dot(p_ref[0, 0].astype(BF16), wup_ref[...])
    gt = _sigmoid(_dot(_rms(x, ggate_ref[...]).astype(BF16), wgate_ref[...]))
    o_ref[0] = x + _rms(e * gt, gple_ref[...])


def _ffn_ple_call(x, p, layer, gpre, wg, wu, wo, gpost, wup, wgate, ggate, gple):
    b, s, d = x.shape
    tn = TOKEN_TILE
    tok = lambda bi, si: (bi, si, 0)
    in_specs = [
        pl.BlockSpec((1, tn, d), tok),
        pl.BlockSpec((1, 1, tn, PLE_DIM), lambda bi, si: (layer, bi, si, 0)),
    ] + [_const_spec(a.shape) for a in (gpre, wg, wu, wo, gpost, wup, wgate, ggate, gple)]
    return pl.pallas_call(
        _ffn_ple_kernel,
        grid=(b, s // tn), in_specs=in_specs,
        out_specs=pl.BlockSpec((1, tn, d), tok),
        out_shape=jax.ShapeDtypeStruct((b, s, d), F32),
        compiler_params=_params(("arbitrary", "arbitrary")),
        name="ffn_ple",
    )(x, p, gpre, wg, wu, wo, gpost, wup, wgate, ggate, gple)


def _prep_in_weights(w):
    d = w.shape[0]
    aq = w[:, 0:384].reshape(d, DIFF_HEADS, 2, DIFF_QK_DIM)
    ak = w[:, 384:768].reshape(d, DIFF_HEADS, 2, DIFF_QK_DIM)
    av = w[:, 768:1152].reshape(d, DIFF_HEADS, DIFF_V_DIM)
    pad_qk = ((0, 0), (0, 0), (0, 0), (0, 64 - DIFF_QK_DIM))
    wk = jnp.pad(ak, pad_qk).reshape(d, DIFF_HEADS * KDIM)
    wq = jnp.pad(aq, pad_qk).reshape(d, DIFF_HEADS * KDIM)
    wv = jnp.pad(av, ((0, 0), (0, 0), (0, VT_ROWS - DIFF_V_DIM))).reshape(d, DIFF_HEADS * VT_ROWS)
    cq = w[:, 1664:2048] * (HEAD_DIM ** -0.5)
    cv = w[:, 2176:2304].reshape(d, SWA_KV_HEADS, HEAD_DIM)
    cv = jnp.pad(cv, ((0, 0), (0, 0), (0, VT_ROWS - HEAD_DIM))).reshape(d, SWA_KV_HEADS * VT_ROWS)
    wstd = jnp.concatenate([wk, w[:, 1152:1664], w[:, 2048:2176]], axis=1).astype(BF16)
    wtr = jnp.concatenate([wq, wv, cq, cv], axis=1).T.astype(BF16)
    return wstd, wtr


def kernel(x, p, g_pre_mix, w_in, lam_q1, lam_k1, lam_q2, lam_k2, g_diff_sub, gmlp_ln_g, gmlp_ln_b,
           w_spatial, b_spatial, swa_sinks, w_out, g_post_mix, g_pre_ffn, w_ffn_in, w_ffn_out,
           g_post_ffn, w_ple_up, w_ple_gate, g_ple_gate, g_ple_post):
    b, s, d = x.shape
    depth = w_in.shape[0]
    assert d == D_MODEL and s % (2 * ATT_QBLOCK) == 0 and s >= 4 * ATT_QBLOCK and s % TOKEN_TILE == 0

    slope2_np, qaug_np, kaug_np, ktab_np, diag_np = _diff_bias_constants(TOKEN_TILE, s)
    _, swa_slopes_np = _alibi_slopes_np()
    slope2 = jnp.asarray(slope2_np)
    qaug = jnp.asarray(qaug_np)
    kaug = jnp.asarray(kaug_np)
    ktab = jnp.asarray(ktab_np)
    diag = jnp.asarray(diag_np)
    swa_bias = jnp.asarray(_swa_bias_np(swa_slopes_np))
    row = lambda a: a.reshape(1, -1).astype(F32)

    for l in range(depth):
        lam_init = 0.8 - 0.6 * math.exp(-0.3 * l)
        wstd, wtr = _prep_in_weights(w_in[l])
        lamp = jnp.stack([lam_q1[l], lam_k1[l], lam_q2[l], lam_k2[l]]).astype(F32)
        kblk, qt, vtblk, bu, bv, cq, ck, cv, lam_tile = _proj_call(
            x, row(g_pre_mix[l]), wstd, wtr, kaug, ktab, lamp, lam_init)

        gsub_b = jnp.broadcast_to(g_diff_sub[l].astype(F32)[:, None], (DIFF_V_DIM, ATT_QBLOCK))
        yat = _diff_attn_call(slope2, qt, kblk, vtblk, qaug, diag, lam_tile, gsub_b, 1.0 - lam_init)
        sink_row = jnp.repeat(swa_sinks[l].astype(F32), SWA_BLOCK).reshape(1, SWA_HEADS * SWA_BLOCK)
        yc = _swa_call(sink_row, cq, ck, cv, swa_bias)

        bs = jnp.broadcast_to(b_spatial[l].T[:, :, None], (CHUNK, GMLP_GROUPS, GMLP_GROUP_DIM))
        bs = bs.reshape(CHUNK, GMLP_WIDTH).astype(F32)
        wo = w_out[l].astype(BF16)
        x = _mix_out_call(
            x, yat, bu, bv, yc, row(gmlp_ln_g[l]), row(gmlp_ln_b[l]), w_spatial[l].astype(BF16), bs,
            wo[0:DIFF_WIDTH], wo[DIFF_WIDTH:DIFF_WIDTH + GMLP_WIDTH], wo[DIFF_WIDTH + GMLP_WIDTH:],
            row(g_post_mix[l]))

        wfi = w_ffn_in[l].astype(BF16)
        x = _ffn_ple_call(
            x, p, l, row(g_pre_ffn[l]), wfi[:, :D_FF], wfi[:, D_FF:], w_ffn_out[l].astype(BF16),
            row(g_post_ffn[l]), w_ple_up[l].astype(BF16), w_ple_gate[l].astype(BF16),
            row(g_ple_gate[l]), row(g_ple_post[l]))
    return x
```

```python
import functools
import math

import numpy as np
import jax
import jax.numpy as jnp
from jax import lax
from jax.experimental import pallas as pl
from jax.experimental.pallas import tpu as pltpu

D_MODEL = 1024
HEAD_DIM = 64
DIFF_HEADS = 6
DIFF_QK_DIM = 32
DIFF_V_DIM = 64
DIFF_WIDTH = DIFF_HEADS * DIFF_V_DIM
GMLP_GROUPS = 4
GMLP_GROUP_DIM = 64
GMLP_WIDTH = GMLP_GROUPS * GMLP_GROUP_DIM
CHUNK = 128
SWA_HEADS = 6
SWA_KV_HEADS = 2
SWA_GROUP = SWA_HEADS // SWA_KV_HEADS
SWA_WIDTH = SWA_HEADS * HEAD_DIM
WINDOW = 128
SWA_BLOCK = 128
SWA_TILE = 512
D_FF = 2816
PLE_DIM = 256
N_ATTN_HEADS = DIFF_HEADS + SWA_HEADS
ALIBI_MAX_EXP = 8.0
EPS = 1e-6
NEG_INF = -1e30
LOG2E = 1.4426950408889634

LANES = 128
MXU_DIM_V7X = 256
VMEM_LIMIT_BYTES_V7X = 56 * 1024 * 1024

TOKEN_TILE = 512
ATT_BLOCK = MXU_DIM_V7X
ATT_QBLOCK = 2 * ATT_BLOCK
KDIM = 2 * 64
AUG_OFF = DIFF_QK_DIM
ATT_BUFFERS = 3
ATT_UNROLL_CHOICES = (9, 3)
DYN_ROW0 = 112
DYN_ROWS = KDIM - DYN_ROW0
VT_ROWS = 80
FF_CHUNK = 256

F32 = jnp.float32
BF16 = jnp.bfloat16


def _bf16_round_np(x):
    u = np.asarray(x, np.float32).view(np.uint32).astype(np.uint64)
    r = ((u >> 16) & 1) + 0x7FFF
    return ((u + r) & 0xFFFF0000).astype(np.uint32).view(np.float32)


def _alibi_slopes_np():
    k = np.arange(1, N_ATTN_HEADS + 1, dtype=np.float64)
    s = np.exp2(-ALIBI_MAX_EXP * k / N_ATTN_HEADS).astype(np.float32)
    return s[SWA_HEADS:], s[:SWA_HEADS]


def _split3_bf16(v):
    v = np.asarray(v, np.float32)
    hi = _bf16_round_np(v)
    mid = _bf16_round_np(v - hi)
    lo = _bf16_round_np(v - hi - mid)
    return hi, mid, lo


def _diff_bias_constants(token_tile, seq_len):
    slopes, _ = _alibi_slopes_np()
    slope2 = (slopes.astype(np.float64) * LOG2E).astype(np.float32)
    hi, mid, lo = _split3_bf16(slope2)
    parts = np.stack([hi, mid, lo], axis=1)
    rel = np.arange(ATT_BLOCK, dtype=np.float32)
    qaug = np.zeros((DIFF_HEADS, KDIM, ATT_BLOCK), np.float32)
    kaug = np.zeros((DIFF_HEADS, ATT_BLOCK, KDIM), np.float32)
    for c in range(2):
        base = c * 64 + AUG_OFF
        for t in range(3):
            qaug[:, base + t, :] = rel[None, :]
            qaug[:, base + 3 + t, :] = -parts[:, t][:, None]
            kaug[:, :, base + t] = parts[:, t][:, None]
            kaug[:, :, base + 3 + t] = rel[None, :]
    kaug[:, :, DYN_ROW0:DYN_ROW0 + 3] = 1.0
    kaug = np.tile(kaug, (1, token_tile // ATT_BLOCK, 1))
    qaug = np.tile(qaug, (1, 1, ATT_QBLOCK // ATT_BLOCK))
    n_kb = seq_len // ATT_BLOCK
    sigma_j = (slope2[:, None] * np.float32(ATT_BLOCK)) * np.arange(n_kb, dtype=np.float32)[None, :]
    ktab = np.zeros((DIFF_HEADS, n_kb, 1, KDIM), np.float32)
    for t, piece in enumerate(_split3_bf16(sigma_j)):
        ktab[:, :, 0, DYN_ROW0 + 3 + t] = piece
    pos = np.arange(ATT_QBLOCK, dtype=np.float32)
    dist = np.abs(pos[:, None] - pos[None, :])
    diag = -(slope2[:, None, None] * dist[None])
    return slope2, qaug, kaug, ktab, diag.astype(np.float32)


def _swa_bias_np(slopes):
    key = np.arange(3 * SWA_BLOCK, dtype=np.float32)[:, None]
    qry = np.arange(SWA_BLOCK, dtype=np.float32)[None, :]
    dist = np.abs(key - SWA_BLOCK - qry)
    bias = -(slopes.astype(np.float32)[:, None, None] * dist[None])
    bias = np.where(dist[None] <= WINDOW, bias, np.float32(NEG_INF)).astype(np.float32)
    return np.concatenate(list(bias), axis=1)


def _rms(x, g):
    return x * lax.rsqrt(jnp.mean(x * x, axis=-1, keepdims=True) + EPS) * g


def _dot(a, b):
    return jnp.dot(a, b, preferred_element_type=F32)


def _dot_nt(a, b):
    return lax.dot_general(a, b, (((1,), (1,)), ((), ())), preferred_element_type=F32)


def _dot_tn(a, b):
    return lax.dot_general(a, b, (((0,), (0,)), ((), ())), preferred_element_type=F32)


def _const_spec(shape):
    nd = len(shape)
    return pl.BlockSpec(shape, lambda *_: (0,) * nd, pipeline_mode=pl.Buffered(1))


def _params(sem, flags=None):
    return pltpu.CompilerParams(dimension_semantics=sem, vmem_limit_bytes=VMEM_LIMIT_BYTES_V7X, flags=flags)


N_STD = DIFF_HEADS * KDIM + 2 * GMLP_WIDTH + SWA_KV_HEADS * HEAD_DIM
N_TR = DIFF_HEADS * KDIM + DIFF_HEADS * VT_ROWS + SWA_WIDTH + SWA_KV_HEADS * VT_ROWS
Q_SCALE = (DIFF_QK_DIM ** -0.5) * LOG2E


def _proj_kernel(x_ref, g_ref, wstd_ref, wtr_ref, kaug_ref, ktab_ref, lamp_ref,
                 k_ref, qt_ref, vt_ref, bu_ref, bv_ref, cqt_ref, ck_ref, cvt_ref, lam_ref, *, lam_init):
    tn = x_ref.shape[1]
    nsub = tn // ATT_BLOCK
    h = _rms(x_ref[0], g_ref[...]).astype(BF16)
    r1 = _dot(h, wstd_ref[...])
    for hh in range(DIFF_HEADS):
        kk = r1[:, hh * KDIM:(hh + 1) * KDIM] + kaug_ref[hh]
        for j in range(nsub):
            k_ref[0, hh, j] = (kk[j * ATT_BLOCK:(j + 1) * ATT_BLOCK] + ktab_ref[hh, j]).astype(BF16)
    o = DIFF_HEADS * KDIM
    bu_ref[0] = r1[:, o:o + GMLP_WIDTH]
    o += GMLP_WIDTH
    bv_ref[0] = r1[:, o:o + GMLP_WIDTH]
    o += GMLP_WIDTH
    ck_ref[0] = r1[:, o:o + 128].astype(BF16)

    r2 = _dot_nt(wtr_ref[...], h)
    ones_row = jnp.where(lax.broadcasted_iota(jnp.int32, (VT_ROWS, ATT_BLOCK), 0) == DIFF_V_DIM, 1.0, 0.0)
    vo = DIFF_HEADS * KDIM
    for hh in range(DIFF_HEADS):
        qt_ref[0, hh] = (r2[hh * KDIM:(hh + 1) * KDIM] * Q_SCALE).astype(BF16)
        vv = r2[vo + hh * VT_ROWS: vo + (hh + 1) * VT_ROWS]
        for j in range(nsub):
            vt_ref[0, hh, j] = (vv[:, j * ATT_BLOCK:(j + 1) * ATT_BLOCK] + ones_row).astype(BF16)
    o = vo + DIFF_HEADS * VT_ROWS
    cqt_ref[0] = r2[o:o + SWA_WIDTH].astype(BF16)
    o += SWA_WIDTH
    ones_rows = jnp.where(lax.broadcasted_iota(jnp.int32, (SWA_KV_HEADS * VT_ROWS, tn), 0) % VT_ROWS == HEAD_DIM,
                          1.0, 0.0)
    cvt_ref[0] = (r2[o:o + SWA_KV_HEADS * VT_ROWS] + ones_rows).astype(BF16)

    lp = lamp_ref[...]
    s1 = jnp.sum(lp[0:1] * lp[1:2], axis=-1, keepdims=True)
    s2 = jnp.sum(lp[2:3] * lp[3:4], axis=-1, keepdims=True)
    lam = jnp.exp(s1) - jnp.exp(s2) + lam_init
    lam_ref[...] = jnp.broadcast_to(lam, lam_ref.shape)


def _proj_call(x, g, wstd, wtr, kaug, ktab, lamp, lam_init):
    b, s, d = x.shape
    tn = TOKEN_TILE
    nsub = tn // ATT_BLOCK
    nkb = s // ATT_BLOCK
    grid = (b, s // tn)
    tok = lambda bi, si: (bi, si, 0)
    out_shape = (
        jax.ShapeDtypeStruct((b, DIFF_HEADS, nkb, ATT_BLOCK, KDIM), BF16),
        jax.ShapeDtypeStruct((b, DIFF_HEADS, KDIM, s), BF16),
        jax.ShapeDtypeStruct((b, DIFF_HEADS, nkb, VT_ROWS, ATT_BLOCK), BF16),
        jax.ShapeDtypeStruct((b, s, GMLP_WIDTH), F32),
        jax.ShapeDtypeStruct((b, s, GMLP_WIDTH), F32),
        jax.ShapeDtypeStruct((b, SWA_WIDTH, s), BF16),
        jax.ShapeDtypeStruct((b, s, 128), BF16),
        jax.ShapeDtypeStruct((b, SWA_KV_HEADS * VT_ROWS, s), BF16),
        jax.ShapeDtypeStruct((8, LANES), F32),
    )
    out_specs = (
        pl.BlockSpec((1, DIFF_HEADS, nsub, ATT_BLOCK, KDIM), lambda bi, si: (bi, 0, si, 0, 0)),
        pl.BlockSpec((1, DIFF_HEADS, KDIM, tn), lambda bi, si: (bi, 0, 0, si)),
        pl.BlockSpec((1, DIFF_HEADS, nsub, VT_ROWS, ATT_BLOCK), lambda bi, si: (bi, 0, si, 0, 0)),
        pl.BlockSpec((1, tn, GMLP_WIDTH), tok),
        pl.BlockSpec((1, tn, GMLP_WIDTH), tok),
        pl.BlockSpec((1, SWA_WIDTH, tn), lambda bi, si: (bi, 0, si)),
        pl.BlockSpec((1, tn, 128), tok),
        pl.BlockSpec((1, SWA_KV_HEADS * VT_ROWS, tn), lambda bi, si: (bi, 0, si)),
        pl.BlockSpec((8, LANES), lambda bi, si: (0, 0)),
    )
    in_specs = [
        pl.BlockSpec((1, tn, d), tok),
        _const_spec(g.shape),
        _const_spec(wstd.shape),
        _const_spec(wtr.shape),
        _const_spec(kaug.shape),
        pl.BlockSpec((DIFF_HEADS, nsub, 1, KDIM), lambda bi, si: (0, si, 0, 0)),
        _const_spec(lamp.shape),
    ]
    return pl.pallas_call(
        functools.partial(_proj_kernel, lam_init=lam_init),
        grid=grid, in_specs=in_specs, out_specs=out_specs, out_shape=out_shape,
        compiler_params=_params(("arbitrary", "arbitrary")),
        name="proj",
    )(x, g, wstd, wtr, kaug, ktab, lamp)


def _diff_attn_kernel(slope_ref, qt_ref, k_ref, vt_ref, qaug_ref, diag_ref, lam_ref, gsub_ref,
                      o_ref, w_ref, *scratch, out_scale):
    s_bufs = scratch[0:ATT_BUFFERS]
    p_bufs = scratch[ATT_BUFFERS:2 * ATT_BUFFERS]
    acc_ref = scratch[2 * ATT_BUFFERS]
    hh = pl.program_id(1)
    qi = pl.program_id(2)
    blk = ATT_BLOCK
    tq = ATT_QBLOCK
    n_rest = k_ref.shape[2] // 2 - 1
    sigma = slope_ref[hh] * float(blk)

    qt = qt_ref[0, 0].astype(F32)
    qaug = qaug_ref[0]
    row = lax.broadcasted_iota(jnp.int32, (KDIM, tq), 0)
    comp_mask = (row < 64, row >= 64)
    zero = jnp.zeros_like(qt)
    for c in range(2):
        w_ref[0, c] = jnp.where(comp_mask[c], qt - qaug, zero).astype(BF16)
        w_ref[1, c] = jnp.where(comp_mask[c], qt + qaug, zero).astype(BF16)

    def key_rows(kb):
        return jnp.concatenate([k_ref[0, 0, 2 * kb], k_ref[0, 0, 2 * kb + 1]], axis=0)

    def value_cols(kb):
        return jnp.concatenate([vt_ref[0, 0, 2 * kb], vt_ref[0, 0, 2 * kb + 1]], axis=1)

    k_t = key_rows(qi)
    vt = value_cols(qi)
    m = []
    for c in range(2):
        s = _dot(k_t, jnp.where(comp_mask[c], qt, zero).astype(BF16)) + diag_ref[0]
        m_c = jnp.max(s, axis=0, keepdims=True)
        acc_ref[c] = _dot(vt, jnp.exp2((s - m_c).astype(BF16)))
        m.append(m_c)

    dyn_row = lax.broadcasted_iota(jnp.int32, (DYN_ROWS, tq), 0)
    lane_row = lax.broadcasted_iota(jnp.int32, (1, tq), 1)
    q_origin = sigma * jnp.where(lane_row < blk, 2 * qi, 2 * qi + 1).astype(F32)

    def rest_block(j):
        after = (j >= qi).astype(jnp.int32)
        return j + after, after

    def reference_rows(rr, key_sign):
        hi = rr.astype(BF16).astype(F32)
        mid = (rr - hi).astype(BF16).astype(F32)
        lo = rr - hi - mid
        rows = jnp.where(dyn_row == 0, hi, jnp.where(dyn_row == 1, mid, jnp.where(dyn_row == 2, lo, 0.0)))
        rows = jnp.where((dyn_row >= 3) & (dyn_row < 6), key_sign, rows)
        return rows.astype(BF16)

    def column_max(z):
        parts = [z[i * 16:(i + 1) * 16] for i in range(z.shape[0] // 16)]
        while len(parts) > 1:
            parts = [jnp.maximum(parts[i], parts[i + 1]) for i in range(0, len(parts), 2)]
        return jnp.max(parts[0].astype(F32), axis=0, keepdims=True)

    def stage_scores(j, ref_max, s_buf):
        kb, side = rest_block(j)
        sgn = jnp.where(side == 1, 1.0, -1.0)
        k_t = key_rows(kb)
        for c in range(2):
            dyn = reference_rows(sgn * q_origin - ref_max[c], -sgn)
            w = jnp.concatenate([w_ref[side, c, 0:DYN_ROW0, :], dyn], axis=0)
            s_buf[c] = jnp.dot(k_t, w, preferred_element_type=F32).astype(BF16)

    def stage_softmax(m2, m1, s_buf, p_buf):
        alphas, m_out = [], []
        for c in range(2):
            z = s_buf[c]
            d = jnp.maximum(m1[c] - m2[c], column_max(z)).astype(BF16)
            m_new = m2[c] + d.astype(F32)
            p_buf[c] = jnp.exp2(z - d)
            alphas.append(jnp.exp2(m1[c] - m_new))
            m_out.append(m_new)
        return alphas, m_out

    def stage_pv(j, p_buf, alphas):
        kb, _ = rest_block(j)
        vt = value_cols(kb)
        for c in range(2):
            acc_ref[c] = acc_ref[c] * alphas[c] + jnp.dot(vt, p_buf[c], preferred_element_type=F32)

    nbuf = ATT_BUFFERS

    def full_step(t, t_mod, state):
        m_t3, m_t2, al_prev = state
        kb_s, side = rest_block(t)
        sgn = jnp.where(side == 1, 1.0, -1.0)
        k_t = key_rows(kb_s)
        kb_v, _ = rest_block(t - 2)
        vt = value_cols(kb_v)
        s_buf = s_bufs[t_mod]
        p_buf = p_bufs[(t_mod - 2) % nbuf]
        for c in range(2):
            dyn = reference_rows(sgn * q_origin - m_t2[c], -sgn)
            w = jnp.concatenate([w_ref[side, c, 0:DYN_ROW0, :], dyn], axis=0)
            for n in range(tq // blk):
                cols = slice(n * blk, (n + 1) * blk)
                s_buf[c, :, cols] = jnp.dot(k_t, w[:, cols], preferred_element_type=F32).astype(BF16)
                acc_ref[c, :, cols] = (acc_ref[c, :, cols] * al_prev[c][:, cols]
                                       + jnp.dot(vt, p_buf[c, :, cols], preferred_element_type=F32))
        al_new, m_t1 = stage_softmax(m_t3, m_t2, s_bufs[(t_mod - 1) % nbuf], p_bufs[(t_mod - 1) % nbuf])
        return m_t2, m_t1, al_new

    stage_scores(0, m, s_bufs[0])
    stage_scores(1, m, s_bufs[1])
    al_a, m_a = stage_softmax(m, m, s_bufs[0], p_bufs[0])
    state = (m, m_a, al_a)

    n_full = n_rest - 2
    unroll = next((u for u in ATT_UNROLL_CHOICES if u <= n_full), 0)
    n_peel = n_full % unroll if unroll else n_full
    for t in range(2, 2 + n_peel):
        state = full_step(t, t % nbuf, state)
    t0 = 2 + n_peel

    def steady(u, state):
        for i in range(unroll):
            state = full_step(t0 + unroll * u + i, (t0 + i) % nbuf, state)
        return state

    if unroll:
        state = lax.fori_loop(0, (n_full - n_peel) // unroll, steady, state)

    m_t3, m_t2, al_prev = state
    last = (n_rest - 1) % nbuf
    al_last, _ = stage_softmax(m_t3, m_t2, s_bufs[last], p_bufs[last])
    stage_pv(n_rest - 2, p_bufs[(n_rest - 2) % nbuf], al_prev)
    stage_pv(n_rest - 1, p_bufs[last], al_last)

    a0 = acc_ref[0]
    a1 = acc_ref[1]
    o0 = a0[0:DIFF_V_DIM] / a0[DIFF_V_DIM:DIFF_V_DIM + 1]
    o1 = a1[0:DIFF_V_DIM] / a1[DIFF_V_DIM:DIFF_V_DIM + 1]
    lam = lam_ref[0:1, 0:1]
    o = o0 - lam * o1
    ms = jnp.mean(o * o, axis=0, keepdims=True)
    y = o * lax.rsqrt(ms + EPS) * gsub_ref[...] * out_scale
    o_ref[0] = y.astype(o_ref.dtype)


def _diff_attn_call(slope2, qt, kblk, vtblk, qaug, diag, lam_tile, gsub_b, out_scale):
    b, nh, kdim, s = qt.shape
    nkb = kblk.shape[2]
    blk = ATT_BLOCK
    tq = ATT_QBLOCK
    grid = (b, nh, s // tq)
    in_specs = [
        pl.BlockSpec(memory_space=pltpu.SMEM),
        pl.BlockSpec((1, 1, kdim, tq), lambda bi, hi, qi: (bi, hi, 0, qi)),
        pl.BlockSpec((1, 1, nkb, blk, kdim), lambda bi, hi, qi: (bi, hi, 0, 0, 0)),
        pl.BlockSpec((1, 1, nkb, VT_ROWS, blk), lambda bi, hi, qi: (bi, hi, 0, 0, 0)),
        pl.BlockSpec((1, kdim, tq), lambda bi, hi, qi: (hi, 0, 0)),
        pl.BlockSpec((1, tq, tq), lambda bi, hi, qi: (hi, 0, 0)),
        _const_spec(lam_tile.shape),
        _const_spec(gsub_b.shape),
    ]
    s_buf = pltpu.VMEM((2, tq, tq), BF16)
    p_buf = pltpu.VMEM((2, tq, tq), BF16)
    return pl.pallas_call(
        functools.partial(_diff_attn_kernel, out_scale=out_scale),
        grid=grid, in_specs=in_specs,
        out_specs=pl.BlockSpec((1, DIFF_V_DIM, tq), lambda bi, hi, qi: (bi, hi, qi)),
        out_shape=jax.ShapeDtypeStruct((b, nh * DIFF_V_DIM, s), BF16),
        scratch_shapes=[pltpu.VMEM((2, 2, kdim, tq), BF16),
                        *([s_buf] * ATT_BUFFERS), *([p_buf] * ATT_BUFFERS),
                        pltpu.VMEM((2, VT_ROWS, tq), F32)],
        compiler_params=_params(("arbitrary", "arbitrary", "arbitrary")),
        name="diff_attn",
    )(slope2, qt, kblk, vtblk, qaug, diag, lam_tile, gsub_b)


def _swa_kernel(sink_ref, qt_ref, kp_ref, kc_ref, kn_ref, vp_ref, vc_ref, vn_ref, bias_ref, o_ref, *, seq_len):
    qi = pl.program_id(1)
    blk = SWA_BLOCK
    nsub = SWA_TILE // blk
    k_all = jnp.concatenate([kp_ref[0], kc_ref[0], kn_ref[0]], axis=0)
    vt_all = jnp.concatenate([vp_ref[0], vc_ref[0], vn_ref[0]], axis=1)
    key_row = lax.broadcasted_iota(jnp.int32, (3 * blk, SWA_HEADS * blk), 0)
    no_q = jnp.zeros((HEAD_DIM, SWA_GROUP * blk), BF16)
    sink = sink_ref[...]
    gw = SWA_GROUP * blk
    for sub in range(nsub):
        key_pos = (qi * nsub + sub - 1) * blk + key_row
        in_seq = (key_pos >= 0) & (key_pos < seq_len)
        kk = k_all[sub * blk:(sub + 3) * blk]
        q_t = [qt_ref[0, hq * HEAD_DIM:(hq + 1) * HEAD_DIM, sub * blk:(sub + 1) * blk] for hq in range(SWA_HEADS)]
        w = jnp.concatenate([jnp.concatenate(q_t[:SWA_GROUP] + [no_q], axis=1),
                             jnp.concatenate([no_q] + q_t[SWA_GROUP:], axis=1)], axis=0)
        sc = _dot(kk, w) + bias_ref[...]
        sc = jnp.where(in_seq, sc, NEG_INF)
        m = jnp.maximum(jnp.max(sc, axis=0, keepdims=True), sink)
        e = jnp.exp(sc - m).astype(BF16)
        tail = jnp.exp(sink - m)
        for kh in range(SWA_KV_HEADS):
            vt = vt_all[kh * VT_ROWS:(kh + 1) * VT_ROWS, sub * blk:(sub + 3) * blk]
            acc = _dot(vt, e[:, kh * gw:(kh + 1) * gw])
            o = acc[0:HEAD_DIM] / (acc[HEAD_DIM:HEAD_DIM + 1] + tail[:, kh * gw:(kh + 1) * gw])
            for g in range(SWA_GROUP):
                hq = kh * SWA_GROUP + g
                o_ref[0, hq * HEAD_DIM:(hq + 1) * HEAD_DIM, sub * blk:(sub + 1) * blk] = (
                    o[:, g * blk:(g + 1) * blk].astype(o_ref.dtype))


def _swa_call(sinks, cqt, ck, cvt, bias):
    b, _, s = cqt.shape
    blk = SWA_BLOCK
    tile = SWA_TILE
    nsub = tile // blk
    nb = s // blk
    prev_i = lambda qi: jnp.maximum(qi * nsub - 1, 0)
    next_i = lambda qi: jnp.minimum((qi + 1) * nsub, nb - 1)
    vrows = SWA_KV_HEADS * VT_ROWS
    in_specs = [
        _const_spec(sinks.shape),
        pl.BlockSpec((1, SWA_WIDTH, tile), lambda bi, qi: (bi, 0, qi)),
        pl.BlockSpec((1, blk, 128), lambda bi, qi: (bi, prev_i(qi), 0)),
        pl.BlockSpec((1, tile, 128), lambda bi, qi: (bi, qi, 0)),
        pl.BlockSpec((1, blk, 128), lambda bi, qi: (bi, next_i(qi), 0)),
        pl.BlockSpec((1, vrows, blk), lambda bi, qi: (bi, 0, prev_i(qi))),
        pl.BlockSpec((1, vrows, tile), lambda bi, qi: (bi, 0, qi)),
        pl.BlockSpec((1, vrows, blk), lambda bi, qi: (bi, 0, next_i(qi))),
        _const_spec(bias.shape),
    ]
    return pl.pallas_call(
        functools.partial(_swa_kernel, seq_len=s),
        grid=(b, s // tile), in_specs=in_specs,
        out_specs=pl.BlockSpec((1, SWA_WIDTH, tile), lambda bi, qi: (bi, 0, qi)),
        out_shape=jax.ShapeDtypeStruct((b, SWA_WIDTH, s), BF16),
        compiler_params=_params(("arbitrary", "arbitrary")),
        name="swa",
    )(sinks, cqt, ck, ck, ck, cvt, cvt, cvt, bias)


def _mix_out_kernel(x_ref, yat_ref, bu_ref, bv_ref, yct_ref, lng_ref, lnb_ref, ws_ref, bs_ref,
                    wa_ref, wb_ref, wc_ref, g_ref, o_ref):
    tn = x_ref.shape[1]
    v = bv_ref[0]
    mu = jnp.mean(v, axis=-1, keepdims=True)
    var = jnp.mean(jnp.square(v - mu), axis=-1, keepdims=True)
    vn = ((v - mu) * lax.rsqrt(var + EPS) * lng_ref[...] + lnb_ref[...]).astype(BF16)
    lane_group = lax.broadcasted_iota(jnp.int32, (CHUNK, GMLP_WIDTH), 1) // GMLP_GROUP_DIM
    u = bu_ref[0]
    yb = []
    for c in range(tn // CHUNK):
        vc = vn[c * CHUNK:(c + 1) * CHUNK]
        mixed = bs_ref[...]
        for g in range(GMLP_GROUPS):
            mixed = mixed + jnp.where(lane_group == g, _dot(ws_ref[g], vc), 0.0)
        yb.append(u[c * CHUNK:(c + 1) * CHUNK] * mixed)
    yb = jnp.concatenate(yb, axis=0).astype(BF16)
    y = _dot_tn(yat_ref[0], wa_ref[...]) + _dot(yb, wb_ref[...]) + _dot_tn(yct_ref[0], wc_ref[...])
    o_ref[0] = x_ref[0] + _rms(y, g_ref[...])


def _mix_out_call(x, yat, bu, bv, yc, lng, lnb, ws, bs, wa, wb, wc, g):
    b, s, d = x.shape
    tn = TOKEN_TILE
    tok = lambda bi, si: (bi, si, 0)
    in_specs = [
        pl.BlockSpec((1, tn, d), tok),
        pl.BlockSpec((1, DIFF_WIDTH, tn), lambda bi, si: (bi, 0, si)),
        pl.BlockSpec((1, tn, GMLP_WIDTH), tok),
        pl.BlockSpec((1, tn, GMLP_WIDTH), tok),
        pl.BlockSpec((1, SWA_WIDTH, tn), lambda bi, si: (bi, 0, si)),
    ] + [_const_spec(a.shape) for a in (lng, lnb, ws, bs, wa, wb, wc, g)]
    return pl.pallas_call(
        _mix_out_kernel,
        grid=(b, s // tn), in_specs=in_specs,
        out_specs=pl.BlockSpec((1, tn, d), tok),
        out_shape=jax.ShapeDtypeStruct((b, s, d), F32),
        compiler_params=_params(("arbitrary", "arbitrary")),
        name="mix_out",
    )(x, yat, bu, bv, yc, lng, lnb, ws, bs, wa, wb, wc, g)


def _sigmoid(z):
    return 1.0 / (1.0 + jnp.exp(-z))


def _ffn_ple_kernel(x_ref, p_ref, gpre_ref, wg_ref, wu_ref, wo_ref, gpost_ref,
                    wup_ref, wgate_ref, ggate_ref, gple_ref, o_ref):
    x = x_ref[0]
    h = _rms(x, gpre_ref[...]).astype(BF16)
    f = jnp.zeros(x.shape, F32)
    for j in range(D_FF // FF_CHUNK):
        sl = slice(j * FF_CHUNK, (j + 1) * FF_CHUNK)
        gate = _dot(h, wg_ref[:, sl])
        up = _dot(h, wu_ref[:, sl])
        a = (gate * _sigmoid(gate) * up).astype(BF16)
        f = f + _dot(a, wo_ref[sl, :])
    x = x + _rms(f, gpost_ref[...])
    e = _dot(p_ref[0, 0].astype(BF16), wup_ref[...])
    gt = _sigmoid(_dot(_rms(x, ggate_ref[...]).astype(BF16), wgate_ref[...]))
    o_ref[0] = x + _rms(e * gt, gple_ref[...])


def _ffn_ple_call(x, p, layer, gpre, wg, wu, wo, gpost, wup, wgate, ggate, gple):
    b, s, d = x.shape
    tn = TOKEN_TILE
    tok = lambda bi, si: (bi, si, 0)
    in_specs = [
        pl.BlockSpec((1, tn, d), tok),
        pl.BlockSpec((1, 1, tn, PLE_DIM), lambda bi, si: (layer, bi, si, 0)),
    ] + [_const_spec(a.shape) for a in (gpre, wg, wu, wo, gpost, wup, wgate, ggate, gple)]
    return pl.pallas_call(
        _ffn_ple_kernel,
        grid=(b, s // tn), in_specs=in_specs,
        out_specs=pl.BlockSpec((1, tn, d), tok),
        out_shape=jax.ShapeDtypeStruct((b, s, d), F32),
        compiler_params=_params(("arbitrary", "arbitrary")),
        name="ffn_ple",
    )(x, p, gpre, wg, wu, wo, gpost, wup, wgate, ggate, gple)


def _prep_in_weights(w):
    d = w.shape[0]
    aq = w[:, 0:384].reshape(d, DIFF_HEADS, 2, DIFF_QK_DIM)
    ak = w[:, 384:768].reshape(d, DIFF_HEADS, 2, DIFF_QK_DIM)
    av = w[:, 768:1152].reshape(d, DIFF_HEADS, DIFF_V_DIM)
    pad_qk = ((0, 0), (0, 0), (0, 0), (0, 64 - DIFF_QK_DIM))
    wk = jnp.pad(ak, pad_qk).reshape(d, DIFF_HEADS * KDIM)
    wq = jnp.pad(aq, pad_qk).reshape(d, DIFF_HEADS * KDIM)
    wv = jnp.pad(av, ((0, 0), (0, 0), (0, VT_ROWS - DIFF_V_DIM))).reshape(d, DIFF_HEADS * VT_ROWS)
    cq = w[:, 1664:2048] * (HEAD_DIM ** -0.5)
    cv = w[:, 2176:2304].reshape(d, SWA_KV_HEADS, HEAD_DIM)
    cv = jnp.pad(cv, ((0, 0), (0, 0), (0, VT_ROWS - HEAD_DIM))).reshape(d, SWA_KV_HEADS * VT_ROWS)
    wstd = jnp.concatenate([wk, w[:, 1152:1664], w[:, 2048:2176]], axis=1).astype(BF16)
    wtr = jnp.concatenate([wq, wv, cq, cv], axis=1).T.astype(BF16)
    return wstd, wtr


def kernel(x, p, g_pre_mix, w_in, lam_q1, lam_k1, lam_q2, lam_k2, g_diff_sub, gmlp_ln_g, gmlp_ln_b,
           w_spatial, b_spatial, swa_sinks, w_out, g_post_mix, g_pre_ffn, w_ffn_in, w_ffn_out,
           g_post_ffn, w_ple_up, w_ple_gate, g_ple_gate, g_ple_post):
    b, s, d = x.shape
    depth = w_in.shape[0]
    assert d == D_MODEL and s % (2 * ATT_QBLOCK) == 0 and s >= 4 * ATT_QBLOCK and s % TOKEN_TILE == 0

    slope2_np, qaug_np, kaug_np, ktab_np, diag_np = _diff_bias_constants(TOKEN_TILE, s)
    _, swa_slopes_np = _alibi_slopes_np()
    slope2 = jnp.asarray(slope2_np)
    qaug = jnp.asarray(qaug_np)
    kaug = jnp.asarray(kaug_np)
    ktab = jnp.asarray(ktab_np)
    diag = jnp.asarray(diag_np)
    swa_bias = jnp.asarray(_swa_bias_np(swa_slopes_np))
    row = lambda a: a.reshape(1, -1).astype(F32)

    for l in range(depth):
        lam_init = 0.8 - 0.6 * math.exp(-0.3 * l)
        wstd, wtr = _prep_in_weights(w_in[l])
        lamp = jnp.stack([lam_q1[l], lam_k1[l], lam_q2[l], lam_k2[l]]).astype(F32)
        kblk, qt, vtblk, bu, bv, cq, ck, cv, lam_tile = _proj_call(
            x, row(g_pre_mix[l]), wstd, wtr, kaug, ktab, lamp, lam_init)

        gsub_b = jnp.broadcast_to(g_diff_sub[l].astype(F32)[:, None], (DIFF_V_DIM, ATT_QBLOCK))
        yat = _diff_attn_call(slope2, qt, kblk, vtblk, qaug, diag, lam_tile, gsub_b, 1.0 - lam_init)
        sink_row = jnp.repeat(swa_sinks[l].astype(F32), SWA_BLOCK).reshape(1, SWA_HEADS * SWA_BLOCK)
        yc = _swa_call(sink_row, cq, ck, cv, swa_bias)

        bs = jnp.broadcast_to(b_spatial[l].T[:, :, None], (CHUNK, GMLP_GROUPS, GMLP_GROUP_DIM))
        bs = bs.reshape(CHUNK, GMLP_WIDTH).astype(F32)
        wo = w_out[l].astype(BF16)
        x = _mix_out_call(
            x, yat, bu, bv, yc, row(gmlp_ln_g[l]), row(gmlp_ln_b[l]), w_spatial[l].astype(BF16), bs,
            wo[0:DIFF_WIDTH], wo[DIFF_WIDTH:DIFF_WIDTH + GMLP_WIDTH], wo[DIFF_WIDTH + GMLP_WIDTH:],
            row(g_post_mix[l]))

        wfi = w_ffn_in[l].astype(BF16)
        x = _ffn_ple_call(
            x, p, l, row(g_pre_ffn[l]), wfi[:, :D_FF], wfi[:, D_FF:], w_ffn_out[l].astype(BF16),
            row(g_post_ffn[l]), w_ple_up[l].astype(BF16), w_ple_gate[l].astype(BF16),
            row(g_ple_gate[l]), row(g_ple_post[l]))
    return x
```

```python
import functools
import math

import numpy as np
import jax
import jax.numpy as jnp
from jax import lax
from jax.experimental import pallas as pl
from jax.experimental.pallas import tpu as pltpu

D_MODEL = 1024
HEAD_DIM = 64
DIFF_HEADS = 6
DIFF_QK_DIM = 32
DIFF_V_DIM = 64
DIFF_WIDTH = DIFF_HEADS * DIFF_V_DIM
GMLP_GROUPS = 4
GMLP_GROUP_DIM = 64
GMLP_WIDTH = GMLP_GROUPS * GMLP_GROUP_DIM
CHUNK = 128
SWA_HEADS = 6
SWA_KV_HEADS = 2
SWA_GROUP = SWA_HEADS // SWA_KV_HEADS
SWA_WIDTH = SWA_HEADS * HEAD_DIM
WINDOW = 128
SWA_BLOCK = 128
SWA_TILE = 512
D_FF = 2816
PLE_DIM = 256
N_ATTN_HEADS = DIFF_HEADS + SWA_HEADS
ALIBI_MAX_EXP = 8.0
EPS = 1e-6
NEG_INF = -1e30
LOG2E = 1.4426950408889634

LANES = 128
MXU_DIM_V7X = 256
VMEM_LIMIT_BYTES_V7X = 56 * 1024 * 1024

TOKEN_TILE = 512
ATT_BLOCK = MXU_DIM_V7X
ATT_QBLOCK = 2 * ATT_BLOCK
KDIM = 2 * 64
AUG_OFF = DIFF_QK_DIM
ATT_BUFFERS = 3
ATT_UNROLL_CHOICES = (27, 9, 3)
DYN_ROW0 = 112
DYN_ROWS = KDIM - DYN_ROW0
VT_ROWS = 80
FF_CHUNK = 256

F32 = jnp.float32
BF16 = jnp.bfloat16


def _bf16_round_np(x):
    u = np.asarray(x, np.float32).view(np.uint32).astype(np.uint64)
    r = ((u >> 16) & 1) + 0x7FFF
    return ((u + r) & 0xFFFF0000).astype(np.uint32).view(np.float32)


def _alibi_slopes_np():
    k = np.arange(1, N_ATTN_HEADS + 1, dtype=np.float64)
    s = np.exp2(-ALIBI_MAX_EXP * k / N_ATTN_HEADS).astype(np.float32)
    return s[SWA_HEADS:], s[:SWA_HEADS]


def _split3_bf16(v):
    v = np.asarray(v, np.float32)
    hi = _bf16_round_np(v)
    mid = _bf16_round_np(v - hi)
    lo = _bf16_round_np(v - hi - mid)
    return hi, mid, lo


def _diff_bias_constants(token_tile, seq_len):
    slopes, _ = _alibi_slopes_np()
    slope2 = (slopes.astype(np.float64) * LOG2E).astype(np.float32)
    hi, mid, lo = _split3_bf16(slope2)
    parts = np.stack([hi, mid, lo], axis=1)
    rel = np.arange(ATT_BLOCK, dtype=np.float32)
    qaug = np.zeros((DIFF_HEADS, KDIM, ATT_BLOCK), np.float32)
    kaug = np.zeros((DIFF_HEADS, ATT_BLOCK, KDIM), np.float32)
    for c in range(2):
        base = c * 64 + AUG_OFF
        for t in range(3):
            qaug[:, base + t, :] = rel[None, :]
            qaug[:, base + 3 + t, :] = -parts[:, t][:, None]
            kaug[:, :, base + t] = parts[:, t][:, None]
            kaug[:, :, base + 3 + t] = rel[None, :]
    kaug[:, :, DYN_ROW0:DYN_ROW0 + 3] = 1.0
    kaug = np.tile(kaug, (1, token_tile // ATT_BLOCK, 1))
    qaug = np.tile(qaug, (1, 1, ATT_QBLOCK // ATT_BLOCK))
    n_kb = seq_len // ATT_BLOCK
    sigma_j = (slope2[:, None] * np.float32(ATT_BLOCK)) * np.arange(n_kb, dtype=np.float32)[None, :]
    ktab = np.zeros((DIFF_HEADS, n_kb, 1, KDIM), np.float32)
    for t, piece in enumerate(_split3_bf16(sigma_j)):
        ktab[:, :, 0, DYN_ROW0 + 3 + t] = piece
    pos = np.arange(ATT_QBLOCK, dtype=np.float32)
    dist = np.abs(pos[:, None] - pos[None, :])
    diag = -(slope2[:, None, None] * dist[None])
    return slope2, qaug, kaug, ktab, diag.astype(np.float32)


def _swa_bias_np(slopes):
    key = np.arange(3 * SWA_BLOCK, dtype=np.float32)[:, None]
    qry = np.arange(SWA_BLOCK, dtype=np.float32)[None, :]
    dist = np.abs(key - SWA_BLOCK - qry)
    bias = -(slopes.astype(np.float32)[:, None, None] * dist[None])
    bias = np.where(dist[None] <= WINDOW, bias, np.float32(NEG_INF)).astype(np.float32)
    return np.concatenate(list(bias), axis=1)


def _rms(x, g):
    return x * lax.rsqrt(jnp.mean(x * x, axis=-1, keepdims=True) + EPS) * g


def _dot(a, b):
    return jnp.dot(a, b, preferred_element_type=F32)


def _dot_nt(a, b):
    return lax.dot_general(a, b, (((1,), (1,)), ((), ())), preferred_element_type=F32)


def _dot_tn(a, b):
    return lax.dot_general(a, b, (((0,), (0,)), ((), ())), preferred_element_type=F32)


def _const_spec(shape):
    nd = len(shape)
    return pl.BlockSpec(shape, lambda *_: (0,) * nd, pipeline_mode=pl.Buffered(1))


def _params(sem, flags=None):
    return pltpu.CompilerParams(dimension_semantics=sem, vmem_limit_bytes=VMEM_LIMIT_BYTES_V7X, flags=flags)


N_STD = DIFF_HEADS * KDIM + 2 * GMLP_WIDTH + SWA_KV_HEADS * HEAD_DIM
N_TR = DIFF_HEADS * KDIM + DIFF_HEADS * VT_ROWS + SWA_WIDTH + SWA_KV_HEADS * VT_ROWS
Q_SCALE = (DIFF_QK_DIM ** -0.5) * LOG2E


def _proj_kernel(x_ref, g_ref, wstd_ref, wtr_ref, kaug_ref, ktab_ref, lamp_ref,
                 k_ref, qt_ref, vt_ref, bu_ref, bv_ref, cqt_ref, ck_ref, cvt_ref, lam_ref, *, lam_init):
    tn = x_ref.shape[1]
    nsub = tn // ATT_BLOCK
    h = _rms(x_ref[0], g_ref[...]).astype(BF16)
    r1 = _dot(h, wstd_ref[...])
    for hh in range(DIFF_HEADS):
        kk = r1[:, hh * KDIM:(hh + 1) * KDIM] + kaug_ref[hh]
        for j in range(nsub):
            k_ref[0, hh, j] = (kk[j * ATT_BLOCK:(j + 1) * ATT_BLOCK] + ktab_ref[hh, j]).astype(BF16)
    o = DIFF_HEADS * KDIM
    bu_ref[0] = r1[:, o:o + GMLP_WIDTH]
    o += GMLP_WIDTH
    bv_ref[0] = r1[:, o:o + GMLP_WIDTH]
    o += GMLP_WIDTH
    ck_ref[0] = r1[:, o:o + 128].astype(BF16)

    r2 = _dot_nt(wtr_ref[...], h)
    ones_row = jnp.where(lax.broadcasted_iota(jnp.int32, (VT_ROWS, ATT_BLOCK), 0) == DIFF_V_DIM, 1.0, 0.0)
    vo = DIFF_HEADS * KDIM
    for hh in range(DIFF_HEADS):
        qt_ref[0, hh] = (r2[hh * KDIM:(hh + 1) * KDIM] * Q_SCALE).astype(BF16)
        vv = r2[vo + hh * VT_ROWS: vo + (hh + 1) * VT_ROWS]
        for j in range(nsub):
            vt_ref[0, hh, j] = (vv[:, j * ATT_BLOCK:(j + 1) * ATT_BLOCK] + ones_row).astype(BF16)
    o = vo + DIFF_HEADS * VT_ROWS
    cqt_ref[0] = r2[o:o + SWA_WIDTH].astype(BF16)
    o += SWA_WIDTH
    ones_rows = jnp.where(lax.broadcasted_iota(jnp.int32, (SWA_KV_HEADS * VT_ROWS, tn), 0) % VT_ROWS == HEAD_DIM,
                          1.0, 0.0)
    cvt_ref[0] = (r2[o:o + SWA_KV_HEADS * VT_ROWS] + ones_rows).astype(BF16)

    lp = lamp_ref[...]
    s1 = jnp.sum(lp[0:1] * lp[1:2], axis=-1, keepdims=True)
    s2 = jnp.sum(lp[2:3] * lp[3:4], axis=-1, keepdims=True)
    lam = jnp.exp(s1) - jnp.exp(s2) + lam_init
    lam_ref[...] = jnp.broadcast_to(lam, lam_ref.shape)


def _proj_call(x, g, wstd, wtr, kaug, ktab, lamp, lam_init):
    b, s, d = x.shape
    tn = TOKEN_TILE
    nsub = tn // ATT_BLOCK
    nkb = s // ATT_BLOCK
    grid = (b, s // tn)
    tok = lambda bi, si: (bi, si, 0)
    out_shape = (
        jax.ShapeDtypeStruct((b, DIFF_HEADS, nkb, ATT_BLOCK, KDIM), BF16),
        jax.ShapeDtypeStruct((b, DIFF_HEADS, KDIM, s), BF16),
        jax.ShapeDtypeStruct((b, DIFF_HEADS, nkb, VT_ROWS, ATT_BLOCK), BF16),
        jax.ShapeDtypeStruct((b, s, GMLP_WIDTH), F32),
        jax.ShapeDtypeStruct((b, s, GMLP_WIDTH), F32),
        jax.ShapeDtypeStruct((b, SWA_WIDTH, s), BF16),
        jax.ShapeDtypeStruct((b, s, 128), BF16),
        jax.ShapeDtypeStruct((b, SWA_KV_HEADS * VT_ROWS, s), BF16),
        jax.ShapeDtypeStruct((8, LANES), F32),
    )
    out_specs = (
        pl.BlockSpec((1, DIFF_HEADS, nsub, ATT_BLOCK, KDIM), lambda bi, si: (bi, 0, si, 0, 0)),
        pl.BlockSpec((1, DIFF_HEADS, KDIM, tn), lambda bi, si: (bi, 0, 0, si)),
        pl.BlockSpec((1, DIFF_HEADS, nsub, VT_ROWS, ATT_BLOCK), lambda bi, si: (bi, 0, si, 0, 0)),
        pl.BlockSpec((1, tn, GMLP_WIDTH), tok),
        pl.BlockSpec((1, tn, GMLP_WIDTH), tok),
        pl.BlockSpec((1, SWA_WIDTH, tn), lambda bi, si: (bi, 0, si)),
        pl.BlockSpec((1, tn, 128), tok),
        pl.BlockSpec((1, SWA_KV_HEADS * VT_ROWS, tn), lambda bi, si: (bi, 0, si)),
        pl.BlockSpec((8, LANES), lambda bi, si: (0, 0)),
    )
    in_specs = [
        pl.BlockSpec((1, tn, d), tok),
        _const_spec(g.shape),
        _const_spec(wstd.shape),
        _const_spec(wtr.shape),
        _const_spec(kaug.shape),
        pl.BlockSpec((DIFF_HEADS, nsub, 1, KDIM), lambda bi, si: (0, si, 0, 0)),
        _const_spec(lamp.shape),
    ]
    return pl.pallas_call(
        functools.partial(_proj_kernel, lam_init=lam_init),
        grid=grid, in_specs=in_specs, out_specs=out_specs, out_shape=out_shape,
        compiler_params=_params(("arbitrary", "arbitrary")),
        name="proj",
    )(x, g, wstd, wtr, kaug, ktab, lamp)


def _diff_attn_kernel(slope_ref, qt_ref, k_ref, vt_ref, qaug_ref, diag_ref, lam_ref, gsub_ref,
                      o_ref, w_ref, *scratch, out_scale):
    s_bufs = scratch[0:ATT_BUFFERS]
    p_bufs = scratch[ATT_BUFFERS:2 * ATT_BUFFERS]
    acc_ref = scratch[2 * ATT_BUFFERS]
    hh = pl.program_id(1)
    qi = pl.program_id(2)
    blk = ATT_BLOCK
    tq = ATT_QBLOCK
    n_rest = k_ref.shape[2] // 2 - 1
    sigma = slope_ref[hh] * float(blk)

    qt = qt_ref[0, 0].astype(F32)
    qaug = qaug_ref[0]
    row = lax.broadcasted_iota(jnp.int32, (KDIM, tq), 0)
    comp_mask = (row < 64, row >= 64)
    zero = jnp.zeros_like(qt)
    for c in range(2):
        w_ref[0, c] = jnp.where(comp_mask[c], qt - qaug, zero).astype(BF16)
        w_ref[1, c] = jnp.where(comp_mask[c], qt + qaug, zero).astype(BF16)

    def key_rows(kb):
        return jnp.concatenate([k_ref[0, 0, 2 * kb], k_ref[0, 0, 2 * kb + 1]], axis=0)

    def value_cols(kb):
        return jnp.concatenate([vt_ref[0, 0, 2 * kb], vt_ref[0, 0, 2 * kb + 1]], axis=1)

    k_t = key_rows(qi)
    vt = value_cols(qi)
    m = []
    for c in range(2):
        s = _dot(k_t, jnp.where(comp_mask[c], qt, zero).astype(BF16)) + diag_ref[0]
        m_c = jnp.max(s, axis=0, keepdims=True)
        acc_ref[c] = _dot(vt, jnp.exp2((s - m_c).astype(BF16)))
        m.append(m_c)

    dyn_row = lax.broadcasted_iota(jnp.int32, (DYN_ROWS, tq), 0)
    lane_row = lax.broadcasted_iota(jnp.int32, (1, tq), 1)
    q_origin = sigma * jnp.where(lane_row < blk, 2 * qi, 2 * qi + 1).astype(F32)

    def rest_block(j):
        after = (j >= qi).astype(jnp.int32)
        return j + after, after

    def reference_rows(rr, key_sign):
        hi = rr.astype(BF16).astype(F32)
        mid = (rr - hi).astype(BF16).astype(F32)
        lo = rr - hi - mid
        rows = jnp.where(dyn_row == 0, hi, jnp.where(dyn_row == 1, mid, jnp.where(dyn_row == 2, lo, 0.0)))
        rows = jnp.where((dyn_row >= 3) & (dyn_row < 6), key_sign, rows)
        return rows.astype(BF16)

    def column_max(z):
        parts = [z[i * 16:(i + 1) * 16] for i in range(z.shape[0] // 16)]
        while len(parts) > 1:
            parts = [jnp.maximum(parts[i], parts[i + 1]) for i in range(0, len(parts), 2)]
        return jnp.max(parts[0].astype(F32), axis=0, keepdims=True)

    def stage_scores(j, ref_max, s_buf):
        kb, side = rest_block(j)
        sgn = jnp.where(side == 1, 1.0, -1.0)
        k_t = key_rows(kb)
        for c in range(2):
            dyn = reference_rows(sgn * q_origin - ref_max[c], -sgn)
            w = jnp.concatenate([w_ref[side, c, 0:DYN_ROW0, :], dyn], axis=0)
            s_buf[c] = jnp.dot(k_t, w, preferred_element_type=F32).astype(BF16)

    def stage_softmax(m2, m1, s_buf, p_buf):
        alphas, m_out = [], []
        for c in range(2):
            z = s_buf[c]
            d = jnp.maximum(m1[c] - m2[c], column_max(z)).astype(BF16)
            m_new = m2[c] + d.astype(F32)
            p_buf[c] = jnp.exp2(z - d)
            alphas.append(jnp.exp2(m1[c] - m_new))
            m_out.append(m_new)
        return alphas, m_out

    def stage_pv(j, p_buf, alphas):
        kb, _ = rest_block(j)
        vt = value_cols(kb)
        for c in range(2):
            acc_ref[c] = acc_ref[c] * alphas[c] + jnp.dot(vt, p_buf[c], preferred_element_type=F32)

    nbuf = ATT_BUFFERS

    def full_step(t, t_mod, state):
        m_t3, m_t2, al_prev = state
        kb_s, side = rest_block(t)
        sgn = jnp.where(side == 1, 1.0, -1.0)
        k_t = key_rows(kb_s)
        kb_v, _ = rest_block(t - 2)
        vt = value_cols(kb_v)
        s_buf = s_bufs[t_mod]
        p_buf = p_bufs[(t_mod - 2) % nbuf]
        for c in range(2):
            dyn = reference_rows(sgn * q_origin - m_t2[c], -sgn)
            w = jnp.concatenate([w_ref[side, c, 0:DYN_ROW0, :], dyn], axis=0)
            for n in range(tq // blk):
                cols = slice(n * blk, (n + 1) * blk)
                s_buf[c, :, cols] = jnp.dot(k_t, w[:, cols], preferred_element_type=F32).astype(BF16)
                acc_ref[c, :, cols] = (acc_ref[c, :, cols] * al_prev[c][:, cols]
                                       + jnp.dot(vt, p_buf[c, :, cols], preferred_element_type=F32))
        al_new, m_t1 = stage_softmax(m_t3, m_t2, s_bufs[(t_mod - 1) % nbuf], p_bufs[(t_mod - 1) % nbuf])
        return m_t2, m_t1, al_new

    stage_scores(0, m, s_bufs[0])
    stage_scores(1, m, s_bufs[1])
    al_a, m_a = stage_softmax(m, m, s_bufs[0], p_bufs[0])
    state = (m, m_a, al_a)

    n_full = n_rest - 2
    unroll = next((u for u in ATT_UNROLL_CHOICES if u <= n_full), 0)
    n_peel = n_full % unroll if unroll else n_full
    for t in range(2, 2 + n_peel):
        state = full_step(t, t % nbuf, state)
    t0 = 2 + n_peel

    def steady(u, state):
        for i in range(unroll):
            state = full_step(t0 + unroll * u + i, (t0 + i) % nbuf, state)
        return state

    if unroll:
        state = lax.fori_loop(0, (n_full - n_peel) // unroll, steady, state)

    m_t3, m_t2, al_prev = state
    last = (n_rest - 1) % nbuf
    al_last, _ = stage_softmax(m_t3, m_t2, s_bufs[last], p_bufs[last])
    stage_pv(n_rest - 2, p_bufs[(n_rest - 2) % nbuf], al_prev)
    stage_pv(n_rest - 1, p_bufs[last], al_last)

    a0 = acc_ref[0]
    a1 = acc_ref[1]
    o0 = a0[0:DIFF_V_DIM] / a0[DIFF_V_DIM:DIFF_V_DIM + 1]
    o1 = a1[0:DIFF_V_DIM] / a1[DIFF_V_DIM:DIFF_V_DIM + 1]
    lam = lam_ref[0:1, 0:1]
    o = o0 - lam * o1
    ms = jnp.mean(o * o, axis=0, keepdims=True)
    y = o * lax.rsqrt(ms + EPS) * gsub_ref[...] * out_scale
    o_ref[0] = y.astype(o_ref.dtype)


def _diff_attn_call(slope2, qt, kblk, vtblk, qaug, diag, lam_tile, gsub_b, out_scale):
    b, nh, kdim, s = qt.shape
    nkb = kblk.shape[2]
    blk = ATT_BLOCK
    tq = ATT_QBLOCK
    grid = (b, nh, s // tq)
    in_specs = [
        pl.BlockSpec(memory_space=pltpu.SMEM),
        pl.BlockSpec((1, 1, kdim, tq), lambda bi, hi, qi: (bi, hi, 0, qi)),
        pl.BlockSpec((1, 1, nkb, blk, kdim), lambda bi, hi, qi: (bi, hi, 0, 0, 0)),
        pl.BlockSpec((1, 1, nkb, VT_ROWS, blk), lambda bi, hi, qi: (bi, hi, 0, 0, 0)),
        pl.BlockSpec((1, kdim, tq), lambda bi, hi, qi: (hi, 0, 0)),
        pl.BlockSpec((1, tq, tq), lambda bi, hi, qi: (hi, 0, 0)),
        _const_spec(lam_tile.shape),
        _const_spec(gsub_b.shape),
    ]
    s_buf = pltpu.VMEM((2, tq, tq), BF16)
    p_buf = pltpu.VMEM((2, tq, tq), BF16)
    return pl.pallas_call(
        functools.partial(_diff_attn_kernel, out_scale=out_scale),
        grid=grid, in_specs=in_specs,
        out_specs=pl.BlockSpec((1, DIFF_V_DIM, tq), lambda bi, hi, qi: (bi, hi, qi)),
        out_shape=jax.ShapeDtypeStruct((b, nh * DIFF_V_DIM, s), BF16),
        scratch_shapes=[pltpu.VMEM((2, 2, kdim, tq), BF16),
                        *([s_buf] * ATT_BUFFERS), *([p_buf] * ATT_BUFFERS),
                        pltpu.VMEM((2, VT_ROWS, tq), F32)],
        compiler_params=_params(("arbitrary", "arbitrary", "arbitrary")),
        name="diff_attn",
    )(slope2, qt, kblk, vtblk, qaug, diag, lam_tile, gsub_b)


def _swa_kernel(sink_ref, qt_ref, kp_ref, kc_ref, kn_ref, vp_ref, vc_ref, vn_ref, bias_ref, o_ref, *, seq_len):
    qi = pl.program_id(1)
    blk = SWA_BLOCK
    nsub = SWA_TILE // blk
    k_all = jnp.concatenate([kp_ref[0], kc_ref[0], kn_ref[0]], axis=0)
    vt_all = jnp.concatenate([vp_ref[0], vc_ref[0], vn_ref[0]], axis=1)
    key_row = lax.broadcasted_iota(jnp.int32, (3 * blk, SWA_HEADS * blk), 0)
    no_q = jnp.zeros((HEAD_DIM, SWA_GROUP * blk), BF16)
    sink = sink_ref[...]
    gw = SWA_GROUP * blk
    for sub in range(nsub):
        key_pos = (qi * nsub + sub - 1) * blk + key_row
        in_seq = (key_pos >= 0) & (key_pos < seq_len)
        kk = k_all[sub * blk:(sub + 3) * blk]
        q_t = [qt_ref[0, hq * HEAD_DIM:(hq + 1) * HEAD_DIM, sub * blk:(sub + 1) * blk] for hq in range(SWA_HEADS)]
        w = jnp.concatenate([jnp.concatenate(q_t[:SWA_GROUP] + [no_q], axis=1),
                             jnp.concatenate([no_q] + q_t[SWA_GROUP:], axis=1)], axis=0)
        sc = _dot(kk, w) + bias_ref[...]
        sc = jnp.where(in_seq, sc, NEG_INF)
        m = jnp.maximum(jnp.max(sc, axis=0, keepdims=True), sink)
        e = jnp.exp(sc - m).astype(BF16)
        tail = jnp.exp(sink - m)
        for kh in range(SWA_KV_HEADS):
            vt = vt_all[kh * VT_ROWS:(kh + 1) * VT_ROWS, sub * blk:(sub + 3) * blk]
            acc = _dot(vt, e[:, kh * gw:(kh + 1) * gw])
            o = acc[0:HEAD_DIM] / (acc[HEAD_DIM:HEAD_DIM + 1] + tail[:, kh * gw:(kh + 1) * gw])
            for g in range(SWA_GROUP):
                hq = kh * SWA_GROUP + g
                o_ref[0, hq * HEAD_DIM:(hq + 1) * HEAD_DIM, sub * blk:(sub + 1) * blk] = (
                    o[:, g * blk:(g + 1) * blk].astype(o_ref.dtype))


def _swa_call(sinks, cqt, ck, cvt, bias):
    b, _, s = cqt.shape
    blk = SWA_BLOCK
    tile = SWA_TILE
    nsub = tile // blk
    nb = s // blk
    prev_i = lambda qi: jnp.maximum(qi * nsub - 1, 0)
    next_i = lambda qi: jnp.minimum((qi + 1) * nsub, nb - 1)
    vrows = SWA_KV_HEADS * VT_ROWS
    in_specs = [
        _const_spec(sinks.shape),
        pl.BlockSpec((1, SWA_WIDTH, tile), lambda bi, qi: (bi, 0, qi)),
        pl.BlockSpec((1, blk, 128), lambda bi, qi: (bi, prev_i(qi), 0)),
        pl.BlockSpec((1, tile, 128), lambda bi, qi: (bi, qi, 0)),
        pl.BlockSpec((1, blk, 128), lambda bi, qi: (bi, next_i(qi), 0)),
        pl.BlockSpec((1, vrows, blk), lambda bi, qi: (bi, 0, prev_i(qi))),
        pl.BlockSpec((1, vrows, tile), lambda bi, qi: (bi, 0, qi)),
        pl.BlockSpec((1, vrows, blk), lambda bi, qi: (bi, 0, next_i(qi))),
        _const_spec(bias.shape),
    ]
    return pl.pallas_call(
        functools.partial(_swa_kernel, seq_len=s),
        grid=(b, s // tile), in_specs=in_specs,
        out_specs=pl.BlockSpec((1, SWA_WIDTH, tile), lambda bi, qi: (bi, 0, qi)),
        out_shape=jax.ShapeDtypeStruct((b, SWA_WIDTH, s), BF16),
        compiler_params=_params(("arbitrary", "arbitrary")),
        name="swa",
    )(sinks, cqt, ck, ck, ck, cvt, cvt, cvt, bias)


def _mix_out_kernel(x_ref, yat_ref, bu_ref, bv_ref, yct_ref, lng_ref, lnb_ref, ws_ref, bs_ref,
                    wa_ref, wb_ref, wc_ref, g_ref, o_ref):
    tn = x_ref.shape[1]
    v = bv_ref[0]
    mu = jnp.mean(v, axis=-1, keepdims=True)
    var = jnp.mean(jnp.square(v - mu), axis=-1, keepdims=True)
    vn = ((v - mu) * lax.rsqrt(var + EPS) * lng_ref[...] + lnb_ref[...]).astype(BF16)
    lane_group = lax.broadcasted_iota(jnp.int32, (CHUNK, GMLP_WIDTH), 1) // GMLP_GROUP_DIM
    u = bu_ref[0]
    yb = []
    for c in range(tn // CHUNK):
        vc = vn[c * CHUNK:(c + 1) * CHUNK]
        mixed = bs_ref[...]
        for g in range(GMLP_GROUPS):
            mixed = mixed + jnp.where(lane_group == g, _dot(ws_ref[g], vc), 0.0)
        yb.append(u[c * CHUNK:(c + 1) * CHUNK] * mixed)
    yb = jnp.concatenate(yb, axis=0).astype(BF16)
    y = _dot_tn(yat_ref[0], wa_ref[...]) + _dot(yb, wb_ref[...]) + _dot_tn(yct_ref[0], wc_ref[...])
    o_ref[0] = x_ref[0] + _rms(y, g_ref[...])


def _mix_out_call(x, yat, bu, bv, yc, lng, lnb, ws, bs, wa, wb, wc, g):
    b, s, d = x.shape
    tn = TOKEN_TILE
    tok = lambda bi, si: (bi, si, 0)
    in_specs = [
        pl.BlockSpec((1, tn, d), tok),
        pl.BlockSpec((1, DIFF_WIDTH, tn), lambda bi, si: (bi, 0, si)),
        pl.BlockSpec((1, tn, GMLP_WIDTH), tok),
        pl.BlockSpec((1, tn, GMLP_WIDTH), tok),
        pl.BlockSpec((1, SWA_WIDTH, tn), lambda bi, si: (bi, 0, si)),
    ] + [_const_spec(a.shape) for a in (lng, lnb, ws, bs, wa, wb, wc, g)]
    return pl.pallas_call(
        _mix_out_kernel,
        grid=(b, s // tn), in_specs=in_specs,
        out_specs=pl.BlockSpec((1, tn, d), tok),
        out_shape=jax.ShapeDtypeStruct((b, s, d), F32),
        compiler_params=_params(("arbitrary", "arbitrary")),
        name="mix_out",
    )(x, yat, bu, bv, yc, lng, lnb, ws, bs, wa, wb, wc, g)


def _sigmoid(z):
    return 1.0 / (1.0 + jnp.exp(-z))


def _ffn_ple_kernel(x_ref, p_ref, gpre_ref, wg_ref, wu_ref, wo_ref, gpost_ref,
                    wup_ref, wgate_ref, ggate_ref, gple_ref, o_ref):
    x = x_ref[0]
    h = _rms(x, gpre_ref[...]).astype(BF16)
    f = jnp.zeros(x.shape, F32)
    for j in range(D_FF // FF_CHUNK):
        sl = slice(j * FF_CHUNK, (j + 1) * FF_CHUNK)
        gate = _dot(h, wg_ref[:, sl])
        up = _dot(h, wu_ref[:, sl])
        a = (gate * _sigmoid(gate) * up).astype(BF16)
        f = f + _dot(a, wo_ref[sl, :])
    x = x + _rms(f, gpost_ref[...])
    e = _dot(p_ref[0, 0].astype(BF16), wup_ref[...])
    gt = _sigmoid(_dot(_rms(x, ggate_ref[...]).astype(BF16), wgate_ref[...]))
    o_ref[0] = x + _rms(e * gt, gple_ref[...])


def _ffn_ple_call(x, p, layer, gpre, wg, wu, wo, gpost, wup, wgate, ggate, gple):
    b, s, d = x.shape
    tn = TOKEN_TILE
    tok = lambda bi, si: (bi, si, 0)
    in_specs = [
        pl.BlockSpec((1, tn, d), tok),
        pl.BlockSpec((1, 1, tn, PLE_DIM), lambda bi, si: (layer, bi, si, 0)),
    ] + [_const_spec(a.shape) for a in (gpre, wg, wu, wo, gpost, wup, wgate, ggate, gple)]
    return pl.pallas_call(
        _ffn_ple_kernel,
        grid=(b, s // tn), in_specs=in_specs,
        out_specs=pl.BlockSpec((1, tn, d), tok),
        out_shape=jax.ShapeDtypeStruct((b, s, d), F32),
        compiler_params=_params(("arbitrary", "arbitrary")),
        name="ffn_ple",
    )(x, p, gpre, wg, wu, wo, gpost, wup, wgate, ggate, gple)


def _prep_in_weights(w):
    d = w.shape[0]
    aq = w[:, 0:384].reshape(d, DIFF_HEADS, 2, DIFF_QK_DIM)
    ak = w[:, 384:768].reshape(d, DIFF_HEADS, 2, DIFF_QK_DIM)
    av = w[:, 768:1152].reshape(d, DIFF_HEADS, DIFF_V_DIM)
    pad_qk = ((0, 0), (0, 0), (0, 0), (0, 64 - DIFF_QK_DIM))
    wk = jnp.pad(ak, pad_qk).reshape(d, DIFF_HEADS * KDIM)
    wq = jnp.pad(aq, pad_qk).reshape(d, DIFF_HEADS * KDIM)
    wv = jnp.pad(av, ((0, 0), (0, 0), (0, VT_ROWS - DIFF_V_DIM))).reshape(d, DIFF_HEADS * VT_ROWS)
    cq = w[:, 1664:2048] * (HEAD_DIM ** -0.5)
    cv = w[:, 2176:2304].reshape(d, SWA_KV_HEADS, HEAD_DIM)
    cv = jnp.pad(cv, ((0, 0), (0, 0), (0, VT_ROWS - HEAD_DIM))).reshape(d, SWA_KV_HEADS * VT_ROWS)
    wstd = jnp.concatenate([wk, w[:, 1152:1664], w[:, 2048:2176]], axis=1).astype(BF16)
    wtr = jnp.concatenate([wq, wv, cq, cv], axis=1).T.astype(BF16)
    return wstd, wtr


def kernel(x, p, g_pre_mix, w_in, lam_q1, lam_k1, lam_q2, lam_k2, g_diff_sub, gmlp_ln_g, gmlp_ln_b,
           w_spatial, b_spatial, swa_sinks, w_out, g_post_mix, g_pre_ffn, w_ffn_in, w_ffn_out,
           g_post_ffn, w_ple_up, w_ple_gate, g_ple_gate, g_ple_post):
    b, s, d = x.shape
    depth = w_in.shape[0]
    assert d == D_MODEL and s % (2 * ATT_QBLOCK) == 0 and s >= 4 * ATT_QBLOCK and s % TOKEN_TILE == 0

    slope2_np, qaug_np, kaug_np, ktab_np, diag_np = _diff_bias_constants(TOKEN_TILE, s)
    _, swa_slopes_np = _alibi_slopes_np()
    slope2 = jnp.asarray(slope2_np)
    qaug = jnp.asarray(qaug_np)
    kaug = jnp.asarray(kaug_np)
    ktab = jnp.asarray(ktab_np)
    diag = jnp.asarray(diag_np)
    swa_bias = jnp.asarray(_swa_bias_np(swa_slopes_np))
    row = lambda a: a.reshape(1, -1).astype(F32)

    for l in range(depth):
        lam_init = 0.8 - 0.6 * math.exp(-0.3 * l)
        wstd, wtr = _prep_in_weights(w_in[l])
        lamp = jnp.stack([lam_q1[l], lam_k1[l], lam_q2[l], lam_k2[l]]).astype(F32)
        kblk, qt, vtblk, bu, bv, cq, ck, cv, lam_tile = _proj_call(
            x, row(g_pre_mix[l]), wstd, wtr, kaug, ktab, lamp, lam_init)

        gsub_b = jnp.broadcast_to(g_diff_sub[l].astype(F32)[:, None], (DIFF_V_DIM, ATT_QBLOCK))
        yat = _diff_attn_call(slope2, qt, kblk, vtblk, qaug, diag, lam_tile, gsub_b, 1.0 - lam_init)
        sink_row = jnp.repeat(swa_sinks[l].astype(F32), SWA_BLOCK).reshape(1, SWA_HEADS * SWA_BLOCK)
        yc = _swa_call(sink_row, cq, ck, cv, swa_bias)

        bs = jnp.broadcast_to(b_spatial[l].T[:, :, None], (CHUNK, GMLP_GROUPS, GMLP_GROUP_DIM))
        bs = bs.reshape(CHUNK, GMLP_WIDTH).astype(F32)
        wo = w_out[l].astype(BF16)
        x = _mix_out_call(
            x, yat, bu, bv, yc, row(gmlp_ln_g[l]), row(gmlp_ln_b[l]), w_spatial[l].astype(BF16), bs,
            wo[0:DIFF_WIDTH], wo[DIFF_WIDTH:DIFF_WIDTH + GMLP_WIDTH], wo[DIFF_WIDTH + GMLP_WIDTH:],
            row(g_post_mix[l]))

        wfi = w_ffn_in[l].astype(BF16)
        x = _ffn_ple_call(
            x, p, l, row(g_pre_ffn[l]), wfi[:, :D_FF], wfi[:, D_FF:], w_ffn_out[l].astype(BF16),
            row(g_post_ffn[l]), w_ple_up[l].astype(BF16), w_ple_gate[l].astype(BF16),
            row(g_ple_gate[l]), row(g_ple_post[l]))
    return x
```

```python
import functools
import math

import numpy as np
import jax
import jax.numpy as jnp
from jax import lax
from jax.experimental import pallas as pl
from jax.experimental.pallas import tpu as pltpu

D_MODEL = 1024
HEAD_DIM = 64
DIFF_HEADS = 6
DIFF_QK_DIM = 32
DIFF_V_DIM = 64
DIFF_WIDTH = DIFF_HEADS * DIFF_V_DIM
GMLP_GROUPS = 4
GMLP_GROUP_DIM = 64
GMLP_WIDTH = GMLP_GROUPS * GMLP_GROUP_DIM
CHUNK = 128
SWA_HEADS = 6
SWA_KV_HEADS = 2
SWA_GROUP = SWA_HEADS // SWA_KV_HEADS
SWA_WIDTH = SWA_HEADS * HEAD_DIM
WINDOW = 128
SWA_BLOCK = 128
SWA_TILE = 512
D_FF = 2816
PLE_DIM = 256
N_ATTN_HEADS = DIFF_HEADS + SWA_HEADS
ALIBI_MAX_EXP = 8.0
EPS = 1e-6
NEG_INF = -1e30
LOG2E = 1.4426950408889634

LANES = 128
MXU_DIM_V7X = 256
VMEM_LIMIT_BYTES_V7X = 56 * 1024 * 1024

TOKEN_TILE = 512
ATT_BLOCK = MXU_DIM_V7X
ATT_QBLOCK = 2 * ATT_BLOCK
KDIM = 2 * 64
AUG_OFF = DIFF_QK_DIM
DYN_ROW0 = 112
DYN_ROWS = KDIM - DYN_ROW0
VT_ROWS = 80
FF_CHUNK = 256

F32 = jnp.float32
BF16 = jnp.bfloat16


def _bf16_round_np(x):
    u = np.asarray(x, np.float32).view(np.uint32).astype(np.uint64)
    r = ((u >> 16) & 1) + 0x7FFF
    return ((u + r) & 0xFFFF0000).astype(np.uint32).view(np.float32)


def _alibi_slopes_np():
    k = np.arange(1, N_ATTN_HEADS + 1, dtype=np.float64)
    s = np.exp2(-ALIBI_MAX_EXP * k / N_ATTN_HEADS).astype(np.float32)
    return s[SWA_HEADS:], s[:SWA_HEADS]


def _split3_bf16(v):
    v = np.asarray(v, np.float32)
    hi = _bf16_round_np(v)
    mid = _bf16_round_np(v - hi)
    lo = _bf16_round_np(v - hi - mid)
    return hi, mid, lo


def _diff_bias_constants(token_tile, seq_len):
    slopes, _ = _alibi_slopes_np()
    slope2 = (slopes.astype(np.float64) * LOG2E).astype(np.float32)
    hi, mid, lo = _split3_bf16(slope2)
    parts = np.stack([hi, mid, lo], axis=1)
    rel = np.arange(ATT_BLOCK, dtype=np.float32)
    qaug = np.zeros((DIFF_HEADS, KDIM, ATT_BLOCK), np.float32)
    kaug = np.zeros((DIFF_HEADS, ATT_BLOCK, KDIM), np.float32)
    for c in range(2):
        base = c * 64 + AUG_OFF
        for t in range(3):
            qaug[:, base + t, :] = rel[None, :]
            qaug[:, base + 3 + t, :] = -parts[:, t][:, None]
            kaug[:, :, base + t] = parts[:, t][:, None]
            kaug[:, :, base + 3 + t] = rel[None, :]
    kaug[:, :, DYN_ROW0:DYN_ROW0 + 3] = 1.0
    kaug = np.tile(kaug, (1, token_tile // ATT_BLOCK, 1))
    qaug = np.tile(qaug, (1, 1, ATT_QBLOCK // ATT_BLOCK))
    n_kb = seq_len // ATT_BLOCK
    sigma_j = (slope2[:, None] * np.float32(ATT_BLOCK)) * np.arange(n_kb, dtype=np.float32)[None, :]
    ktab = np.zeros((DIFF_HEADS, n_kb, 1, KDIM), np.float32)
    for t, piece in enumerate(_split3_bf16(sigma_j)):
        ktab[:, :, 0, DYN_ROW0 + 3 + t] = piece
    pos = np.arange(ATT_QBLOCK, dtype=np.float32)
    dist = np.abs(pos[:, None] - pos[None, :])
    diag = -(slope2[:, None, None] * dist[None])
    return slope2, qaug, kaug, ktab, diag.astype(np.float32)


def _swa_bias_np(slopes):
    key = np.arange(3 * SWA_BLOCK, dtype=np.float32)[:, None]
    qry = np.arange(SWA_BLOCK, dtype=np.float32)[None, :]
    dist = np.abs(key - SWA_BLOCK - qry)
    bias = -(slopes.astype(np.float32)[:, None, None] * dist[None])
    bias = np.where(dist[None] <= WINDOW, bias, np.float32(NEG_INF)).astype(np.float32)
    return np.concatenate(list(bias), axis=1)


def _rms(x, g):
    return x * lax.rsqrt(jnp.mean(x * x, axis=-1, keepdims=True) + EPS) * g


def _dot(a, b):
    return jnp.dot(a, b, preferred_element_type=F32)


def _dot_nt(a, b):
    return lax.dot_general(a, b, (((1,), (1,)), ((), ())), preferred_element_type=F32)


def _dot_tn(a, b):
    return lax.dot_general(a, b, (((0,), (0,)), ((), ())), preferred_element_type=F32)


def _const_spec(shape):
    nd = len(shape)
    return pl.BlockSpec(shape, lambda *_: (0,) * nd, pipeline_mode=pl.Buffered(1))


def _params(sem, flags=None):
    return pltpu.CompilerParams(dimension_semantics=sem, vmem_limit_bytes=VMEM_LIMIT_BYTES_V7X, flags=flags)


N_STD = DIFF_HEADS * KDIM + 2 * GMLP_WIDTH + SWA_KV_HEADS * HEAD_DIM
N_TR = DIFF_HEADS * KDIM + DIFF_HEADS * VT_ROWS + SWA_WIDTH + SWA_KV_HEADS * VT_ROWS
Q_SCALE = (DIFF_QK_DIM ** -0.5) * LOG2E


def _proj_kernel(x_ref, g_ref, wstd_ref, wtr_ref, kaug_ref, ktab_ref, lamp_ref,
                 k_ref, qt_ref, vt_ref, bu_ref, bv_ref, cqt_ref, ck_ref, cvt_ref, lam_ref, *, lam_init):
    tn = x_ref.shape[1]
    nsub = tn // ATT_BLOCK
    h = _rms(x_ref[0], g_ref[...]).astype(BF16)
    r1 = _dot(h, wstd_ref[...])
    for hh in range(DIFF_HEADS):
        kk = r1[:, hh * KDIM:(hh + 1) * KDIM] + kaug_ref[hh]
        for j in range(nsub):
            k_ref[0, hh, j] = (kk[j * ATT_BLOCK:(j + 1) * ATT_BLOCK] + ktab_ref[hh, j]).astype(BF16)
    o = DIFF_HEADS * KDIM
    bu_ref[0] = r1[:, o:o + GMLP_WIDTH]
    o += GMLP_WIDTH
    bv_ref[0] = r1[:, o:o + GMLP_WIDTH]
    o += GMLP_WIDTH
    ck_ref[0] = r1[:, o:o + 128].astype(BF16)

    r2 = _dot_nt(wtr_ref[...], h)
    ones_row = jnp.where(lax.broadcasted_iota(jnp.int32, (VT_ROWS, ATT_BLOCK), 0) == DIFF_V_DIM, 1.0, 0.0)
    vo = DIFF_HEADS * KDIM
    for hh in range(DIFF_HEADS):
        qt_ref[0, hh] = (r2[hh * KDIM:(hh + 1) * KDIM] * Q_SCALE).astype(BF16)
        vv = r2[vo + hh * VT_ROWS: vo + (hh + 1) * VT_ROWS]
        for j in range(nsub):
            vt_ref[0, hh, j] = (vv[:, j * ATT_BLOCK:(j + 1) * ATT_BLOCK] + ones_row).astype(BF16)
    o = vo + DIFF_HEADS * VT_ROWS
    cqt_ref[0] = r2[o:o + SWA_WIDTH].astype(BF16)
    o += SWA_WIDTH
    ones_rows = jnp.where(lax.broadcasted_iota(jnp.int32, (SWA_KV_HEADS * VT_ROWS, tn), 0) % VT_ROWS == HEAD_DIM,
                          1.0, 0.0)
    cvt_ref[0] = (r2[o:o + SWA_KV_HEADS * VT_ROWS] + ones_rows).astype(BF16)

    lp = lamp_ref[...]
    s1 = jnp.sum(lp[0:1] * lp[1:2], axis=-1, keepdims=True)
    s2 = jnp.sum(lp[2:3] * lp[3:4], axis=-1, keepdims=True)
    lam = jnp.exp(s1) - jnp.exp(s2) + lam_init
    lam_ref[...] = jnp.broadcast_to(lam, lam_ref.shape)


def _proj_call(x, g, wstd, wtr, kaug, ktab, lamp, lam_init):
    b, s, d = x.shape
    tn = TOKEN_TILE
    nsub = tn // ATT_BLOCK
    nkb = s // ATT_BLOCK
    grid = (b, s // tn)
    tok = lambda bi, si: (bi, si, 0)
    out_shape = (
        jax.ShapeDtypeStruct((b, DIFF_HEADS, nkb, ATT_BLOCK, KDIM), BF16),
        jax.ShapeDtypeStruct((b, DIFF_HEADS, KDIM, s), BF16),
        jax.ShapeDtypeStruct((b, DIFF_HEADS, nkb, VT_ROWS, ATT_BLOCK), BF16),
        jax.ShapeDtypeStruct((b, s, GMLP_WIDTH), F32),
        jax.ShapeDtypeStruct((b, s, GMLP_WIDTH), F32),
        jax.ShapeDtypeStruct((b, SWA_WIDTH, s), BF16),
        jax.ShapeDtypeStruct((b, s, 128), BF16),
        jax.ShapeDtypeStruct((b, SWA_KV_HEADS * VT_ROWS, s), BF16),
        jax.ShapeDtypeStruct((8, LANES), F32),
    )
    out_specs = (
        pl.BlockSpec((1, DIFF_HEADS, nsub, ATT_BLOCK, KDIM), lambda bi, si: (bi, 0, si, 0, 0)),
        pl.BlockSpec((1, DIFF_HEADS, KDIM, tn), lambda bi, si: (bi, 0, 0, si)),
        pl.BlockSpec((1, DIFF_HEADS, nsub, VT_ROWS, ATT_BLOCK), lambda bi, si: (bi, 0, si, 0, 0)),
        pl.BlockSpec((1, tn, GMLP_WIDTH), tok),
        pl.BlockSpec((1, tn, GMLP_WIDTH), tok),
        pl.BlockSpec((1, SWA_WIDTH, tn), lambda bi, si: (bi, 0, si)),
        pl.BlockSpec((1, tn, 128), tok),
        pl.BlockSpec((1, SWA_KV_HEADS * VT_ROWS, tn), lambda bi, si: (bi, 0, si)),
        pl.BlockSpec((8, LANES), lambda bi, si: (0, 0)),
    )
    in_specs = [
        pl.BlockSpec((1, tn, d), tok),
        _const_spec(g.shape),
        _const_spec(wstd.shape),
        _const_spec(wtr.shape),
        _const_spec(kaug.shape),
        pl.BlockSpec((DIFF_HEADS, nsub, 1, KDIM), lambda bi, si: (0, si, 0, 0)),
        _const_spec(lamp.shape),
    ]
    return pl.pallas_call(
        functools.partial(_proj_kernel, lam_init=lam_init),
        grid=grid, in_specs=in_specs, out_specs=out_specs, out_shape=out_shape,
        compiler_params=_params(("arbitrary", "arbitrary")),
        name="proj",
    )(x, g, wstd, wtr, kaug, ktab, lamp)


def _diff_attn_kernel(slope_ref, qt_ref, k_ref, vt_ref, qaug_ref, diag_ref, lam_ref, gsub_ref,
                      o_ref, w_ref, s0_ref, s1_ref, p0_ref, p1_ref, acc_ref, *, out_scale):
    hh = pl.program_id(1)
    qi = pl.program_id(2)
    blk = ATT_BLOCK
    tq = ATT_QBLOCK
    n_rest = k_ref.shape[2] // 2 - 1
    sigma = slope_ref[hh] * float(blk)

    qt = qt_ref[0, 0].astype(F32)
    qaug = qaug_ref[0]
    row = lax.broadcasted_iota(jnp.int32, (KDIM, tq), 0)
    comp_mask = (row < 64, row >= 64)
    zero = jnp.zeros_like(qt)
    for c in range(2):
        w_ref[0, c] = jnp.where(comp_mask[c], qt - qaug, zero).astype(BF16)
        w_ref[1, c] = jnp.where(comp_mask[c], qt + qaug, zero).astype(BF16)

    def key_rows(kb):
        return jnp.concatenate([k_ref[0, 0, 2 * kb], k_ref[0, 0, 2 * kb + 1]], axis=0)

    def value_cols(kb):
        return jnp.concatenate([vt_ref[0, 0, 2 * kb], vt_ref[0, 0, 2 * kb + 1]], axis=1)

    k_t = key_rows(qi)
    vt = value_cols(qi)
    m = []
    for c in range(2):
        s = _dot(k_t, jnp.where(comp_mask[c], qt, zero).astype(BF16)) + diag_ref[0]
        m_c = jnp.max(s, axis=0, keepdims=True)
        acc_ref[c] = _dot(vt, jnp.exp2((s - m_c).astype(BF16)))
        m.append(m_c)

    dyn_row = lax.broadcasted_iota(jnp.int32, (DYN_ROWS, tq), 0)
    lane_row = lax.broadcasted_iota(jnp.int32, (1, tq), 1)
    q_origin = sigma * jnp.where(lane_row < blk, 2 * qi, 2 * qi + 1).astype(F32)

    def rest_block(j):
        after = (j >= qi).astype(jnp.int32)
        return j + after, after

    def reference_rows(rr, key_sign):
        hi = rr.astype(BF16).astype(F32)
        mid = (rr - hi).astype(BF16).astype(F32)
        lo = rr - hi - mid
        rows = jnp.where(dyn_row == 0, hi, jnp.where(dyn_row == 1, mid, jnp.where(dyn_row == 2, lo, 0.0)))
        rows = jnp.where((dyn_row >= 3) & (dyn_row < 6), key_sign, rows)
        return rows.astype(BF16)

    def column_max(z):
        parts = [z[i * 16:(i + 1) * 16] for i in range(z.shape[0] // 16)]
        while len(parts) > 1:
            parts = [jnp.maximum(parts[i], parts[i + 1]) for i in range(0, len(parts), 2)]
        return jnp.max(parts[0].astype(F32), axis=0, keepdims=True)

    tiles = [slice(n * blk, (n + 1) * blk) for n in range(tq // blk)]
    s_bufs = (s0_ref, s1_ref)
    p_bufs = (p0_ref, p1_ref)
    alphas = {}
    maxes = {-2: m, -1: m}
    for t in range(n_rest + 2):
        do_scores = t < n_rest
        do_pv = t >= 2
        if do_scores:
            kb_s, side = rest_block(t)
            sgn = jnp.where(side == 1, 1.0, -1.0)
            k_t = key_rows(kb_s)
        if do_pv:
            kb_v, _ = rest_block(t - 2)
            vt = value_cols(kb_v)
        for c in range(2):
            if do_scores:
                dyn = reference_rows(sgn * q_origin - maxes[t - 2][c], -sgn)
                w = jnp.concatenate([w_ref[side, c, 0:DYN_ROW0, :], dyn], axis=0)
            for n, cols in enumerate(tiles):
                if do_scores:
                    s_bufs[t % 2][c, :, cols] = jnp.dot(k_t, w[:, cols], preferred_element_type=F32).astype(BF16)
                if do_pv:
                    acc_ref[c, :, cols] = (acc_ref[c, :, cols] * alphas[t - 2][c][:, cols]
                                           + jnp.dot(vt, p_bufs[t % 2][c, :, cols], preferred_element_type=F32))
        u = t - 1
        if 0 <= u < n_rest:
            m2, m1 = maxes[u - 2], maxes[u - 1]
            al_u, m_u = [], []
            for c in range(2):
                z = s_bufs[u % 2][c]
                d = jnp.maximum(m1[c] - m2[c], column_max(z)).astype(BF16)
                m_new = m2[c] + d.astype(F32)
                p_bufs[u % 2][c] = jnp.exp2(z - d)
                al_u.append(jnp.exp2(m1[c] - m_new))
                m_u.append(m_new)
            alphas[u], maxes[u] = al_u, m_u

    a0 = acc_ref[0]
    a1 = acc_ref[1]
    o0 = a0[0:DIFF_V_DIM] / a0[DIFF_V_DIM:DIFF_V_DIM + 1]
    o1 = a1[0:DIFF_V_DIM] / a1[DIFF_V_DIM:DIFF_V_DIM + 1]
    lam = lam_ref[0:1, 0:1]
    o = o0 - lam * o1
    ms = jnp.mean(o * o, axis=0, keepdims=True)
    y = o * lax.rsqrt(ms + EPS) * gsub_ref[...] * out_scale
    o_ref[0] = y.astype(o_ref.dtype)


def _diff_attn_call(slope2, qt, kblk, vtblk, qaug, diag, lam_tile, gsub_b, out_scale):
    b, nh, kdim, s = qt.shape
    nkb = kblk.shape[2]
    blk = ATT_BLOCK
    tq = ATT_QBLOCK
    grid = (b, nh, s // tq)
    in_specs = [
        pl.BlockSpec(memory_space=pltpu.SMEM),
        pl.BlockSpec((1, 1, kdim, tq), lambda bi, hi, qi: (bi, hi, 0, qi)),
        pl.BlockSpec((1, 1, nkb, blk, kdim), lambda bi, hi, qi: (bi, hi, 0, 0, 0)),
        pl.BlockSpec((1, 1, nkb, VT_ROWS, blk), lambda bi, hi, qi: (bi, hi, 0, 0, 0)),
        pl.BlockSpec((1, kdim, tq), lambda bi, hi, qi: (hi, 0, 0)),
        pl.BlockSpec((1, tq, tq), lambda bi, hi, qi: (hi, 0, 0)),
        _const_spec(lam_tile.shape),
        _const_spec(gsub_b.shape),
    ]
    return pl.pallas_call(
        functools.partial(_diff_attn_kernel, out_scale=out_scale),
        grid=grid, in_specs=in_specs,
        out_specs=pl.BlockSpec((1, DIFF_V_DIM, tq), lambda bi, hi, qi: (bi, hi, qi)),
        out_shape=jax.ShapeDtypeStruct((b, nh * DIFF_V_DIM, s), BF16),
        scratch_shapes=[pltpu.VMEM((2, 2, kdim, tq), BF16),
                        *([pltpu.VMEM((2, tq, tq), BF16)] * 4),
                        pltpu.VMEM((2, VT_ROWS, tq), F32)],
        compiler_params=_params(("arbitrary", "arbitrary", "arbitrary")),
        name="diff_attn",
    )(slope2, qt, kblk, vtblk, qaug, diag, lam_tile, gsub_b)


def _swa_kernel(sink_ref, qt_ref, kp_ref, kc_ref, kn_ref, vp_ref, vc_ref, vn_ref, bias_ref, o_ref, *, seq_len):
    qi = pl.program_id(1)
    blk = SWA_BLOCK
    nsub = SWA_TILE // blk
    k_all = jnp.concatenate([kp_ref[0], kc_ref[0], kn_ref[0]], axis=0)
    vt_all = jnp.concatenate([vp_ref[0], vc_ref[0], vn_ref[0]], axis=1)
    key_row = lax.broadcasted_iota(jnp.int32, (3 * blk, SWA_HEADS * blk), 0)
    no_q = jnp.zeros((HEAD_DIM, SWA_GROUP * blk), BF16)
    sink = sink_ref[...]
    gw = SWA_GROUP * blk
    for sub in range(nsub):
        key_pos = (qi * nsub + sub - 1) * blk + key_row
        in_seq = (key_pos >= 0) & (key_pos < seq_len)
        kk = k_all[sub * blk:(sub + 3) * blk]
        q_t = [qt_ref[0, hq * HEAD_DIM:(hq + 1) * HEAD_DIM, sub * blk:(sub + 1) * blk] for hq in range(SWA_HEADS)]
        w = jnp.concatenate([jnp.concatenate(q_t[:SWA_GROUP] + [no_q], axis=1),
                             jnp.concatenate([no_q] + q_t[SWA_GROUP:], axis=1)], axis=0)
        sc = _dot(kk, w) + bias_ref[...]
        sc = jnp.where(in_seq, sc, NEG_INF)
        m = jnp.maximum(jnp.max(sc, axis=0, keepdims=True), sink)
        e = jnp.exp(sc - m).astype(BF16)
        tail = jnp.exp(sink - m)
        for kh in range(SWA_KV_HEADS):
            vt = vt_all[kh * VT_ROWS:(kh + 1) * VT_ROWS, sub * blk:(sub + 3) * blk]
            acc = _dot(vt, e[:, kh * gw:(kh + 1) * gw])
            o = acc[0:HEAD_DIM] / (acc[HEAD_DIM:HEAD_DIM + 1] + tail[:, kh * gw:(kh + 1) * gw])
            for g in range(SWA_GROUP):
                hq = kh * SWA_GROUP + g
                o_ref[0, hq * HEAD_DIM:(hq + 1) * HEAD_DIM, sub * blk:(sub + 1) * blk] = (
                    o[:, g * blk:(g + 1) * blk].astype(o_ref.dtype))


def _swa_call(sinks, cqt, ck, cvt, bias):
    b, _, s = cqt.shape
    blk = SWA_BLOCK
    tile = SWA_TILE
    nsub = tile // blk
    nb = s // blk
    prev_i = lambda qi: jnp.maximum(qi * nsub - 1, 0)
    next_i = lambda qi: jnp.minimum((qi + 1) * nsub, nb - 1)
    vrows = SWA_KV_HEADS * VT_ROWS
    in_specs = [
        _const_spec(sinks.shape),
        pl.BlockSpec((1, SWA_WIDTH, tile), lambda bi, qi: (bi, 0, qi)),
        pl.BlockSpec((1, blk, 128), lambda bi, qi: (bi, prev_i(qi), 0)),
        pl.BlockSpec((1, tile, 128), lambda bi, qi: (bi, qi, 0)),
        pl.BlockSpec((1, blk, 128), lambda bi, qi: (bi, next_i(qi), 0)),
        pl.BlockSpec((1, vrows, blk), lambda bi, qi: (bi, 0, prev_i(qi))),
        pl.BlockSpec((1, vrows, tile), lambda bi, qi: (bi, 0, qi)),
        pl.BlockSpec((1, vrows, blk), lambda bi, qi: (bi, 0, next_i(qi))),
        _const_spec(bias.shape),
    ]
    return pl.pallas_call(
        functools.partial(_swa_kernel, seq_len=s),
        grid=(b, s // tile), in_specs=in_specs,
        out_specs=pl.BlockSpec((1, SWA_WIDTH, tile), lambda bi, qi: (bi, 0, qi)),
        out_shape=jax.ShapeDtypeStruct((b, SWA_WIDTH, s), BF16),
        compiler_params=_params(("arbitrary", "arbitrary")),
        name="swa",
    )(sinks, cqt, ck, ck, ck, cvt, cvt, cvt, bias)


def _mix_out_kernel(x_ref, yat_ref, bu_ref, bv_ref, yct_ref, lng_ref, lnb_ref, ws_ref, bs_ref,
                    wa_ref, wb_ref, wc_ref, g_ref, o_ref):
    tn = x_ref.shape[1]
    v = bv_ref[0]
    mu = jnp.mean(v, axis=-1, keepdims=True)
    var = jnp.mean(jnp.square(v - mu), axis=-1, keepdims=True)
    vn = ((v - mu) * lax.rsqrt(var + EPS) * lng_ref[...] + lnb_ref[...]).astype(BF16)
    lane_group = lax.broadcasted_iota(jnp.int32, (CHUNK, GMLP_WIDTH), 1) // GMLP_GROUP_DIM
    u = bu_ref[0]
    yb = []
    for c in range(tn // CHUNK):
        vc = vn[c * CHUNK:(c + 1) * CHUNK]
        mixed = bs_ref[...]
        for g in range(GMLP_GROUPS):
            mixed = mixed + jnp.where(lane_group == g, _dot(ws_ref[g], vc), 0.0)
        yb.append(u[c * CHUNK:(c + 1) * CHUNK] * mixed)
    yb = jnp.concatenate(yb, axis=0).astype(BF16)
    y = _dot_tn(yat_ref[0], wa_ref[...]) + _dot(yb, wb_ref[...]) + _dot_tn(yct_ref[0], wc_ref[...])
    o_ref[0] = x_ref[0] + _rms(y, g_ref[...])


def _mix_out_call(x, yat, bu, bv, yc, lng, lnb, ws, bs, wa, wb, wc, g):
    b, s, d = x.shape
    tn = TOKEN_TILE
    tok = lambda bi, si: (bi, si, 0)
    in_specs = [
        pl.BlockSpec((1, tn, d), tok),
        pl.BlockSpec((1, DIFF_WIDTH, tn), lambda bi, si: (bi, 0, si)),
        pl.BlockSpec((1, tn, GMLP_WIDTH), tok),
        pl.BlockSpec((1, tn, GMLP_WIDTH), tok),
        pl.BlockSpec((1, SWA_WIDTH, tn), lambda bi, si: (bi, 0, si)),
    ] + [_const_spec(a.shape) for a in (lng, lnb, ws, bs, wa, wb, wc, g)]
    return pl.pallas_call(
        _mix_out_kernel,
        grid=(b, s // tn), in_specs=in_specs,
        out_specs=pl.BlockSpec((1, tn, d), tok),
        out_shape=jax.ShapeDtypeStruct((b, s, d), F32),
        compiler_params=_params(("arbitrary", "arbitrary")),
        name="mix_out",
    )(x, yat, bu, bv, yc, lng, lnb, ws, bs, wa, wb, wc, g)


def _sigmoid(z):
    return 1.0 / (1.0 + jnp.exp(-z))


def _ffn_ple_kernel(x_ref, p_ref, gpre_ref, wg_ref, wu_ref, wo_ref, gpost_ref,
                    wup_ref, wgate_ref, ggate_ref, gple_ref, o_ref):
    x = x_ref[0]
    h = _rms(x, gpre_ref[...]).astype(BF16)
    f = jnp.zeros(x.shape, F32)
    for j in range(D_FF // FF_CHUNK):
        sl = slice(j * FF_CHUNK, (j + 1) * FF_CHUNK)
        gate = _dot(h, wg_ref[:, sl])
        up = _dot(h, wu_ref[:, sl])
        a = (gate * _sigmoid(gate) * up).astype(BF16)
        f = f + _dot(a, wo_ref[sl, :])
    x = x + _rms(f, gpost_ref[...])
    e = _dot(p_ref[0, 0].astype(BF16), wup_ref[...])
    gt = _sigmoid(_dot(_rms(x, ggate_ref[...]).astype(BF16), wgate_ref[...]))
    o_ref[0] = x + _rms(e * gt, gple_ref[...])


def _ffn_ple_call(x, p, layer, gpre, wg, wu, wo, gpost, wup, wgate, ggate, gple):
    b, s, d = x.shape
    tn = TOKEN_TILE
    tok = lambda bi, si: (bi, si, 0)
    in_specs = [
        pl.BlockSpec((1, tn, d), tok),
        pl.BlockSpec((1, 1, tn, PLE_DIM), lambda bi, si: (layer, bi, si, 0)),
    ] + [_const_spec(a.shape) for a in (gpre, wg, wu, wo, gpost, wup, wgate, ggate, gple)]
    return pl.pallas_call(
        _ffn_ple_kernel,
        grid=(b, s // tn), in_specs=in_specs,
        out_specs=pl.BlockSpec((1, tn, d), tok),
        out_shape=jax.ShapeDtypeStruct((b, s, d), F32),
        compiler_params=_params(("arbitrary", "arbitrary")),
        name="ffn_ple",
    )(x, p, gpre, wg, wu, wo, gpost, wup, wgate, ggate, gple)


def _prep_in_weights(w):
    d = w.shape[0]
    aq = w[:, 0:384].reshape(d, DIFF_HEADS, 2, DIFF_QK_DIM)
    ak = w[:, 384:768].reshape(d, DIFF_HEADS, 2, DIFF_QK_DIM)
    av = w[:, 768:1152].reshape(d, DIFF_HEADS, DIFF_V_DIM)
    pad_qk = ((0, 0), (0, 0), (0, 0), (0, 64 - DIFF_QK_DIM))
    wk = jnp.pad(ak, pad_qk).reshape(d, DIFF_HEADS * KDIM)
    wq = jnp.pad(aq, pad_qk).reshape(d, DIFF_HEADS * KDIM)
    wv = jnp.pad(av, ((0, 0), (0, 0), (0, VT_ROWS - DIFF_V_DIM))).reshape(d, DIFF_HEADS * VT_ROWS)
    cq = w[:, 1664:2048] * (HEAD_DIM ** -0.5)
    cv = w[:, 2176:2304].reshape(d, SWA_KV_HEADS, HEAD_DIM)
    cv = jnp.pad(cv, ((0, 0), (0, 0), (0, VT_ROWS - HEAD_DIM))).reshape(d, SWA_KV_HEADS * VT_ROWS)
    wstd = jnp.concatenate([wk, w[:, 1152:1664], w[:, 2048:2176]], axis=1).astype(BF16)
    wtr = jnp.concatenate([wq, wv, cq, cv], axis=1).T.astype(BF16)
    return wstd, wtr


def kernel(x, p, g_pre_mix, w_in, lam_q1, lam_k1, lam_q2, lam_k2, g_diff_sub, gmlp_ln_g, gmlp_ln_b,
           w_spatial, b_spatial, swa_sinks, w_out, g_post_mix, g_pre_ffn, w_ffn_in, w_ffn_out,
           g_post_ffn, w_ple_up, w_ple_gate, g_ple_gate, g_ple_post):
    b, s, d = x.shape
    depth = w_in.shape[0]
    assert d == D_MODEL and s % ATT_QBLOCK == 0 and s >= 2 * ATT_QBLOCK and s % TOKEN_TILE == 0

    slope2_np, qaug_np, kaug_np, ktab_np, diag_np = _diff_bias_constants(TOKEN_TILE, s)
    _, swa_slopes_np = _alibi_slopes_np()
    slope2 = jnp.asarray(slope2_np)
    qaug = jnp.asarray(qaug_np)
    kaug = jnp.asarray(kaug_np)
    ktab = jnp.asarray(ktab_np)
    diag = jnp.asarray(diag_np)
    swa_bias = jnp.asarray(_swa_bias_np(swa_slopes_np))
    row = lambda a: a.reshape(1, -1).astype(F32)

    for l in range(depth):
        lam_init = 0.8 - 0.6 * math.exp(-0.3 * l)
        wstd, wtr = _prep_in_weights(w_in[l])
        lamp = jnp.stack([lam_q1[l], lam_k1[l], lam_q2[l], lam_k2[l]]).astype(F32)
        kblk, qt, vtblk, bu, bv, cq, ck, cv, lam_tile = _proj_call(
            x, row(g_pre_mix[l]), wstd, wtr, kaug, ktab, lamp, lam_init)

        gsub_b = jnp.broadcast_to(g_diff_sub[l].astype(F32)[:, None], (DIFF_V_DIM, ATT_QBLOCK))
        yat = _diff_attn_call(slope2, qt, kblk, vtblk, qaug, diag, lam_tile, gsub_b, 1.0 - lam_init)
        sink_row = jnp.repeat(swa_sinks[l].astype(F32), SWA_BLOCK).reshape(1, SWA_HEADS * SWA_BLOCK)
        yc = _swa_call(sink_row, cq, ck, cv, swa_bias)

        bs = jnp.broadcast_to(b_spatial[l].T[:, :, None], (CHUNK, GMLP_GROUPS, GMLP_GROUP_DIM))
        bs = bs.reshape(CHUNK, GMLP_WIDTH).astype(F32)
        wo = w_out[l].astype(BF16)
        x = _mix_out_call(
            x, yat, bu, bv, yc, row(gmlp_ln_g[l]), row(gmlp_ln_b[l]), w_spatial[l].astype(BF16), bs,
            wo[0:DIFF_WIDTH], wo[DIFF_WIDTH:DIFF_WIDTH + GMLP_WIDTH], wo[DIFF_WIDTH + GMLP_WIDTH:],
            row(g_post_mix[l]))

        wfi = w_ffn_in[l].astype(BF16)
        x = _ffn_ple_call(
            x, p, l, row(g_pre_ffn[l]), wfi[:, :D_FF], wfi[:, D_FF:], w_ffn_out[l].astype(BF16),
            row(g_post_ffn[l]), w_ple_up[l].astype(BF16), w_ple_gate[l].astype(BF16),
            row(g_ple_gate[l]), row(g_ple_post[l]))
    return x
```

```python
import functools
import math

import numpy as np
import jax
import jax.numpy as jnp
from jax import lax
from jax.experimental import pallas as pl
from jax.experimental.pallas import tpu as pltpu

D_MODEL = 1024
HEAD_DIM = 64
DIFF_HEADS = 6
DIFF_QK_DIM = 32
DIFF_V_DIM = 64
DIFF_WIDTH = DIFF_HEADS * DIFF_V_DIM
GMLP_GROUPS = 4
GMLP_GROUP_DIM = 64
GMLP_WIDTH = GMLP_GROUPS * GMLP_GROUP_DIM
CHUNK = 128
SWA_HEADS = 6
SWA_KV_HEADS = 2
SWA_GROUP = SWA_HEADS // SWA_KV_HEADS
SWA_WIDTH = SWA_HEADS * HEAD_DIM
WINDOW = 128
SWA_BLOCK = 128
SWA_TILE = 512
D_FF = 2816
PLE_DIM = 256
N_ATTN_HEADS = DIFF_HEADS + SWA_HEADS
ALIBI_MAX_EXP = 8.0
EPS = 1e-6
NEG_INF = -1e30
LOG2E = 1.4426950408889634

LANES = 128
MXU_DIM_V7X = 256
VMEM_LIMIT_BYTES_V7X = 56 * 1024 * 1024

TOKEN_TILE = 512
ATT_BLOCK = MXU_DIM_V7X
ATT_QBLOCK = 2 * ATT_BLOCK
COMP_LANES = 64
KDIM = 2 * COMP_LANES
AUG_OFF = DIFF_QK_DIM
SWA_KV_WIDTH = SWA_KV_HEADS * HEAD_DIM
DYN_ROW0 = 112
DYN_ROWS = KDIM - DYN_ROW0
VT_ROWS = 80
FF_CHUNK = 256

F32 = jnp.float32
BF16 = jnp.bfloat16


def _bf16_round_np(x):
    u = np.asarray(x, np.float32).view(np.uint32).astype(np.uint64)
    r = ((u >> 16) & 1) + 0x7FFF
    return ((u + r) & 0xFFFF0000).astype(np.uint32).view(np.float32)


def _alibi_slopes_np():
    k = np.arange(1, N_ATTN_HEADS + 1, dtype=np.float64)
    s = np.exp2(-ALIBI_MAX_EXP * k / N_ATTN_HEADS).astype(np.float32)
    return s[SWA_HEADS:], s[:SWA_HEADS]


def _split3_bf16(v):
    v = np.asarray(v, np.float32)
    hi = _bf16_round_np(v)
    mid = _bf16_round_np(v - hi)
    lo = _bf16_round_np(v - hi - mid)
    return hi, mid, lo


def _diff_bias_constants(token_tile, seq_len):
    slopes, _ = _alibi_slopes_np()
    slope2 = (slopes.astype(np.float64) * LOG2E).astype(np.float32)
    hi, mid, lo = _split3_bf16(slope2)
    parts = np.stack([hi, mid, lo], axis=1)
    rel = np.arange(ATT_BLOCK, dtype=np.float32)
    qaug = np.zeros((DIFF_HEADS, KDIM, ATT_BLOCK), np.float32)
    kaug = np.zeros((DIFF_HEADS, ATT_BLOCK, KDIM), np.float32)
    for c in range(2):
        base = c * COMP_LANES + AUG_OFF
        for t in range(3):
            qaug[:, base + t, :] = rel[None, :]
            qaug[:, base + 3 + t, :] = -parts[:, t][:, None]
            kaug[:, :, base + t] = parts[:, t][:, None]
            kaug[:, :, base + 3 + t] = rel[None, :]
    kaug[:, :, DYN_ROW0:DYN_ROW0 + 3] = 1.0
    kaug = np.tile(kaug, (1, token_tile // ATT_BLOCK, 1))
    qaug = np.tile(qaug, (1, 1, ATT_QBLOCK // ATT_BLOCK))
    n_kb = seq_len // ATT_BLOCK
    sigma_j = (slope2[:, None] * np.float32(ATT_BLOCK)) * np.arange(n_kb, dtype=np.float32)[None, :]
    ktab = np.zeros((DIFF_HEADS, n_kb, 1, KDIM), np.float32)
    for t, piece in enumerate(_split3_bf16(sigma_j)):
        ktab[:, :, 0, DYN_ROW0 + 3 + t] = piece
    pos = np.arange(ATT_QBLOCK, dtype=np.float32)
    dist = np.abs(pos[:, None] - pos[None, :])
    diag = -(slope2[:, None, None] * dist[None])
    return slope2, qaug, kaug, ktab, diag.astype(np.float32)


def _swa_bias_np(slopes):
    key = np.arange(3 * SWA_BLOCK, dtype=np.float32)[:, None]
    qry = np.arange(SWA_BLOCK, dtype=np.float32)[None, :]
    dist = np.abs(key - SWA_BLOCK - qry)
    bias = -(slopes.astype(np.float32)[:, None, None] * dist[None])
    bias = np.where(dist[None] <= WINDOW, bias, np.float32(NEG_INF)).astype(np.float32)
    return np.concatenate(list(bias), axis=1)


def _rms(x, g):
    return x * lax.rsqrt(jnp.mean(x * x, axis=-1, keepdims=True) + EPS) * g


def _dot(a, b):
    return jnp.dot(a, b, preferred_element_type=F32)


def _dot_nt(a, b):
    return lax.dot_general(a, b, (((1,), (1,)), ((), ())), preferred_element_type=F32)


def _dot_tn(a, b):
    return lax.dot_general(a, b, (((0,), (0,)), ((), ())), preferred_element_type=F32)


def _const_spec(shape):
    nd = len(shape)
    return pl.BlockSpec(shape, lambda *_: (0,) * nd, pipeline_mode=pl.Buffered(1))


def _params(sem):
    return pltpu.CompilerParams(dimension_semantics=sem, vmem_limit_bytes=VMEM_LIMIT_BYTES_V7X)


N_STD = DIFF_HEADS * KDIM + 2 * GMLP_WIDTH + SWA_KV_HEADS * HEAD_DIM
N_TR = DIFF_HEADS * KDIM + DIFF_HEADS * VT_ROWS + SWA_WIDTH + SWA_KV_HEADS * VT_ROWS
Q_SCALE = (DIFF_QK_DIM ** -0.5) * LOG2E


def _proj_kernel(x_ref, g_ref, wstd_ref, wtr_ref, kaug_ref, ktab_ref, lamp_ref,
                 k_ref, qt_ref, vt_ref, bu_ref, bv_ref, cqt_ref, ck_ref, cvt_ref, lam_ref, *, lam_init):
    tn = x_ref.shape[1]
    nsub = tn // ATT_BLOCK
    h = _rms(x_ref[0], g_ref[...]).astype(BF16)
    r1 = _dot(h, wstd_ref[...])
    for hh in range(DIFF_HEADS):
        kk = r1[:, hh * KDIM:(hh + 1) * KDIM] + kaug_ref[hh]
        for j in range(nsub):
            k_ref[0, hh, j] = (kk[j * ATT_BLOCK:(j + 1) * ATT_BLOCK] + ktab_ref[hh, j]).astype(BF16)
    o = DIFF_HEADS * KDIM
    bu_ref[0] = r1[:, o:o + GMLP_WIDTH]
    o += GMLP_WIDTH
    bv_ref[0] = r1[:, o:o + GMLP_WIDTH]
    o += GMLP_WIDTH
    ck_ref[0] = r1[:, o:o + SWA_KV_WIDTH].astype(BF16)

    r2 = _dot_nt(wtr_ref[...], h)
    ones_row = jnp.where(lax.broadcasted_iota(jnp.int32, (VT_ROWS, ATT_BLOCK), 0) == DIFF_V_DIM, 1.0, 0.0)
    vo = DIFF_HEADS * KDIM
    for hh in range(DIFF_HEADS):
        qt_ref[0, hh] = (r2[hh * KDIM:(hh + 1) * KDIM] * Q_SCALE).astype(BF16)
        vv = r2[vo + hh * VT_ROWS: vo + (hh + 1) * VT_ROWS]
        for j in range(nsub):
            vt_ref[0, hh, j] = (vv[:, j * ATT_BLOCK:(j + 1) * ATT_BLOCK] + ones_row).astype(BF16)
    o = vo + DIFF_HEADS * VT_ROWS
    cqt_ref[0] = r2[o:o + SWA_WIDTH].astype(BF16)
    o += SWA_WIDTH
    ones_rows = jnp.where(lax.broadcasted_iota(jnp.int32, (SWA_KV_HEADS * VT_ROWS, tn), 0) % VT_ROWS == HEAD_DIM,
                          1.0, 0.0)
    cvt_ref[0] = (r2[o:o + SWA_KV_HEADS * VT_ROWS] + ones_rows).astype(BF16)

    lp = lamp_ref[...]
    s1 = jnp.sum(lp[0:1] * lp[1:2], axis=-1, keepdims=True)
    s2 = jnp.sum(lp[2:3] * lp[3:4], axis=-1, keepdims=True)
    lam = jnp.exp(s1) - jnp.exp(s2) + lam_init
    lam_ref[...] = jnp.broadcast_to(lam, lam_ref.shape)


def _proj_call(x, g, wstd, wtr, kaug, ktab, lamp, lam_init):
    b, s, d = x.shape
    tn = TOKEN_TILE
    nsub = tn // ATT_BLOCK
    nkb = s // ATT_BLOCK
    grid = (b, s // tn)
    tok = lambda bi, si: (bi, si, 0)
    out_shape = (
        jax.ShapeDtypeStruct((b, DIFF_HEADS, nkb, ATT_BLOCK, KDIM), BF16),
        jax.ShapeDtypeStruct((b, DIFF_HEADS, KDIM, s), BF16),
        jax.ShapeDtypeStruct((b, DIFF_HEADS, nkb, VT_ROWS, ATT_BLOCK), BF16),
        jax.ShapeDtypeStruct((b, s, GMLP_WIDTH), F32),
        jax.ShapeDtypeStruct((b, s, GMLP_WIDTH), F32),
        jax.ShapeDtypeStruct((b, SWA_WIDTH, s), BF16),
        jax.ShapeDtypeStruct((b, s, SWA_KV_WIDTH), BF16),
        jax.ShapeDtypeStruct((b, SWA_KV_HEADS * VT_ROWS, s), BF16),
        jax.ShapeDtypeStruct((8, LANES), F32),
    )
    out_specs = (
        pl.BlockSpec((1, DIFF_HEADS, nsub, ATT_BLOCK, KDIM), lambda bi, si: (bi, 0, si, 0, 0)),
        pl.BlockSpec((1, DIFF_HEADS, KDIM, tn), lambda bi, si: (bi, 0, 0, si)),
        pl.BlockSpec((1, DIFF_HEADS, nsub, VT_ROWS, ATT_BLOCK), lambda bi, si: (bi, 0, si, 0, 0)),
        pl.BlockSpec((1, tn, GMLP_WIDTH), tok),
        pl.BlockSpec((1, tn, GMLP_WIDTH), tok),
        pl.BlockSpec((1, SWA_WIDTH, tn), lambda bi, si: (bi, 0, si)),
        pl.BlockSpec((1, tn, SWA_KV_WIDTH), tok),
        pl.BlockSpec((1, SWA_KV_HEADS * VT_ROWS, tn), lambda bi, si: (bi, 0, si)),
        pl.BlockSpec((8, LANES), lambda bi, si: (0, 0)),
    )
    in_specs = [
        pl.BlockSpec((1, tn, d), tok),
        _const_spec(g.shape),
        _const_spec(wstd.shape),
        _const_spec(wtr.shape),
        _const_spec(kaug.shape),
        pl.BlockSpec((DIFF_HEADS, nsub, 1, KDIM), lambda bi, si: (0, si, 0, 0)),
        _const_spec(lamp.shape),
    ]
    return pl.pallas_call(
        functools.partial(_proj_kernel, lam_init=lam_init),
        grid=grid, in_specs=in_specs, out_specs=out_specs, out_shape=out_shape,
        compiler_params=_params(("arbitrary", "arbitrary")),
        name="proj",
    )(x, g, wstd, wtr, kaug, ktab, lamp)


def _diff_attn_kernel(slope_ref, qt_ref, k_ref, vt_ref, qaug_ref, diag_ref, lam_ref, gsub_ref,
                      o_ref, w_ref, s0_ref, s1_ref, p0_ref, p1_ref, acc_ref, *, out_scale):
    hh = pl.program_id(1)
    qi = pl.program_id(2)
    blk = ATT_BLOCK
    tq = ATT_QBLOCK
    n_rest = k_ref.shape[2] // 2 - 1
    sigma = slope_ref[hh] * float(blk)

    qt = qt_ref[0, 0].astype(F32)
    qaug = qaug_ref[0]
    row = lax.broadcasted_iota(jnp.int32, (KDIM, tq), 0)
    comp_mask = (row < COMP_LANES, row >= COMP_LANES)
    zero = jnp.zeros_like(qt)
    for c in range(2):
        w_ref[0, c] = jnp.where(comp_mask[c], qt - qaug, zero).astype(BF16)
        w_ref[1, c] = jnp.where(comp_mask[c], qt + qaug, zero).astype(BF16)

    def key_rows(kb):
        return jnp.concatenate([k_ref[0, 0, 2 * kb], k_ref[0, 0, 2 * kb + 1]], axis=0)

    def value_cols(kb):
        return jnp.concatenate([vt_ref[0, 0, 2 * kb], vt_ref[0, 0, 2 * kb + 1]], axis=1)

    k_t = key_rows(qi)
    vt = value_cols(qi)
    m = []
    for c in range(2):
        s = _dot(k_t, jnp.where(comp_mask[c], qt, zero).astype(BF16)) + diag_ref[0]
        m_c = jnp.max(s, axis=0, keepdims=True)
        acc_ref[c] = _dot(vt, jnp.exp2((s - m_c).astype(BF16)))
        m.append(m_c)

    dyn_row = lax.broadcasted_iota(jnp.int32, (DYN_ROWS, tq), 0)
    lane_row = lax.broadcasted_iota(jnp.int32, (1, tq), 1)
    q_origin = sigma * jnp.where(lane_row < blk, 2 * qi, 2 * qi + 1).astype(F32)

    def rest_block(j):
        after = (j >= qi).astype(jnp.int32)
        return j + after, after

    def reference_rows(rr, key_sign):
        hi = rr.astype(BF16).astype(F32)
        mid = (rr - hi).astype(BF16).astype(F32)
        lo = rr - hi - mid
        rows = jnp.where(dyn_row == 0, hi, jnp.where(dyn_row == 1, mid, jnp.where(dyn_row == 2, lo, 0.0)))
        rows = jnp.where((dyn_row >= 3) & (dyn_row < 6), key_sign, rows)
        return rows.astype(BF16)

    def column_max(z):
        parts = [z[i * 16:(i + 1) * 16] for i in range(z.shape[0] // 16)]
        while len(parts) > 1:
            parts = [jnp.maximum(parts[i], parts[i + 1]) for i in range(0, len(parts), 2)]
        return jnp.max(parts[0].astype(F32), axis=0, keepdims=True)

    tiles = [slice(n * blk, (n + 1) * blk) for n in range(tq // blk)]
    s_bufs = (s0_ref, s1_ref)
    p_bufs = (p0_ref, p1_ref)
    alphas = {}
    maxes = {-2: m, -1: m}
    for t in range(n_rest + 2):
        do_scores = t < n_rest
        do_pv = t >= 2
        if do_scores:
            kb_s, side = rest_block(t)
            sgn = jnp.where(side == 1, 1.0, -1.0)
            k_t = key_rows(kb_s)
        if do_pv:
            kb_v, _ = rest_block(t - 2)
            vt = value_cols(kb_v)
        for c in range(2):
            if do_scores:
                dyn = reference_rows(sgn * q_origin - maxes[t - 2][c], -sgn)
                w = jnp.concatenate([w_ref[side, c, 0:DYN_ROW0, :], dyn], axis=0)
            for n, cols in enumerate(tiles):
                if do_scores:
                    s_bufs[t % 2][c, :, cols] = _dot(k_t, w[:, cols]).astype(BF16)
                if do_pv:
                    acc_ref[c, :, cols] = (acc_ref[c, :, cols] * alphas[t - 2][c][:, cols]
                                           + _dot(vt, p_bufs[t % 2][c, :, cols]))
        u = t - 1
        if 0 <= u < n_rest:
            m2, m1 = maxes[u - 2], maxes[u - 1]
            al_u, m_u = [], []
            for c in range(2):
                z = s_bufs[u % 2][c]
                d = jnp.maximum(m1[c] - m2[c], column_max(z)).astype(BF16)
                m_new = m2[c] + d.astype(F32)
                p_bufs[u % 2][c] = jnp.exp2(z - d)
                al_u.append(jnp.exp2(m1[c] - m_new))
                m_u.append(m_new)
            alphas[u], maxes[u] = al_u, m_u

    a0 = acc_ref[0]
    a1 = acc_ref[1]
    o0 = a0[0:DIFF_V_DIM] / a0[DIFF_V_DIM:DIFF_V_DIM + 1]
    o1 = a1[0:DIFF_V_DIM] / a1[DIFF_V_DIM:DIFF_V_DIM + 1]
    lam = lam_ref[0:1, 0:1]
    o = o0 - lam * o1
    ms = jnp.mean(o * o, axis=0, keepdims=True)
    y = o * lax.rsqrt(ms + EPS) * gsub_ref[...] * out_scale
    o_ref[0] = y.astype(o_ref.dtype)


def _diff_attn_call(slope2, qt, kblk, vtblk, qaug, diag, lam_tile, gsub_b, out_scale):
    b, nh, kdim, s = qt.shape
    nkb = kblk.shape[2]
    blk = ATT_BLOCK
    tq = ATT_QBLOCK
    grid = (b, nh, s // tq)
    in_specs = [
        pl.BlockSpec(memory_space=pltpu.SMEM),
        pl.BlockSpec((1, 1, kdim, tq), lambda bi, hi, qi: (bi, hi, 0, qi)),
        pl.BlockSpec((1, 1, nkb, blk, kdim), lambda bi, hi, qi: (bi, hi, 0, 0, 0)),
        pl.BlockSpec((1, 1, nkb, VT_ROWS, blk), lambda bi, hi, qi: (bi, hi, 0, 0, 0)),
        pl.BlockSpec((1, kdim, tq), lambda bi, hi, qi: (hi, 0, 0)),
        pl.BlockSpec((1, tq, tq), lambda bi, hi, qi: (hi, 0, 0)),
        _const_spec(lam_tile.shape),
        _const_spec(gsub_b.shape),
    ]
    return pl.pallas_call(
        functools.partial(_diff_attn_kernel, out_scale=out_scale),
        grid=grid, in_specs=in_specs,
        out_specs=pl.BlockSpec((1, DIFF_V_DIM, tq), lambda bi, hi, qi: (bi, hi, qi)),
        out_shape=jax.ShapeDtypeStruct((b, nh * DIFF_V_DIM, s), BF16),
        scratch_shapes=[pltpu.VMEM((2, 2, kdim, tq), BF16),
                        *([pltpu.VMEM((2, tq, tq), BF16)] * 4),
                        pltpu.VMEM((2, VT_ROWS, tq), F32)],
        compiler_params=_params(("arbitrary", "arbitrary", "arbitrary")),
        name="diff_attn",
    )(slope2, qt, kblk, vtblk, qaug, diag, lam_tile, gsub_b)


def _swa_kernel(sink_ref, qt_ref, kp_ref, kc_ref, kn_ref, vp_ref, vc_ref, vn_ref, bias_ref, o_ref, *, seq_len):
    qi = pl.program_id(1)
    blk = SWA_BLOCK
    nsub = SWA_TILE // blk
    k_all = jnp.concatenate([kp_ref[0], kc_ref[0], kn_ref[0]], axis=0)
    vt_all = jnp.concatenate([vp_ref[0], vc_ref[0], vn_ref[0]], axis=1)
    key_row = lax.broadcasted_iota(jnp.int32, (3 * blk, SWA_HEADS * blk), 0)
    no_q = jnp.zeros((HEAD_DIM, SWA_GROUP * blk), BF16)
    sink = sink_ref[...]
    gw = SWA_GROUP * blk
    for sub in range(nsub):
        key_pos = (qi * nsub + sub - 1) * blk + key_row
        in_seq = (key_pos >= 0) & (key_pos < seq_len)
        kk = k_all[sub * blk:(sub + 3) * blk]
        q_t = [qt_ref[0, hq * HEAD_DIM:(hq + 1) * HEAD_DIM, sub * blk:(sub + 1) * blk] for hq in range(SWA_HEADS)]
        w = jnp.concatenate([jnp.concatenate(q_t[:SWA_GROUP] + [no_q], axis=1),
                             jnp.concatenate([no_q] + q_t[SWA_GROUP:], axis=1)], axis=0)
        sc = _dot(kk, w) + bias_ref[...]
        sc = jnp.where(in_seq, sc, NEG_INF)
        m = jnp.maximum(jnp.max(sc, axis=0, keepdims=True), sink)
        e = jnp.exp(sc - m).astype(BF16)
        tail = jnp.exp(sink - m)
        for kh in range(SWA_KV_HEADS):
            vt = vt_all[kh * VT_ROWS:(kh + 1) * VT_ROWS, sub * blk:(sub + 3) * blk]
            acc = _dot(vt, e[:, kh * gw:(kh + 1) * gw])
            o = acc[0:HEAD_DIM] / (acc[HEAD_DIM:HEAD_DIM + 1] + tail[:, kh * gw:(kh + 1) * gw])
            for g in range(SWA_GROUP):
                hq = kh * SWA_GROUP + g
                o_ref[0, hq * HEAD_DIM:(hq + 1) * HEAD_DIM, sub * blk:(sub + 1) * blk] = (
                    o[:, g * blk:(g + 1) * blk].astype(o_ref.dtype))


def _swa_call(sinks, cqt, ck, cvt, bias):
    b, _, s = cqt.shape
    blk = SWA_BLOCK
    tile = SWA_TILE
    nsub = tile // blk
    nb = s // blk
    prev_i = lambda qi: jnp.maximum(qi * nsub - 1, 0)
    next_i = lambda qi: jnp.minimum((qi + 1) * nsub, nb - 1)
    vrows = SWA_KV_HEADS * VT_ROWS
    in_specs = [
        _const_spec(sinks.shape),
        pl.BlockSpec((1, SWA_WIDTH, tile), lambda bi, qi: (bi, 0, qi)),
        pl.BlockSpec((1, blk, SWA_KV_WIDTH), lambda bi, qi: (bi, prev_i(qi), 0)),
        pl.BlockSpec((1, tile, SWA_KV_WIDTH), lambda bi, qi: (bi, qi, 0)),
        pl.BlockSpec((1, blk, SWA_KV_WIDTH), lambda bi, qi: (bi, next_i(qi), 0)),
        pl.BlockSpec((1, vrows, blk), lambda bi, qi: (bi, 0, prev_i(qi))),
        pl.BlockSpec((1, vrows, tile), lambda bi, qi: (bi, 0, qi)),
        pl.BlockSpec((1, vrows, blk), lambda bi, qi: (bi, 0, next_i(qi))),
        _const_spec(bias.shape),
    ]
    return pl.pallas_call(
        functools.partial(_swa_kernel, seq_len=s),
        grid=(b, s // tile), in_specs=in_specs,
        out_specs=pl.BlockSpec((1, SWA_WIDTH, tile), lambda bi, qi: (bi, 0, qi)),
        out_shape=jax.ShapeDtypeStruct((b, SWA_WIDTH, s), BF16),
        compiler_params=_params(("arbitrary", "arbitrary")),
        name="swa",
    )(sinks, cqt, ck, ck, ck, cvt, cvt, cvt, bias)


def _mix_out_body(x_ref, yat_ref, bu_ref, bv_ref, yct_ref, lng_ref, lnb_ref, ws_ref, bs_ref,
                  wa_ref, wb_ref, wc_ref, g_ref):
    tn = x_ref.shape[1]
    v = bv_ref[0]
    mu = jnp.mean(v, axis=-1, keepdims=True)
    var = jnp.mean(jnp.square(v - mu), axis=-1, keepdims=True)
    vn = ((v - mu) * lax.rsqrt(var + EPS) * lng_ref[...] + lnb_ref[...]).astype(BF16)
    lane_group = lax.broadcasted_iota(jnp.int32, (CHUNK, GMLP_WIDTH), 1) // GMLP_GROUP_DIM
    u = bu_ref[0]
    yb = []
    for c in range(tn // CHUNK):
        vc = vn[c * CHUNK:(c + 1) * CHUNK]
        mixed = bs_ref[...]
        for g in range(GMLP_GROUPS):
            mixed = mixed + jnp.where(lane_group == g, _dot(ws_ref[g], vc), 0.0)
        yb.append(u[c * CHUNK:(c + 1) * CHUNK] * mixed)
    yb = jnp.concatenate(yb, axis=0).astype(BF16)
    y = _dot_tn(yat_ref[0], wa_ref[...]) + _dot(yb, wb_ref[...]) + _dot_tn(yct_ref[0], wc_ref[...])
    return x_ref[0] + _rms(y, g_ref[...])


def _sigmoid(z):
    return 1.0 / (1.0 + jnp.exp(-z))


def _ffn_ple_body(x, p_ref, gpre_ref, wg_ref, wu_ref, wo_ref, gpost_ref,
                  wup_ref, wgate_ref, ggate_ref, gple_ref):
    h = _rms(x, gpre_ref[...]).astype(BF16)
    f = jnp.zeros(x.shape, F32)
    for lo in range(0, D_FF, FF_CHUNK):
        sl = slice(lo, min(lo + FF_CHUNK, D_FF))
        gate = _dot(h, wg_ref[:, sl])
        up = _dot(h, wu_ref[:, sl])
        a = (gate * _sigmoid(gate) * up).astype(BF16)
        f = f + _dot(a, wo_ref[sl, :])
    x = x + _rms(f, gpost_ref[...])
    e = _dot(p_ref[0, 0].astype(BF16), wup_ref[...])
    gt = _sigmoid(_dot(_rms(x, ggate_ref[...]).astype(BF16), wgate_ref[...]))
    return x + _rms(e * gt, gple_ref[...])


N_MIX_REFS = 13


def _channel_kernel(*refs):
    o_ref = refs[-1]
    x_mid = _mix_out_body(*refs[:N_MIX_REFS])
    o_ref[0] = _ffn_ple_body(x_mid, *refs[N_MIX_REFS:-1])


def _channel_call(x, yat, bu, bv, yc, mix_consts, p, layer, ffn_consts):
    b, s, d = x.shape
    tn = TOKEN_TILE
    tok = lambda bi, si: (bi, si, 0)
    in_specs = [
        pl.BlockSpec((1, tn, d), tok),
        pl.BlockSpec((1, DIFF_WIDTH, tn), lambda bi, si: (bi, 0, si)),
        pl.BlockSpec((1, tn, GMLP_WIDTH), tok),
        pl.BlockSpec((1, tn, GMLP_WIDTH), tok),
        pl.BlockSpec((1, SWA_WIDTH, tn), lambda bi, si: (bi, 0, si)),
    ] + [_const_spec(a.shape) for a in mix_consts] + [
        pl.BlockSpec((1, 1, tn, PLE_DIM), lambda bi, si: (layer, bi, si, 0)),
    ] + [_const_spec(a.shape) for a in ffn_consts]
    assert 5 + len(mix_consts) == N_MIX_REFS
    return pl.pallas_call(
        _channel_kernel,
        grid=(b, s // tn), in_specs=in_specs,
        out_specs=pl.BlockSpec((1, tn, d), tok),
        out_shape=jax.ShapeDtypeStruct((b, s, d), F32),
        compiler_params=_params(("arbitrary", "arbitrary")),
        name="mix_ffn_ple",
    )(x, yat, bu, bv, yc, *mix_consts, p, *ffn_consts)


def _prep_in_weights(w):
    d = w.shape[0]
    aq = w[:, 0:384].reshape(d, DIFF_HEADS, 2, DIFF_QK_DIM)
    ak = w[:, 384:768].reshape(d, DIFF_HEADS, 2, DIFF_QK_DIM)
    av = w[:, 768:1152].reshape(d, DIFF_HEADS, DIFF_V_DIM)
    pad_qk = ((0, 0), (0, 0), (0, 0), (0, COMP_LANES - DIFF_QK_DIM))
    wk = jnp.pad(ak, pad_qk).reshape(d, DIFF_HEADS * KDIM)
    wq = jnp.pad(aq, pad_qk).reshape(d, DIFF_HEADS * KDIM)
    wv = jnp.pad(av, ((0, 0), (0, 0), (0, VT_ROWS - DIFF_V_DIM))).reshape(d, DIFF_HEADS * VT_ROWS)
    cq = w[:, 1664:2048] * (HEAD_DIM ** -0.5)
    cv = w[:, 2176:2304].reshape(d, SWA_KV_HEADS, HEAD_DIM)
    cv = jnp.pad(cv, ((0, 0), (0, 0), (0, VT_ROWS - HEAD_DIM))).reshape(d, SWA_KV_HEADS * VT_ROWS)
    wstd = jnp.concatenate([wk, w[:, 1152:1664], w[:, 2048:2176]], axis=1).astype(BF16)
    wtr = jnp.concatenate([wq, wv, cq, cv], axis=1).T.astype(BF16)
    return wstd, wtr


def kernel(x, p, g_pre_mix, w_in, lam_q1, lam_k1, lam_q2, lam_k2, g_diff_sub, gmlp_ln_g, gmlp_ln_b,
           w_spatial, b_spatial, swa_sinks, w_out, g_post_mix, g_pre_ffn, w_ffn_in, w_ffn_out,
           g_post_ffn, w_ple_up, w_ple_gate, g_ple_gate, g_ple_post):
    b, s, d = x.shape
    depth = w_in.shape[0]
    assert d == D_MODEL and s % ATT_QBLOCK == 0 and s >= 2 * ATT_QBLOCK and s % TOKEN_TILE == 0

    slope2_np, qaug_np, kaug_np, ktab_np, diag_np = _diff_bias_constants(TOKEN_TILE, s)
    _, swa_slopes_np = _alibi_slopes_np()
    slope2 = jnp.asarray(slope2_np)
    qaug = jnp.asarray(qaug_np)
    kaug = jnp.asarray(kaug_np)
    ktab = jnp.asarray(ktab_np)
    diag = jnp.asarray(diag_np)
    swa_bias = jnp.asarray(_swa_bias_np(swa_slopes_np))
    row = lambda a: a.reshape(1, -1).astype(F32)

    for l in range(depth):
        lam_init = 0.8 - 0.6 * math.exp(-0.3 * l)
        wstd, wtr = _prep_in_weights(w_in[l])
        lamp = jnp.stack([lam_q1[l], lam_k1[l], lam_q2[l], lam_k2[l]]).astype(F32)
        kblk, qt, vtblk, bu, bv, cq, ck, cv, lam_tile = _proj_call(
            x, row(g_pre_mix[l]), wstd, wtr, kaug, ktab, lamp, lam_init)

        gsub_b = jnp.broadcast_to(g_diff_sub[l].astype(F32)[:, None], (DIFF_V_DIM, ATT_QBLOCK))
        yat = _diff_attn_call(slope2, qt, kblk, vtblk, qaug, diag, lam_tile, gsub_b, 1.0 - lam_init)
        sink_row = jnp.repeat(swa_sinks[l].astype(F32), SWA_BLOCK).reshape(1, SWA_HEADS * SWA_BLOCK)
        yc = _swa_call(sink_row, cq, ck, cv, swa_bias)

        bs = jnp.broadcast_to(b_spatial[l].T[:, :, None], (CHUNK, GMLP_GROUPS, GMLP_GROUP_DIM))
        bs = bs.reshape(CHUNK, GMLP_WIDTH).astype(F32)
        wo = w_out[l].astype(BF16)
        mix_consts = (row(gmlp_ln_g[l]), row(gmlp_ln_b[l]), w_spatial[l].astype(BF16), bs,
                      wo[0:DIFF_WIDTH], wo[DIFF_WIDTH:DIFF_WIDTH + GMLP_WIDTH], wo[DIFF_WIDTH + GMLP_WIDTH:],
                      row(g_post_mix[l]))
        wfi = w_ffn_in[l].astype(BF16)
        ffn_consts = (row(g_pre_ffn[l]), wfi[:, :D_FF], wfi[:, D_FF:], w_ffn_out[l].astype(BF16),
                      row(g_post_ffn[l]), w_ple_up[l].astype(BF16), w_ple_gate[l].astype(BF16),
                      row(g_ple_gate[l]), row(g_ple_post[l]))
        x = _channel_call(x, yat, bu, bv, yc, mix_consts, p, l, ffn_consts)
    return x
```

```python
import functools
import math

import numpy as np
import jax
import jax.numpy as jnp
from jax import lax
from jax.experimental import pallas as pl
from jax.experimental.pallas import tpu as pltpu

D_MODEL = 1024
HEAD_DIM = 64
DIFF_HEADS = 6
DIFF_QK_DIM = 32
DIFF_V_DIM = 64
DIFF_WIDTH = DIFF_HEADS * DIFF_V_DIM
GMLP_GROUPS = 4
GMLP_GROUP_DIM = 64
GMLP_WIDTH = GMLP_GROUPS * GMLP_GROUP_DIM
CHUNK = 128
SWA_HEADS = 6
SWA_KV_HEADS = 2
SWA_GROUP = SWA_HEADS // SWA_KV_HEADS
SWA_WIDTH = SWA_HEADS * HEAD_DIM
WINDOW = 128
SWA_BLOCK = 128
SWA_TILE = 512
D_FF = 2816
PLE_DIM = 256
N_ATTN_HEADS = DIFF_HEADS + SWA_HEADS
ALIBI_MAX_EXP = 8.0
EPS = 1e-6
NEG_INF = -1e30
LOG2E = 1.4426950408889634

LANES = 128
MXU_DIM_V7X = 256
VMEM_LIMIT_BYTES_V7X = 56 * 1024 * 1024

TOKEN_TILE = 512
ATT_BLOCK = MXU_DIM_V7X
ATT_QBLOCK = 2 * ATT_BLOCK
COMP_LANES = 64
KDIM = 2 * COMP_LANES
AUG_OFF = DIFF_QK_DIM
SWA_KV_WIDTH = SWA_KV_HEADS * HEAD_DIM
DYN_ROW0 = 112
DYN_ROWS = KDIM - DYN_ROW0
VT_ROWS = 80
FF_CHUNK = 256

F32 = jnp.float32
BF16 = jnp.bfloat16


def _bf16_round_np(x):
    u = np.asarray(x, np.float32).view(np.uint32).astype(np.uint64)
    r = ((u >> 16) & 1) + 0x7FFF
    return ((u + r) & 0xFFFF0000).astype(np.uint32).view(np.float32)


def _alibi_slopes_np():
    k = np.arange(1, N_ATTN_HEADS + 1, dtype=np.float64)
    s = np.exp2(-ALIBI_MAX_EXP * k / N_ATTN_HEADS).astype(np.float32)
    return s[SWA_HEADS:], s[:SWA_HEADS]


def _split3_bf16(v):
    v = np.asarray(v, np.float32)
    hi = _bf16_round_np(v)
    mid = _bf16_round_np(v - hi)
    lo = _bf16_round_np(v - hi - mid)
    return hi, mid, lo


def _diff_bias_constants(token_tile, seq_len):
    slopes, _ = _alibi_slopes_np()
    slope2 = (slopes.astype(np.float64) * LOG2E).astype(np.float32)
    hi, mid, lo = _split3_bf16(slope2)
    parts = np.stack([hi, mid, lo], axis=1)
    rel = np.arange(ATT_BLOCK, dtype=np.float32)
    qaug = np.zeros((DIFF_HEADS, KDIM, ATT_BLOCK), np.float32)
    kaug = np.zeros((DIFF_HEADS, ATT_BLOCK, KDIM), np.float32)
    for c in range(2):
        base = c * COMP_LANES + AUG_OFF
        for t in range(3):
            qaug[:, base + t, :] = rel[None, :]
            qaug[:, base + 3 + t, :] = -parts[:, t][:, None]
            kaug[:, :, base + t] = parts[:, t][:, None]
            kaug[:, :, base + 3 + t] = rel[None, :]
    kaug[:, :, DYN_ROW0:DYN_ROW0 + 3] = 1.0
    kaug = np.tile(kaug, (1, token_tile // ATT_BLOCK, 1))
    qaug = np.tile(qaug, (1, 1, ATT_QBLOCK // ATT_BLOCK))
    n_kb = seq_len // ATT_BLOCK
    sigma_j = (slope2[:, None] * np.float32(ATT_BLOCK)) * np.arange(n_kb, dtype=np.float32)[None, :]
    ktab = np.zeros((DIFF_HEADS, n_kb, 1, KDIM), np.float32)
    for t, piece in enumerate(_split3_bf16(sigma_j)):
        ktab[:, :, 0, DYN_ROW0 + 3 + t] = piece
    pos = np.arange(ATT_QBLOCK, dtype=np.float32)
    dist = np.abs(pos[:, None] - pos[None, :])
    diag = -(slope2[:, None, None] * dist[None])
    return slope2, qaug, kaug, ktab, diag.astype(np.float32)


def _swa_bias_np(slopes):
    key = np.arange(3 * SWA_BLOCK, dtype=np.float32)[:, None]
    qry = np.arange(SWA_BLOCK, dtype=np.float32)[None, :]
    dist = np.abs(key - SWA_BLOCK - qry)
    bias = -(slopes.astype(np.float32)[:, None, None] * dist[None])
    bias = np.where(dist[None] <= WINDOW, bias, np.float32(NEG_INF)).astype(np.float32)
    return np.concatenate(list(bias), axis=1)


def _rms(x, g):
    return x * lax.rsqrt(jnp.mean(x * x, axis=-1, keepdims=True) + EPS) * g


def _dot(a, b):
    return jnp.dot(a, b, preferred_element_type=F32)


def _dot_nt(a, b):
    return lax.dot_general(a, b, (((1,), (1,)), ((), ())), preferred_element_type=F32)


def _dot_tn(a, b):
    return lax.dot_general(a, b, (((0,), (0,)), ((), ())), preferred_element_type=F32)


def _const_spec(shape):
    nd = len(shape)
    return pl.BlockSpec(shape, lambda *_: (0,) * nd, pipeline_mode=pl.Buffered(1))


def _params(sem):
    return pltpu.CompilerParams(dimension_semantics=sem, vmem_limit_bytes=VMEM_LIMIT_BYTES_V7X)


N_STD = DIFF_HEADS * KDIM + 2 * GMLP_WIDTH + SWA_KV_HEADS * HEAD_DIM
N_TR = DIFF_HEADS * KDIM + DIFF_HEADS * VT_ROWS + SWA_WIDTH + SWA_KV_HEADS * VT_ROWS
Q_SCALE = (DIFF_QK_DIM ** -0.5) * LOG2E


def _proj_kernel(x_ref, g_ref, wstd_ref, wtr_ref, kaug_ref, ktab_ref, lamp_ref,
                 k_ref, qt_ref, vt_ref, bu_ref, bv_ref, cqt_ref, ck_ref, cvt_ref, lam_ref, *, lam_init):
    tn = x_ref.shape[1]
    nsub = tn // ATT_BLOCK
    h = _rms(x_ref[0], g_ref[...]).astype(BF16)
    r1 = _dot(h, wstd_ref[...])
    for hh in range(DIFF_HEADS):
        kk = r1[:, hh * KDIM:(hh + 1) * KDIM] + kaug_ref[hh]
        for j in range(nsub):
            k_ref[0, hh, j] = (kk[j * ATT_BLOCK:(j + 1) * ATT_BLOCK] + ktab_ref[hh, j]).astype(BF16)
    o = DIFF_HEADS * KDIM
    bu_ref[0] = r1[:, o:o + GMLP_WIDTH]
    o += GMLP_WIDTH
    bv_ref[0] = r1[:, o:o + GMLP_WIDTH]
    o += GMLP_WIDTH
    ck_ref[0] = r1[:, o:o + SWA_KV_WIDTH].astype(BF16)

    r2 = _dot_nt(wtr_ref[...], h)
    ones_row = jnp.where(lax.broadcasted_iota(jnp.int32, (VT_ROWS, ATT_BLOCK), 0) == DIFF_V_DIM, 1.0, 0.0)
    vo = DIFF_HEADS * KDIM
    for hh in range(DIFF_HEADS):
        qt_ref[0, hh] = (r2[hh * KDIM:(hh + 1) * KDIM] * Q_SCALE).astype(BF16)
        vv = r2[vo + hh * VT_ROWS: vo + (hh + 1) * VT_ROWS]
        for j in range(nsub):
            vt_ref[0, hh, j] = (vv[:, j * ATT_BLOCK:(j + 1) * ATT_BLOCK] + ones_row).astype(BF16)
    o = vo + DIFF_HEADS * VT_ROWS
    cqt_ref[0] = r2[o:o + SWA_WIDTH].astype(BF16)
    o += SWA_WIDTH
    ones_rows = jnp.where(lax.broadcasted_iota(jnp.int32, (SWA_KV_HEADS * VT_ROWS, tn), 0) % VT_ROWS == HEAD_DIM,
                          1.0, 0.0)
    cvt_ref[0] = (r2[o:o + SWA_KV_HEADS * VT_ROWS] + ones_rows).astype(BF16)

    lp = lamp_ref[...]
    s1 = jnp.sum(lp[0:1] * lp[1:2], axis=-1, keepdims=True)
    s2 = jnp.sum(lp[2:3] * lp[3:4], axis=-1, keepdims=True)
    lam = jnp.exp(s1) - jnp.exp(s2) + lam_init
    lam_ref[...] = jnp.broadcast_to(lam, lam_ref.shape)


def _proj_call(x, g, wstd, wtr, kaug, ktab, lamp, lam_init):
    b, s, d = x.shape
    tn = TOKEN_TILE
    nsub = tn // ATT_BLOCK
    nkb = s // ATT_BLOCK
    grid = (b, s // tn)
    tok = lambda bi, si: (bi, si, 0)
    out_shape = (
        jax.ShapeDtypeStruct((b, DIFF_HEADS, nkb, ATT_BLOCK, KDIM), BF16),
        jax.ShapeDtypeStruct((b, DIFF_HEADS, KDIM, s), BF16),
        jax.ShapeDtypeStruct((b, DIFF_HEADS, nkb, VT_ROWS, ATT_BLOCK), BF16),
        jax.ShapeDtypeStruct((b, s, GMLP_WIDTH), F32),
        jax.ShapeDtypeStruct((b, s, GMLP_WIDTH), F32),
        jax.ShapeDtypeStruct((b, SWA_WIDTH, s), BF16),
        jax.ShapeDtypeStruct((b, s, SWA_KV_WIDTH), BF16),
        jax.ShapeDtypeStruct((b, SWA_KV_HEADS * VT_ROWS, s), BF16),
        jax.ShapeDtypeStruct((8, LANES), F32),
    )
    out_specs = (
        pl.BlockSpec((1, DIFF_HEADS, nsub, ATT_BLOCK, KDIM), lambda bi, si: (bi, 0, si, 0, 0)),
        pl.BlockSpec((1, DIFF_HEADS, KDIM, tn), lambda bi, si: (bi, 0, 0, si)),
        pl.BlockSpec((1, DIFF_HEADS, nsub, VT_ROWS, ATT_BLOCK), lambda bi, si: (bi, 0, si, 0, 0)),
        pl.BlockSpec((1, tn, GMLP_WIDTH), tok),
        pl.BlockSpec((1, tn, GMLP_WIDTH), tok),
        pl.BlockSpec((1, SWA_WIDTH, tn), lambda bi, si: (bi, 0, si)),
        pl.BlockSpec((1, tn, SWA_KV_WIDTH), tok),
        pl.BlockSpec((1, SWA_KV_HEADS * VT_ROWS, tn), lambda bi, si: (bi, 0, si)),
        pl.BlockSpec((8, LANES), lambda bi, si: (0, 0)),
    )
    in_specs = [
        pl.BlockSpec((1, tn, d), tok),
        _const_spec(g.shape),
        _const_spec(wstd.shape),
        _const_spec(wtr.shape),
        _const_spec(kaug.shape),
        pl.BlockSpec((DIFF_HEADS, nsub, 1, KDIM), lambda bi, si: (0, si, 0, 0)),
        _const_spec(lamp.shape),
    ]
    return pl.pallas_call(
        functools.partial(_proj_kernel, lam_init=lam_init),
        grid=grid, in_specs=in_specs, out_specs=out_specs, out_shape=out_shape,
        compiler_params=_params(("arbitrary", "arbitrary")),
        name="proj",
    )(x, g, wstd, wtr, kaug, ktab, lamp)


def _diff_attn_kernel(slope_ref, qt_ref, k_ref, vt_ref, qaug_ref, diag_ref, lam_ref, gsub_ref,
                      o_ref, w_ref, s0_ref, s1_ref, p0_ref, p1_ref, acc_ref, *, out_scale):
    hh = pl.program_id(1)
    qi = pl.program_id(2)
    blk = ATT_BLOCK
    tq = ATT_QBLOCK
    n_rest = k_ref.shape[2] // 2 - 1
    sigma = slope_ref[hh] * float(blk)

    qt = qt_ref[0, 0].astype(F32)
    qaug = qaug_ref[0]
    row = lax.broadcasted_iota(jnp.int32, (KDIM, tq), 0)
    comp_mask = (row < COMP_LANES, row >= COMP_LANES)
    zero = jnp.zeros_like(qt)
    for c in range(2):
        w_ref[0, c] = jnp.where(comp_mask[c], qt - qaug, zero).astype(BF16)
        w_ref[1, c] = jnp.where(comp_mask[c], qt + qaug, zero).astype(BF16)

    def key_rows(kb):
        return jnp.concatenate([k_ref[0, 0, 2 * kb], k_ref[0, 0, 2 * kb + 1]], axis=0)

    def value_cols(kb):
        return jnp.concatenate([vt_ref[0, 0, 2 * kb], vt_ref[0, 0, 2 * kb + 1]], axis=1)

    k_t = key_rows(qi)
    vt = value_cols(qi)
    m = []
    for c in range(2):
        w_c = jnp.where(comp_mask[c], qt, zero).astype(BF16)
        m_parts = []
        for n in range(tq // blk):
            cols = slice(n * blk, (n + 1) * blk)
            s = _dot(k_t, w_c[:, cols]) + diag_ref[0, :, cols]
            m_cn = jnp.max(s, axis=0, keepdims=True)
            acc_ref[c, :, cols] = _dot(vt, jnp.exp2((s - m_cn).astype(BF16)))
            m_parts.append(m_cn)
        m.append(jnp.concatenate(m_parts, axis=1))

    dyn_row = lax.broadcasted_iota(jnp.int32, (DYN_ROWS, tq), 0)
    lane_row = lax.broadcasted_iota(jnp.int32, (1, tq), 1)
    q_origin = sigma * jnp.where(lane_row < blk, 2 * qi, 2 * qi + 1).astype(F32)

    def rest_block(j):
        after = (j >= qi).astype(jnp.int32)
        return j + after, after

    def reference_rows(rr, key_sign):
        hi = rr.astype(BF16).astype(F32)
        mid = (rr - hi).astype(BF16).astype(F32)
        lo = rr - hi - mid
        rows = jnp.where(dyn_row == 0, hi, jnp.where(dyn_row == 1, mid, jnp.where(dyn_row == 2, lo, 0.0)))
        rows = jnp.where((dyn_row >= 3) & (dyn_row < 6), key_sign, rows)
        return rows.astype(BF16)

    def column_max(z):
        parts = [z[i * 16:(i + 1) * 16] for i in range(z.shape[0] // 16)]
        while len(parts) > 1:
            parts = [jnp.maximum(parts[i], parts[i + 1]) for i in range(0, len(parts), 2)]
        return jnp.max(parts[0].astype(F32), axis=0, keepdims=True)

    tiles = [slice(n * blk, (n + 1) * blk) for n in range(tq // blk)]
    s_bufs = (s0_ref, s1_ref)
    p_bufs = (p0_ref, p1_ref)
    alphas = {}
    maxes = {-2: m, -1: m}
    for t in range(n_rest + 2):
        do_scores = t < n_rest
        do_pv = t >= 2
        if do_scores:
            kb_s, side = rest_block(t)
            sgn = jnp.where(side == 1, 1.0, -1.0)
            k_t = key_rows(kb_s)
        if do_pv:
            kb_v, _ = rest_block(t - 2)
            vt = value_cols(kb_v)
        for c in range(2):
            if do_scores:
                dyn = reference_rows(sgn * q_origin - maxes[t - 2][c], -sgn)
                w = jnp.concatenate([w_ref[side, c, 0:DYN_ROW0, :], dyn], axis=0)
            for n, cols in enumerate(tiles):
                if do_scores:
                    s_bufs[t % 2][c, :, cols] = _dot(k_t, w[:, cols]).astype(BF16)
                if do_pv:
                    acc_ref[c, :, cols] = (acc_ref[c, :, cols] * alphas[t - 2][c][:, cols]
                                           + _dot(vt, p_bufs[t % 2][c, :, cols]))
        u = t - 1
        if 0 <= u < n_rest:
            m2, m1 = maxes[u - 2], maxes[u - 1]
            al_u, m_u = [], []
            for c in range(2):
                z = s_bufs[u % 2][c]
                d = jnp.maximum(m1[c] - m2[c], column_max(z)).astype(BF16)
                m_new = m2[c] + d.astype(F32)
                p_bufs[u % 2][c] = jnp.exp2(z - d)
                al_u.append(jnp.exp2(m1[c] - m_new))
                m_u.append(m_new)
            alphas[u], maxes[u] = al_u, m_u

    a0 = acc_ref[0]
    a1 = acc_ref[1]
    o0 = a0[0:DIFF_V_DIM] / a0[DIFF_V_DIM:DIFF_V_DIM + 1]
    o1 = a1[0:DIFF_V_DIM] / a1[DIFF_V_DIM:DIFF_V_DIM + 1]
    lam = lam_ref[0:1, 0:1]
    o = o0 - lam * o1
    ms = jnp.mean(o * o, axis=0, keepdims=True)
    y = o * lax.rsqrt(ms + EPS) * gsub_ref[...] * out_scale
    o_ref[0] = y.astype(o_ref.dtype)


def _diff_attn_call(slope2, qt, kblk, vtblk, qaug, diag, lam_tile, gsub_b, out_scale):
    b, nh, kdim, s = qt.shape
    nkb = kblk.shape[2]
    blk = ATT_BLOCK
    tq = ATT_QBLOCK
    grid = (b, nh, s // tq)
    in_specs = [
        pl.BlockSpec(memory_space=pltpu.SMEM),
        pl.BlockSpec((1, 1, kdim, tq), lambda bi, hi, qi: (bi, hi, 0, qi)),
        pl.BlockSpec((1, 1, nkb, blk, kdim), lambda bi, hi, qi: (bi, hi, 0, 0, 0)),
        pl.BlockSpec((1, 1, nkb, VT_ROWS, blk), lambda bi, hi, qi: (bi, hi, 0, 0, 0)),
        pl.BlockSpec((1, kdim, tq), lambda bi, hi, qi: (hi, 0, 0)),
        pl.BlockSpec((1, tq, tq), lambda bi, hi, qi: (hi, 0, 0)),
        _const_spec(lam_tile.shape),
        _const_spec(gsub_b.shape),
    ]
    return pl.pallas_call(
        functools.partial(_diff_attn_kernel, out_scale=out_scale),
        grid=grid, in_specs=in_specs,
        out_specs=pl.BlockSpec((1, DIFF_V_DIM, tq), lambda bi, hi, qi: (bi, hi, qi)),
        out_shape=jax.ShapeDtypeStruct((b, nh * DIFF_V_DIM, s), BF16),
        scratch_shapes=[pltpu.VMEM((2, 2, kdim, tq), BF16),
                        *([pltpu.VMEM((2, tq, tq), BF16)] * 4),
                        pltpu.VMEM((2, VT_ROWS, tq), F32)],
        compiler_params=_params(("arbitrary", "arbitrary", "arbitrary")),
        name="diff_attn",
    )(slope2, qt, kblk, vtblk, qaug, diag, lam_tile, gsub_b)


def _swa_kernel(sink_ref, qt_ref, kp_ref, kc_ref, kn_ref, vp_ref, vc_ref, vn_ref, bias_ref, o_ref, *, seq_len):
    qi = pl.program_id(1)
    blk = SWA_BLOCK
    nsub = SWA_TILE // blk
    k_all = jnp.concatenate([kp_ref[0], kc_ref[0], kn_ref[0]], axis=0)
    vt_all = jnp.concatenate([vp_ref[0], vc_ref[0], vn_ref[0]], axis=1)
    key_row = lax.broadcasted_iota(jnp.int32, (3 * blk, SWA_HEADS * blk), 0)
    no_q = jnp.zeros((HEAD_DIM, SWA_GROUP * blk), BF16)
    sink = sink_ref[...]
    gw = SWA_GROUP * blk
    for sub in range(nsub):
        key_pos = (qi * nsub + sub - 1) * blk + key_row
        in_seq = (key_pos >= 0) & (key_pos < seq_len)
        kk = k_all[sub * blk:(sub + 3) * blk]
        q_t = [qt_ref[0, hq * HEAD_DIM:(hq + 1) * HEAD_DIM, sub * blk:(sub + 1) * blk] for hq in range(SWA_HEADS)]
        w = jnp.concatenate([jnp.concatenate(q_t[:SWA_GROUP] + [no_q], axis=1),
                             jnp.concatenate([no_q] + q_t[SWA_GROUP:], axis=1)], axis=0)
        sc = _dot(kk, w) + bias_ref[...]
        sc = jnp.where(in_seq, sc, NEG_INF)
        m = jnp.maximum(jnp.max(sc, axis=0, keepdims=True), sink)
        e = jnp.exp(sc - m).astype(BF16)
        tail = jnp.exp(sink - m)
        for kh in range(SWA_KV_HEADS):
            vt = vt_all[kh * VT_ROWS:(kh + 1) * VT_ROWS, sub * blk:(sub + 3) * blk]
            acc = _dot(vt, e[:, kh * gw:(kh + 1) * gw])
            o = acc[0:HEAD_DIM] / (acc[HEAD_DIM:HEAD_DIM + 1] + tail[:, kh * gw:(kh + 1) * gw])
            for g in range(SWA_GROUP):
                hq = kh * SWA_GROUP + g
                o_ref[0, hq * HEAD_DIM:(hq + 1) * HEAD_DIM, sub * blk:(sub + 1) * blk] = (
                    o[:, g * blk:(g + 1) * blk].astype(o_ref.dtype))


def _swa_call(sinks, cqt, ck, cvt, bias):
    b, _, s = cqt.shape
    blk = SWA_BLOCK
    tile = SWA_TILE
    nsub = tile // blk
    nb = s // blk
    prev_i = lambda qi: jnp.maximum(qi * nsub - 1, 0)
    next_i = lambda qi: jnp.minimum((qi + 1) * nsub, nb - 1)
    vrows = SWA_KV_HEADS * VT_ROWS
    in_specs = [
        _const_spec(sinks.shape),
        pl.BlockSpec((1, SWA_WIDTH, tile), lambda bi, qi: (bi, 0, qi)),
        pl.BlockSpec((1, blk, SWA_KV_WIDTH), lambda bi, qi: (bi, prev_i(qi), 0)),
        pl.BlockSpec((1, tile, SWA_KV_WIDTH), lambda bi, qi: (bi, qi, 0)),
        pl.BlockSpec((1, blk, SWA_KV_WIDTH), lambda bi, qi: (bi, next_i(qi), 0)),
        pl.BlockSpec((1, vrows, blk), lambda bi, qi: (bi, 0, prev_i(qi))),
        pl.BlockSpec((1, vrows, tile), lambda bi, qi: (bi, 0, qi)),
        pl.BlockSpec((1, vrows, blk), lambda bi, qi: (bi, 0, next_i(qi))),
        _const_spec(bias.shape),
    ]
    return pl.pallas_call(
        functools.partial(_swa_kernel, seq_len=s),
        grid=(b, s // tile), in_specs=in_specs,
        out_specs=pl.BlockSpec((1, SWA_WIDTH, tile), lambda bi, qi: (bi, 0, qi)),
        out_shape=jax.ShapeDtypeStruct((b, SWA_WIDTH, s), BF16),
        compiler_params=_params(("arbitrary", "arbitrary")),
        name="swa",
    )(sinks, cqt, ck, ck, ck, cvt, cvt, cvt, bias)


def _mix_out_body(x_ref, yat_ref, bu_ref, bv_ref, yct_ref, lng_ref, lnb_ref, ws_ref, bs_ref,
                  wa_ref, wb_ref, wc_ref, g_ref):
    tn = x_ref.shape[1]
    v = bv_ref[0]
    mu = jnp.mean(v, axis=-1, keepdims=True)
    var = jnp.mean(jnp.square(v - mu), axis=-1, keepdims=True)
    vn = ((v - mu) * lax.rsqrt(var + EPS) * lng_ref[...] + lnb_ref[...]).astype(BF16)
    lane_group = lax.broadcasted_iota(jnp.int32, (CHUNK, GMLP_WIDTH), 1) // GMLP_GROUP_DIM
    u = bu_ref[0]
    yb = []
    for c in range(tn // CHUNK):
        vc = vn[c * CHUNK:(c + 1) * CHUNK]
        mixed = bs_ref[...]
        for g in range(GMLP_GROUPS):
            mixed = mixed + jnp.where(lane_group == g, _dot(ws_ref[g], vc), 0.0)
        yb.append(u[c * CHUNK:(c + 1) * CHUNK] * mixed)
    yb = jnp.concatenate(yb, axis=0).astype(BF16)
    y = _dot_tn(yat_ref[0], wa_ref[...]) + _dot(yb, wb_ref[...]) + _dot_tn(yct_ref[0], wc_ref[...])
    return x_ref[0] + _rms(y, g_ref[...])


def _sigmoid(z):
    return 1.0 / (1.0 + jnp.exp(-z))


def _ffn_ple_body(x, p_ref, gpre_ref, wg_ref, wu_ref, wo_ref, gpost_ref,
                  wup_ref, wgate_ref, ggate_ref, gple_ref):
    h = _rms(x, gpre_ref[...]).astype(BF16)
    f = jnp.zeros(x.shape, F32)
    for lo in range(0, D_FF, FF_CHUNK):
        sl = slice(lo, min(lo + FF_CHUNK, D_FF))
        gate = _dot(h, wg_ref[:, sl])
        up = _dot(h, wu_ref[:, sl])
        a = (gate * _sigmoid(gate) * up).astype(BF16)
        f = f + _dot(a, wo_ref[sl, :])
    x = x + _rms(f, gpost_ref[...])
    e = _dot(p_ref[0, 0].astype(BF16), wup_ref[...])
    gt = _sigmoid(_dot(_rms(x, ggate_ref[...]).astype(BF16), wgate_ref[...]))
    return x + _rms(e * gt, gple_ref[...])


N_MIX_REFS = 13


def _channel_kernel(*refs):
    o_ref = refs[-1]
    x_mid = _mix_out_body(*refs[:N_MIX_REFS])
    o_ref[0] = _ffn_ple_body(x_mid, *refs[N_MIX_REFS:-1])


def _channel_call(x, yat, bu, bv, yc, mix_consts, p, layer, ffn_consts):
    b, s, d = x.shape
    tn = TOKEN_TILE
    tok = lambda bi, si: (bi, si, 0)
    in_specs = [
        pl.BlockSpec((1, tn, d), tok),
        pl.BlockSpec((1, DIFF_WIDTH, tn), lambda bi, si: (bi, 0, si)),
        pl.BlockSpec((1, tn, GMLP_WIDTH), tok),
        pl.BlockSpec((1, tn, GMLP_WIDTH), tok),
        pl.BlockSpec((1, SWA_WIDTH, tn), lambda bi, si: (bi, 0, si)),
    ] + [_const_spec(a.shape) for a in mix_consts] + [
        pl.BlockSpec((1, 1, tn, PLE_DIM), lambda bi, si: (layer, bi, si, 0)),
    ] + [_const_spec(a.shape) for a in ffn_consts]
    assert 5 + len(mix_consts) == N_MIX_REFS
    return pl.pallas_call(
        _channel_kernel,
        grid=(b, s // tn), in_specs=in_specs,
        out_specs=pl.BlockSpec((1, tn, d), tok),
        out_shape=jax.ShapeDtypeStruct((b, s, d), F32),
        compiler_params=_params(("arbitrary", "arbitrary")),
        name="mix_ffn_ple",
    )(x, yat, bu, bv, yc, *mix_consts, p, *ffn_consts)


def _prep_in_weights(w):
    d = w.shape[0]
    aq = w[:, 0:384].reshape(d, DIFF_HEADS, 2, DIFF_QK_DIM)
    ak = w[:, 384:768].reshape(d, DIFF_HEADS, 2, DIFF_QK_DIM)
    av = w[:, 768:1152].reshape(d, DIFF_HEADS, DIFF_V_DIM)
    pad_qk = ((0, 0), (0, 0), (0, 0), (0, COMP_LANES - DIFF_QK_DIM))
    wk = jnp.pad(ak, pad_qk).reshape(d, DIFF_HEADS * KDIM)
    wq = jnp.pad(aq, pad_qk).reshape(d, DIFF_HEADS * KDIM)
    wv = jnp.pad(av, ((0, 0), (0, 0), (0, VT_ROWS - DIFF_V_DIM))).reshape(d, DIFF_HEADS * VT_ROWS)
    cq = w[:, 1664:2048] * (HEAD_DIM ** -0.5)
    cv = w[:, 2176:2304].reshape(d, SWA_KV_HEADS, HEAD_DIM)
    cv = jnp.pad(cv, ((0, 0), (0, 0), (0, VT_ROWS - HEAD_DIM))).reshape(d, SWA_KV_HEADS * VT_ROWS)
    wstd = jnp.concatenate([wk, w[:, 1152:1664], w[:, 2048:2176]], axis=1).astype(BF16)
    wtr = jnp.concatenate([wq, wv, cq, cv], axis=1).T.astype(BF16)
    return wstd, wtr


def kernel(x, p, g_pre_mix, w_in, lam_q1, lam_k1, lam_q2, lam_k2, g_diff_sub, gmlp_ln_g, gmlp_ln_b,
           w_spatial, b_spatial, swa_sinks, w_out, g_post_mix, g_pre_ffn, w_ffn_in, w_ffn_out,
           g_post_ffn, w_ple_up, w_ple_gate, g_ple_gate, g_ple_post):
    b, s, d = x.shape
    depth = w_in.shape[0]
    assert d == D_MODEL and s % ATT_QBLOCK == 0 and s >= 2 * ATT_QBLOCK and s % TOKEN_TILE == 0

    slope2_np, qaug_np, kaug_np, ktab_np, diag_np = _diff_bias_constants(TOKEN_TILE, s)
    _, swa_slopes_np = _alibi_slopes_np()
    slope2 = jnp.asarray(slope2_np)
    qaug = jnp.asarray(qaug_np)
    kaug = jnp.asarray(kaug_np)
    ktab = jnp.asarray(ktab_np)
    diag = jnp.asarray(diag_np)
    swa_bias = jnp.asarray(_swa_bias_np(swa_slopes_np))
    row = lambda a: a.reshape(1, -1).astype(F32)

    for l in range(depth):
        lam_init = 0.8 - 0.6 * math.exp(-0.3 * l)
        wstd, wtr = _prep_in_weights(w_in[l])
        lamp = jnp.stack([lam_q1[l], lam_k1[l], lam_q2[l], lam_k2[l]]).astype(F32)
        kblk, qt, vtblk, bu, bv, cq, ck, cv, lam_tile = _proj_call(
            x, row(g_pre_mix[l]), wstd, wtr, kaug, ktab, lamp, lam_init)

        gsub_b = jnp.broadcast_to(g_diff_sub[l].astype(F32)[:, None], (DIFF_V_DIM, ATT_QBLOCK))
        yat = _diff_attn_call(slope2, qt, kblk, vtblk, qaug, diag, lam_tile, gsub_b, 1.0 - lam_init)
        sink_row = jnp.repeat(swa_sinks[l].astype(F32), SWA_BLOCK).reshape(1, SWA_HEADS * SWA_BLOCK)
        yc = _swa_call(sink_row, cq, ck, cv, swa_bias)

        bs = jnp.broadcast_to(b_spatial[l].T[:, :, None], (CHUNK, GMLP_GROUPS, GMLP_GROUP_DIM))
        bs = bs.reshape(CHUNK, GMLP_WIDTH).astype(F32)
        wo = w_out[l].astype(BF16)
        mix_consts = (row(gmlp_ln_g[l]), row(gmlp_ln_b[l]), w_spatial[l].astype(BF16), bs,
                      wo[0:DIFF_WIDTH], wo[DIFF_WIDTH:DIFF_WIDTH + GMLP_WIDTH], wo[DIFF_WIDTH + GMLP_WIDTH:],
                      row(g_post_mix[l]))
        wfi = w_ffn_in[l].astype(BF16)
        ffn_consts = (row(g_pre_ffn[l]), wfi[:, :D_FF], wfi[:, D_FF:], w_ffn_out[l].astype(BF16),
                      row(g_post_ffn[l]), w_ple_up[l].astype(BF16), w_ple_gate[l].astype(BF16),
                      row(g_ple_gate[l]), row(g_ple_post[l]))
        x = _channel_call(x, yat, bu, bv, yc, mix_consts, p, l, ffn_consts)
    return x
```

```python
import functools
import math

import numpy as np
import jax
import jax.numpy as jnp
from jax import lax
from jax.experimental import pallas as pl
from jax.experimental.pallas import tpu as pltpu

D_MODEL = 1024
HEAD_DIM = 64
DIFF_HEADS = 6
DIFF_QK_DIM = 32
DIFF_V_DIM = 64
DIFF_WIDTH = DIFF_HEADS * DIFF_V_DIM
GMLP_GROUPS = 4
GMLP_GROUP_DIM = 64
GMLP_WIDTH = GMLP_GROUPS * GMLP_GROUP_DIM
CHUNK = 128
SWA_HEADS = 6
SWA_KV_HEADS = 2
SWA_GROUP = SWA_HEADS // SWA_KV_HEADS
SWA_WIDTH = SWA_HEADS * HEAD_DIM
WINDOW = 128
SWA_BLOCK = 128
SWA_TILE = 512
D_FF = 2816
PLE_DIM = 256
N_ATTN_HEADS = DIFF_HEADS + SWA_HEADS
ALIBI_MAX_EXP = 8.0
EPS = 1e-6
NEG_INF = -1e30
LOG2E = 1.4426950408889634

LANES = 128
MXU_DIM_V7X = 256
VMEM_LIMIT_BYTES_V7X = 56 * 1024 * 1024

TOKEN_TILE = 512
PROJ_TILE = 1024
ATT_BLOCK = MXU_DIM_V7X
ATT_QBLOCK = 2 * ATT_BLOCK
COMP_LANES = 64
KDIM = 2 * COMP_LANES
AUG_OFF = DIFF_QK_DIM
SWA_KV_WIDTH = SWA_KV_HEADS * HEAD_DIM
DYN_ROW0 = 112
DYN_ROWS = KDIM - DYN_ROW0
VT_ROWS = 80
FF_CHUNK = 256

F32 = jnp.float32
BF16 = jnp.bfloat16


def _bf16_round_np(x):
    u = np.asarray(x, np.float32).view(np.uint32).astype(np.uint64)
    r = ((u >> 16) & 1) + 0x7FFF
    return ((u + r) & 0xFFFF0000).astype(np.uint32).view(np.float32)


def _alibi_slopes_np():
    k = np.arange(1, N_ATTN_HEADS + 1, dtype=np.float64)
    s = np.exp2(-ALIBI_MAX_EXP * k / N_ATTN_HEADS).astype(np.float32)
    return s[SWA_HEADS:], s[:SWA_HEADS]


def _split3_bf16(v):
    v = np.asarray(v, np.float32)
    hi = _bf16_round_np(v)
    mid = _bf16_round_np(v - hi)
    lo = _bf16_round_np(v - hi - mid)
    return hi, mid, lo


def _diff_bias_constants(token_tile, seq_len):
    slopes, _ = _alibi_slopes_np()
    slope2 = (slopes.astype(np.float64) * LOG2E).astype(np.float32)
    hi, mid, lo = _split3_bf16(slope2)
    parts = np.stack([hi, mid, lo], axis=1)
    rel = np.arange(ATT_BLOCK, dtype=np.float32)
    qaug = np.zeros((DIFF_HEADS, KDIM, ATT_BLOCK), np.float32)
    kaug = np.zeros((DIFF_HEADS, ATT_BLOCK, KDIM), np.float32)
    for c in range(2):
        base = c * COMP_LANES + AUG_OFF
        for t in range(3):
            qaug[:, base + t, :] = rel[None, :]
            qaug[:, base + 3 + t, :] = -parts[:, t][:, None]
            kaug[:, :, base + t] = parts[:, t][:, None]
            kaug[:, :, base + 3 + t] = rel[None, :]
    kaug[:, :, DYN_ROW0:DYN_ROW0 + 3] = 1.0
    kaug = np.tile(kaug, (1, token_tile // ATT_BLOCK, 1))
    qaug = np.tile(qaug, (1, 1, ATT_QBLOCK // ATT_BLOCK))
    n_kb = seq_len // ATT_BLOCK
    sigma_j = (slope2[:, None] * np.float32(ATT_BLOCK)) * np.arange(n_kb, dtype=np.float32)[None, :]
    ktab = np.zeros((DIFF_HEADS, n_kb, 1, KDIM), np.float32)
    for t, piece in enumerate(_split3_bf16(sigma_j)):
        ktab[:, :, 0, DYN_ROW0 + 3 + t] = piece
    pos = np.arange(ATT_QBLOCK, dtype=np.float32)
    dist = np.abs(pos[:, None] - pos[None, :])
    diag = -(slope2[:, None, None] * dist[None])
    return slope2, qaug, kaug, ktab, diag.astype(np.float32)


def _swa_bias_np(slopes):
    key = np.arange(3 * SWA_BLOCK, dtype=np.float32)[:, None]
    qry = np.arange(SWA_BLOCK, dtype=np.float32)[None, :]
    dist = np.abs(key - SWA_BLOCK - qry)
    bias = -(slopes.astype(np.float32)[:, None, None] * dist[None])
    bias = np.where(dist[None] <= WINDOW, bias, np.float32(NEG_INF)).astype(np.float32)
    return np.concatenate(list(bias), axis=1)


def _rms(x, g):
    return x * lax.rsqrt(jnp.mean(x * x, axis=-1, keepdims=True) + EPS) * g


def _dot(a, b):
    return jnp.dot(a, b, preferred_element_type=F32)


def _dot_nt(a, b):
    return lax.dot_general(a, b, (((1,), (1,)), ((), ())), preferred_element_type=F32)


def _dot_tn(a, b):
    return lax.dot_general(a, b, (((0,), (0,)), ((), ())), preferred_element_type=F32)


def _const_spec(shape):
    nd = len(shape)
    return pl.BlockSpec(shape, lambda *_: (0,) * nd, pipeline_mode=pl.Buffered(1))


def _params(sem):
    return pltpu.CompilerParams(dimension_semantics=sem, vmem_limit_bytes=VMEM_LIMIT_BYTES_V7X)


N_STD = DIFF_HEADS * KDIM + 2 * GMLP_WIDTH + SWA_KV_HEADS * HEAD_DIM
N_TR = DIFF_HEADS * KDIM + DIFF_HEADS * VT_ROWS + SWA_WIDTH + SWA_KV_HEADS * VT_ROWS
Q_SCALE = (DIFF_QK_DIM ** -0.5) * LOG2E


def _proj_kernel(x_ref, g_ref, wstd_ref, wtr_ref, kaug_ref, ktab_ref, lamp_ref,
                 k_ref, qt_ref, vt_ref, bu_ref, bv_ref, cqt_ref, ck_ref, cvt_ref, lam_ref, *, lam_init):
    tn = x_ref.shape[1]
    nsub = tn // ATT_BLOCK
    h = _rms(x_ref[0], g_ref[...]).astype(BF16)
    r1 = _dot(h, wstd_ref[...])
    for hh in range(DIFF_HEADS):
        kk = r1[:, hh * KDIM:(hh + 1) * KDIM] + kaug_ref[hh]
        for j in range(nsub):
            k_ref[0, hh, j] = (kk[j * ATT_BLOCK:(j + 1) * ATT_BLOCK] + ktab_ref[hh, j]).astype(BF16)
    o = DIFF_HEADS * KDIM
    bu_ref[0] = r1[:, o:o + GMLP_WIDTH]
    o += GMLP_WIDTH
    bv_ref[0] = r1[:, o:o + GMLP_WIDTH]
    o += GMLP_WIDTH
    ck_ref[0] = r1[:, o:o + SWA_KV_WIDTH].astype(BF16)

    r2 = _dot_nt(wtr_ref[...], h)
    ones_row = jnp.where(lax.broadcasted_iota(jnp.int32, (VT_ROWS, ATT_BLOCK), 0) == DIFF_V_DIM, 1.0, 0.0)
    vo = DIFF_HEADS * KDIM
    for hh in range(DIFF_HEADS):
        qt_ref[0, hh] = (r2[hh * KDIM:(hh + 1) * KDIM] * Q_SCALE).astype(BF16)
        vv = r2[vo + hh * VT_ROWS: vo + (hh + 1) * VT_ROWS]
        for j in range(nsub):
            vt_ref[0, hh, j] = (vv[:, j * ATT_BLOCK:(j + 1) * ATT_BLOCK] + ones_row).astype(BF16)
    o = vo + DIFF_HEADS * VT_ROWS
    cqt_ref[0] = r2[o:o + SWA_WIDTH].astype(BF16)
    o += SWA_WIDTH
    ones_rows = jnp.where(lax.broadcasted_iota(jnp.int32, (SWA_KV_HEADS * VT_ROWS, tn), 0) % VT_ROWS == HEAD_DIM,
                          1.0, 0.0)
    cvt_ref[0] = (r2[o:o + SWA_KV_HEADS * VT_ROWS] + ones_rows).astype(BF16)

    lp = lamp_ref[...]
    s1 = jnp.sum(lp[0:1] * lp[1:2], axis=-1, keepdims=True)
    s2 = jnp.sum(lp[2:3] * lp[3:4], axis=-1, keepdims=True)
    lam = jnp.exp(s1) - jnp.exp(s2) + lam_init
    lam_ref[...] = jnp.broadcast_to(lam, lam_ref.shape)


def _proj_call(x, g, wstd, wtr, kaug, ktab, lamp, lam_init):
    b, s, d = x.shape
    tn = PROJ_TILE
    nsub = tn // ATT_BLOCK
    nkb = s // ATT_BLOCK
    grid = (b, s // tn)
    tok = lambda bi, si: (bi, si, 0)
    out_shape = (
        jax.ShapeDtypeStruct((b, DIFF_HEADS, nkb, ATT_BLOCK, KDIM), BF16),
        jax.ShapeDtypeStruct((b, DIFF_HEADS, KDIM, s), BF16),
        jax.ShapeDtypeStruct((b, DIFF_HEADS, nkb, VT_ROWS, ATT_BLOCK), BF16),
        jax.ShapeDtypeStruct((b, s, GMLP_WIDTH), F32),
        jax.ShapeDtypeStruct((b, s, GMLP_WIDTH), F32),
        jax.ShapeDtypeStruct((b, SWA_WIDTH, s), BF16),
        jax.ShapeDtypeStruct((b, s, SWA_KV_WIDTH), BF16),
        jax.ShapeDtypeStruct((b, SWA_KV_HEADS * VT_ROWS, s), BF16),
        jax.ShapeDtypeStruct((8, LANES), F32),
    )
    out_specs = (
        pl.BlockSpec((1, DIFF_HEADS, nsub, ATT_BLOCK, KDIM), lambda bi, si: (bi, 0, si, 0, 0)),
        pl.BlockSpec((1, DIFF_HEADS, KDIM, tn), lambda bi, si: (bi, 0, 0, si)),
        pl.BlockSpec((1, DIFF_HEADS, nsub, VT_ROWS, ATT_BLOCK), lambda bi, si: (bi, 0, si, 0, 0)),
        pl.BlockSpec((1, tn, GMLP_WIDTH), tok),
        pl.BlockSpec((1, tn, GMLP_WIDTH), tok),
        pl.BlockSpec((1, SWA_WIDTH, tn), lambda bi, si: (bi, 0, si)),
        pl.BlockSpec((1, tn, SWA_KV_WIDTH), tok),
        pl.BlockSpec((1, SWA_KV_HEADS * VT_ROWS, tn), lambda bi, si: (bi, 0, si)),
        pl.BlockSpec((8, LANES), lambda bi, si: (0, 0)),
    )
    in_specs = [
        pl.BlockSpec((1, tn, d), tok),
        _const_spec(g.shape),
        _const_spec(wstd.shape),
        _const_spec(wtr.shape),
        _const_spec(kaug.shape),
        pl.BlockSpec((DIFF_HEADS, nsub, 1, KDIM), lambda bi, si: (0, si, 0, 0)),
        _const_spec(lamp.shape),
    ]
    return pl.pallas_call(
        functools.partial(_proj_kernel, lam_init=lam_init),
        grid=grid, in_specs=in_specs, out_specs=out_specs, out_shape=out_shape,
        compiler_params=_params(("arbitrary", "arbitrary")),
        name="proj",
    )(x, g, wstd, wtr, kaug, ktab, lamp)


def _diff_attn_kernel(slope_ref, qt_ref, k_ref, vt_ref, qaug_ref, diag_ref, lam_ref, gsub_ref,
                      o_ref, w_ref, s0_ref, s1_ref, p0_ref, p1_ref, acc_ref, *, out_scale):
    hh = pl.program_id(1)
    qi = pl.program_id(2)
    blk = ATT_BLOCK
    tq = ATT_QBLOCK
    n_rest = k_ref.shape[2] // 2 - 1
    sigma = slope_ref[hh] * float(blk)

    qt = qt_ref[0, 0].astype(F32)
    qaug = qaug_ref[0]
    row = lax.broadcasted_iota(jnp.int32, (KDIM, tq), 0)
    comp_mask = (row < COMP_LANES, row >= COMP_LANES)
    zero = jnp.zeros_like(qt)
    for c in range(2):
        w_ref[0, c] = jnp.where(comp_mask[c], qt - qaug, zero).astype(BF16)
        w_ref[1, c] = jnp.where(comp_mask[c], qt + qaug, zero).astype(BF16)

    def key_rows(kb):
        return jnp.concatenate([k_ref[0, 0, 2 * kb], k_ref[0, 0, 2 * kb + 1]], axis=0)

    def value_cols(kb):
        return jnp.concatenate([vt_ref[0, 0, 2 * kb], vt_ref[0, 0, 2 * kb + 1]], axis=1)

    k_t = key_rows(qi)
    vt = value_cols(qi)
    m = []
    for c in range(2):
        w_c = jnp.where(comp_mask[c], qt, zero).astype(BF16)
        m_parts = []
        for n in range(tq // blk):
            cols = slice(n * blk, (n + 1) * blk)
            s = _dot(k_t, w_c[:, cols]) + diag_ref[0, :, cols]
            m_cn = jnp.max(s, axis=0, keepdims=True)
            acc_ref[c, :, cols] = _dot(vt, jnp.exp2((s - m_cn).astype(BF16)))
            m_parts.append(m_cn)
        m.append(jnp.concatenate(m_parts, axis=1))

    dyn_row = lax.broadcasted_iota(jnp.int32, (DYN_ROWS, tq), 0)
    lane_row = lax.broadcasted_iota(jnp.int32, (1, tq), 1)
    q_origin = sigma * jnp.where(lane_row < blk, 2 * qi, 2 * qi + 1).astype(F32)

    def rest_block(j):
        after = (j >= qi).astype(jnp.int32)
        return j + after, after

    def reference_rows(rr, key_sign):
        hi = rr.astype(BF16).astype(F32)
        mid = (rr - hi).astype(BF16).astype(F32)
        lo = rr - hi - mid
        rows = jnp.where(dyn_row == 0, hi, jnp.where(dyn_row == 1, mid, jnp.where(dyn_row == 2, lo, 0.0)))
        rows = jnp.where((dyn_row >= 3) & (dyn_row < 6), key_sign, rows)
        return rows.astype(BF16)

    def column_max(z):
        parts = [z[i * 16:(i + 1) * 16] for i in range(z.shape[0] // 16)]
        while len(parts) > 1:
            parts = [jnp.maximum(parts[i], parts[i + 1]) for i in range(0, len(parts), 2)]
        return jnp.max(parts[0].astype(F32), axis=0, keepdims=True)

    tiles = [slice(n * blk, (n + 1) * blk) for n in range(tq // blk)]
    s_bufs = (s0_ref, s1_ref)
    p_bufs = (p0_ref, p1_ref)
    alphas = {}
    maxes = {-2: m, -1: m}
    for t in range(n_rest + 2):
        do_scores = t < n_rest
        do_pv = t >= 2
        if do_scores:
            kb_s, side = rest_block(t)
            sgn = jnp.where(side == 1, 1.0, -1.0)
            k_t = key_rows(kb_s)
        if do_pv:
            kb_v, _ = rest_block(t - 2)
            vt = value_cols(kb_v)
        for c in range(2):
            if do_scores:
                dyn = reference_rows(sgn * q_origin - maxes[t - 2][c], -sgn)
                w = jnp.concatenate([w_ref[side, c, 0:DYN_ROW0, :], dyn], axis=0)
            for n, cols in enumerate(tiles):
                if do_scores:
                    s_bufs[t % 2][c, :, cols] = _dot(k_t, w[:, cols]).astype(BF16)
                if do_pv:
                    acc_ref[c, :, cols] = (acc_ref[c, :, cols] * alphas[t - 2][c][:, cols]
                                           + _dot(vt, p_bufs[t % 2][c, :, cols]))
        u = t - 1
        if 0 <= u < n_rest:
            m2, m1 = maxes[u - 2], maxes[u - 1]
            al_u, m_u = [], []
            for c in range(2):
                z = s_bufs[u % 2][c]
                d = jnp.maximum(m1[c] - m2[c], column_max(z)).astype(BF16)
                m_new = m2[c] + d.astype(F32)
                p_bufs[u % 2][c] = jnp.exp2(z - d)
                al_u.append(jnp.exp2(m1[c] - m_new))
                m_u.append(m_new)
            alphas[u], maxes[u] = al_u, m_u

    a0 = acc_ref[0]
    a1 = acc_ref[1]
    o0 = a0[0:DIFF_V_DIM] / a0[DIFF_V_DIM:DIFF_V_DIM + 1]
    o1 = a1[0:DIFF_V_DIM] / a1[DIFF_V_DIM:DIFF_V_DIM + 1]
    lam = lam_ref[0:1, 0:1]
    o = o0 - lam * o1
    ms = jnp.mean(o * o, axis=0, keepdims=True)
    y = o * lax.rsqrt(ms + EPS) * gsub_ref[...] * out_scale
    o_ref[0] = y.astype(o_ref.dtype)


def _diff_attn_call(slope2, qt, kblk, vtblk, qaug, diag, lam_tile, gsub_b, out_scale):
    b, nh, kdim, s = qt.shape
    nkb = kblk.shape[2]
    blk = ATT_BLOCK
    tq = ATT_QBLOCK
    grid = (b, nh, s // tq)
    in_specs = [
        pl.BlockSpec(memory_space=pltpu.SMEM),
        pl.BlockSpec((1, 1, kdim, tq), lambda bi, hi, qi: (bi, hi, 0, qi)),
        pl.BlockSpec((1, 1, nkb, blk, kdim), lambda bi, hi, qi: (bi, hi, 0, 0, 0)),
        pl.BlockSpec((1, 1, nkb, VT_ROWS, blk), lambda bi, hi, qi: (bi, hi, 0, 0, 0)),
        pl.BlockSpec((1, kdim, tq), lambda bi, hi, qi: (hi, 0, 0)),
        pl.BlockSpec((1, tq, tq), lambda bi, hi, qi: (hi, 0, 0)),
        _const_spec(lam_tile.shape),
        _const_spec(gsub_b.shape),
    ]
    return pl.pallas_call(
        functools.partial(_diff_attn_kernel, out_scale=out_scale),
        grid=grid, in_specs=in_specs,
        out_specs=pl.BlockSpec((1, DIFF_V_DIM, tq), lambda bi, hi, qi: (bi, hi, qi)),
        out_shape=jax.ShapeDtypeStruct((b, nh * DIFF_V_DIM, s), BF16),
        scratch_shapes=[pltpu.VMEM((2, 2, kdim, tq), BF16),
                        *([pltpu.VMEM((2, tq, tq), BF16)] * 4),
                        pltpu.VMEM((2, VT_ROWS, tq), F32)],
        compiler_params=_params(("arbitrary", "arbitrary", "arbitrary")),
        name="diff_attn",
    )(slope2, qt, kblk, vtblk, qaug, diag, lam_tile, gsub_b)


def _swa_kernel(sink_ref, qt_ref, kp_ref, kc_ref, kn_ref, vp_ref, vc_ref, vn_ref, bias_ref, o_ref, *, seq_len):
    qi = pl.program_id(1)
    blk = SWA_BLOCK
    nsub = SWA_TILE // blk
    k_all = jnp.concatenate([kp_ref[0], kc_ref[0], kn_ref[0]], axis=0)
    vt_all = jnp.concatenate([vp_ref[0], vc_ref[0], vn_ref[0]], axis=1)
    key_row = lax.broadcasted_iota(jnp.int32, (3 * blk, SWA_HEADS * blk), 0)
    no_q = jnp.zeros((HEAD_DIM, SWA_GROUP * blk), BF16)
    sink = sink_ref[...]
    gw = SWA_GROUP * blk
    for sub in range(nsub):
        key_pos = (qi * nsub + sub - 1) * blk + key_row
        in_seq = (key_pos >= 0) & (key_pos < seq_len)
        kk = k_all[sub * blk:(sub + 3) * blk]
        q_t = [qt_ref[0, hq * HEAD_DIM:(hq + 1) * HEAD_DIM, sub * blk:(sub + 1) * blk] for hq in range(SWA_HEADS)]
        w = jnp.concatenate([jnp.concatenate(q_t[:SWA_GROUP] + [no_q], axis=1),
                             jnp.concatenate([no_q] + q_t[SWA_GROUP:], axis=1)], axis=0)
        sc = _dot(kk, w) + bias_ref[...]
        sc = jnp.where(in_seq, sc, NEG_INF)
        m = jnp.maximum(jnp.max(sc, axis=0, keepdims=True), sink)
        e = jnp.exp(sc - m).astype(BF16)
        tail = jnp.exp(sink - m)
        for kh in range(SWA_KV_HEADS):
            vt = vt_all[kh * VT_ROWS:(kh + 1) * VT_ROWS, sub * blk:(sub + 3) * blk]
            acc = _dot(vt, e[:, kh * gw:(kh + 1) * gw])
            o = acc[0:HEAD_DIM] / (acc[HEAD_DIM:HEAD_DIM + 1] + tail[:, kh * gw:(kh + 1) * gw])
            for g in range(SWA_GROUP):
                hq = kh * SWA_GROUP + g
                o_ref[0, hq * HEAD_DIM:(hq + 1) * HEAD_DIM, sub * blk:(sub + 1) * blk] = (
                    o[:, g * blk:(g + 1) * blk].astype(o_ref.dtype))


def _swa_in_specs(sinks, bias, s):
    blk = SWA_BLOCK
    tile = SWA_TILE
    nsub = tile // blk
    nb = s // blk
    prev_i = lambda qi: jnp.maximum(qi * nsub - 1, 0)
    next_i = lambda qi: jnp.minimum((qi + 1) * nsub, nb - 1)
    vrows = SWA_KV_HEADS * VT_ROWS
    return [
        _const_spec(sinks.shape),
        pl.BlockSpec((1, SWA_WIDTH, tile), lambda bi, qi: (bi, 0, qi)),
        pl.BlockSpec((1, blk, SWA_KV_WIDTH), lambda bi, qi: (bi, prev_i(qi), 0)),
        pl.BlockSpec((1, tile, SWA_KV_WIDTH), lambda bi, qi: (bi, qi, 0)),
        pl.BlockSpec((1, blk, SWA_KV_WIDTH), lambda bi, qi: (bi, next_i(qi), 0)),
        pl.BlockSpec((1, vrows, blk), lambda bi, qi: (bi, 0, prev_i(qi))),
        pl.BlockSpec((1, vrows, tile), lambda bi, qi: (bi, 0, qi)),
        pl.BlockSpec((1, vrows, blk), lambda bi, qi: (bi, 0, next_i(qi))),
        _const_spec(bias.shape),
    ]


N_SWA_REFS = 9


def _mix_out_body(x_ref, yat_ref, bu_ref, bv_ref, yct_ref, lng_ref, lnb_ref, ws_ref, bs_ref,
                  wa_ref, wb_ref, wc_ref, g_ref):
    tn = x_ref.shape[1]
    v = bv_ref[0]
    mu = jnp.mean(v, axis=-1, keepdims=True)
    var = jnp.mean(jnp.square(v - mu), axis=-1, keepdims=True)
    vn = ((v - mu) * lax.rsqrt(var + EPS) * lng_ref[...] + lnb_ref[...]).astype(BF16)
    lane_group = lax.broadcasted_iota(jnp.int32, (CHUNK, GMLP_WIDTH), 1) // GMLP_GROUP_DIM
    u = bu_ref[0]
    yb = []
    for c in range(tn // CHUNK):
        vc = vn[c * CHUNK:(c + 1) * CHUNK]
        mixed = bs_ref[...]
        for g in range(GMLP_GROUPS):
            mixed = mixed + jnp.where(lane_group == g, _dot(ws_ref[g], vc), 0.0)
        yb.append(u[c * CHUNK:(c + 1) * CHUNK] * mixed)
    yb = jnp.concatenate(yb, axis=0).astype(BF16)
    y = _dot_tn(yat_ref[0], wa_ref[...]) + _dot(yb, wb_ref[...]) + _dot_tn(yct_ref[0], wc_ref[...])
    return x_ref[0] + _rms(y, g_ref[...])


def _sigmoid(z):
    return 1.0 / (1.0 + jnp.exp(-z))


def _ffn_ple_body(x, p_ref, gpre_ref, wg_ref, wu_ref, wo_ref, gpost_ref,
                  wup_ref, wgate_ref, ggate_ref, gple_ref):
    h = _rms(x, gpre_ref[...]).astype(BF16)
    f = jnp.zeros(x.shape, F32)
    for lo in range(0, D_FF, FF_CHUNK):
        sl = slice(lo, min(lo + FF_CHUNK, D_FF))
        gate = _dot(h, wg_ref[:, sl])
        up = _dot(h, wu_ref[:, sl])
        a = (gate * _sigmoid(gate) * up).astype(BF16)
        f = f + _dot(a, wo_ref[sl, :])
    x = x + _rms(f, gpost_ref[...])
    e = _dot(p_ref[0, 0].astype(BF16), wup_ref[...])
    gt = _sigmoid(_dot(_rms(x, ggate_ref[...]).astype(BF16), wgate_ref[...]))
    return x + _rms(e * gt, gple_ref[...])


N_MIX_CONSTS = 8


def _channel_kernel(*refs, seq_len):
    swa_in = refs[:N_SWA_REFS]
    x_ref, yat_ref, bu_ref, bv_ref = refs[N_SWA_REFS:N_SWA_REFS + 4]
    mix_consts = refs[N_SWA_REFS + 4:N_SWA_REFS + 4 + N_MIX_CONSTS]
    ffn_in = refs[N_SWA_REFS + 4 + N_MIX_CONSTS:-2]
    o_ref, yct_ref = refs[-2], refs[-1]
    _swa_kernel(*swa_in, yct_ref, seq_len=seq_len)
    x_mid = _mix_out_body(x_ref, yat_ref, bu_ref, bv_ref, yct_ref, *mix_consts)
    o_ref[0] = _ffn_ple_body(x_mid, *ffn_in)


def _channel_call(x, yat, bu, bv, swa_args, mix_consts, p, layer, ffn_consts):
    b, s, d = x.shape
    tn = TOKEN_TILE
    assert SWA_TILE == tn and len(mix_consts) == N_MIX_CONSTS
    sinks, cqt, ck, cvt, bias = swa_args
    tok = lambda bi, si: (bi, si, 0)
    in_specs = _swa_in_specs(sinks, bias, s) + [
        pl.BlockSpec((1, tn, d), tok),
        pl.BlockSpec((1, DIFF_WIDTH, tn), lambda bi, si: (bi, 0, si)),
        pl.BlockSpec((1, tn, GMLP_WIDTH), tok),
        pl.BlockSpec((1, tn, GMLP_WIDTH), tok),
    ] + [_const_spec(a.shape) for a in mix_consts] + [
        pl.BlockSpec((1, 1, tn, PLE_DIM), lambda bi, si: (layer, bi, si, 0)),
    ] + [_const_spec(a.shape) for a in ffn_consts]
    return pl.pallas_call(
        functools.partial(_channel_kernel, seq_len=s),
        grid=(b, s // tn), in_specs=in_specs,
        out_specs=pl.BlockSpec((1, tn, d), tok),
        out_shape=jax.ShapeDtypeStruct((b, s, d), F32),
        scratch_shapes=[pltpu.VMEM((1, SWA_WIDTH, tn), BF16)],
        compiler_params=_params(("arbitrary", "arbitrary")),
        name="swa_mix_ffn_ple",
    )(sinks, cqt, ck, ck, ck, cvt, cvt, cvt, bias, x, yat, bu, bv, *mix_consts, p, *ffn_consts)


def _prep_in_weights(w):
    d = w.shape[0]
    aq = w[:, 0:384].reshape(d, DIFF_HEADS, 2, DIFF_QK_DIM)
    ak = w[:, 384:768].reshape(d, DIFF_HEADS, 2, DIFF_QK_DIM)
    av = w[:, 768:1152].reshape(d, DIFF_HEADS, DIFF_V_DIM)
    pad_qk = ((0, 0), (0, 0), (0, 0), (0, COMP_LANES - DIFF_QK_DIM))
    wk = jnp.pad(ak, pad_qk).reshape(d, DIFF_HEADS * KDIM)
    wq = jnp.pad(aq, pad_qk).reshape(d, DIFF_HEADS * KDIM)
    wv = jnp.pad(av, ((0, 0), (0, 0), (0, VT_ROWS - DIFF_V_DIM))).reshape(d, DIFF_HEADS * VT_ROWS)
    cq = w[:, 1664:2048] * (HEAD_DIM ** -0.5)
    cv = w[:, 2176:2304].reshape(d, SWA_KV_HEADS, HEAD_DIM)
    cv = jnp.pad(cv, ((0, 0), (0, 0), (0, VT_ROWS - HEAD_DIM))).reshape(d, SWA_KV_HEADS * VT_ROWS)
    wstd = jnp.concatenate([wk, w[:, 1152:1664], w[:, 2048:2176]], axis=1).astype(BF16)
    wtr = jnp.concatenate([wq, wv, cq, cv], axis=1).T.astype(BF16)
    return wstd, wtr


def kernel(x, p, g_pre_mix, w_in, lam_q1, lam_k1, lam_q2, lam_k2, g_diff_sub, gmlp_ln_g, gmlp_ln_b,
           w_spatial, b_spatial, swa_sinks, w_out, g_post_mix, g_pre_ffn, w_ffn_in, w_ffn_out,
           g_post_ffn, w_ple_up, w_ple_gate, g_ple_gate, g_ple_post):
    b, s, d = x.shape
    depth = w_in.shape[0]
    assert d == D_MODEL and s % ATT_QBLOCK == 0 and s >= 2 * ATT_QBLOCK and s % TOKEN_TILE == 0

    slope2_np, qaug_np, kaug_np, ktab_np, diag_np = _diff_bias_constants(PROJ_TILE, s)
    _, swa_slopes_np = _alibi_slopes_np()
    slope2 = jnp.asarray(slope2_np)
    qaug = jnp.asarray(qaug_np)
    kaug = jnp.asarray(kaug_np)
    ktab = jnp.asarray(ktab_np)
    diag = jnp.asarray(diag_np)
    swa_bias = jnp.asarray(_swa_bias_np(swa_slopes_np))
    row = lambda a: a.reshape(1, -1).astype(F32)

    for l in range(depth):
        lam_init = 0.8 - 0.6 * math.exp(-0.3 * l)
        wstd, wtr = _prep_in_weights(w_in[l])
        lamp = jnp.stack([lam_q1[l], lam_k1[l], lam_q2[l], lam_k2[l]]).astype(F32)
        kblk, qt, vtblk, bu, bv, cq, ck, cv, lam_tile = _proj_call(
            x, row(g_pre_mix[l]), wstd, wtr, kaug, ktab, lamp, lam_init)

        gsub_b = jnp.broadcast_to(g_diff_sub[l].astype(F32)[:, None], (DIFF_V_DIM, ATT_QBLOCK))
        yat = _diff_attn_call(slope2, qt, kblk, vtblk, qaug, diag, lam_tile, gsub_b, 1.0 - lam_init)
        sink_row = jnp.repeat(swa_sinks[l].astype(F32), SWA_BLOCK).reshape(1, SWA_HEADS * SWA_BLOCK)

        bs = jnp.broadcast_to(b_spatial[l].T[:, :, None], (CHUNK, GMLP_GROUPS, GMLP_GROUP_DIM))
        bs = bs.reshape(CHUNK, GMLP_WIDTH).astype(F32)
        wo = w_out[l].astype(BF16)
        mix_consts = (row(gmlp_ln_g[l]), row(gmlp_ln_b[l]), w_spatial[l].astype(BF16), bs,
                      wo[0:DIFF_WIDTH], wo[DIFF_WIDTH:DIFF_WIDTH + GMLP_WIDTH], wo[DIFF_WIDTH + GMLP_WIDTH:],
                      row(g_post_mix[l]))
        wfi = w_ffn_in[l].astype(BF16)
        ffn_consts = (row(g_pre_ffn[l]), wfi[:, :D_FF], wfi[:, D_FF:], w_ffn_out[l].astype(BF16),
                      row(g_post_ffn[l]), w_ple_up[l].astype(BF16), w_ple_gate[l].astype(BF16),
                      row(g_ple_gate[l]), row(g_ple_post[l]))
        x = _channel_call(x, yat, bu, bv, (sink_row, cq, ck, cv, swa_bias), mix_consts, p, l, ffn_consts)
    return x
```
